```python
import math
import jax, jax.numpy as jnp
from jax import lax
import numpy as np

D_MODEL = 1024
BATCH = 1
SEQ = 16384
DEPTH = 4
DEC_BATCH = 2
DEC_SEQ = 8192
PAST_LEN = 128

HEAD_DIM = 64
A_HEADS = D_MODEL // (2 * HEAD_DIM)
A_KV_HEADS = max(1, A_HEADS // 4)
A_WINDOW = 128
B_HEADS = D_MODEL // (2 * HEAD_DIM)
B_BRANCHES = ((128, 1), (512, 4), (2048, 16))
C_HEADS = D_MODEL // HEAD_DIM
GRID_W = 64
NA_ROWS = 8
NA_COLS = 16
REL_BUCKETS = 32
REL_MAX_DIST = 1024
N_GROUPS = 4
EXPERTS_PER_GROUP = 4
N_EXPERTS = N_GROUPS * EXPERTS_PER_GROUP
EXPERT_TOP_K = 2
D_EXPERT = D_MODEL // 2
N_EVEN = (DEPTH + 1) // 2
N_ODD = DEPTH // 2
DEEPNORM_ALPHA = (2.0 * DEPTH) ** 0.25
DEEPNORM_BETA = (8.0 * DEPTH) ** -0.25
LN_EPS = 1e-5
ATTN_SCALE = HEAD_DIM ** -0.5
NEG_INF = -1e30
QA_W = A_HEADS * HEAD_DIM
KVA_W = A_KV_HEADS * HEAD_DIM
B_W = B_HEADS * HEAD_DIM
AB_IN = QA_W + 2 * KVA_W + 3 * B_W
MIX_W = QA_W + B_W
C_W = C_HEADS * HEAD_DIM

kernel_name = 'hybrid_bidir_encoder_window_dilated_na_hmoe'


def layer_norm(x, g, b):
    xf = x.astype(jnp.float32)
    mu = jnp.mean(xf, axis=-1, keepdims=True)
    var = jnp.mean(jnp.square(xf - mu), axis=-1, keepdims=True)
    y = (xf - mu) * lax.rsqrt(var + LN_EPS) * g.astype(jnp.float32) + b.astype(jnp.float32)
    return y.astype(x.dtype)


def t5_bucket(rel):
    half_b = REL_BUCKETS // 2
    max_exact = half_b // 2
    n = jnp.abs(rel)
    large = max_exact + (jnp.log(jnp.maximum(n, max_exact).astype(jnp.float32) / max_exact)
                         / math.log(REL_MAX_DIST / max_exact) * (half_b - max_exact)).astype(jnp.int32)
    large = jnp.minimum(large, half_b - 1)
    return jnp.where(rel > 0, half_b, 0) + jnp.where(n < max_exact, n, large)


def banded_bias(table, half, dil):
    rel = (jnp.arange(3 * half)[None, :] - half - jnp.arange(half)[:, None]) * dil
    return jnp.transpose(table[t5_bucket(rel)], (2, 0, 1))


def banded_attention(q, k, v, bias, half, n_valid, sink=None):
    n, L, hq, dh = q.shape
    hk = k.shape[2]
    rep = hq // hk
    nb = L // half
    qb = q.reshape(n, nb, half, hk, rep, dh)

    def blocks3(t):
        tp = jnp.pad(t, ((0, 0), (half, half), (0, 0), (0, 0)))
        tb = tp.reshape(n, nb + 2, half, hk, dh)
        return jnp.concatenate([tb[:, :-2], tb[:, 1:-1], tb[:, 2:]], axis=2)

    kb = blocks3(k)
    vb = blocks3(v)
    s = jnp.einsum('nbqgrd,nbkgd->nbgrqk', qb, kb, preferred_element_type=jnp.float32) * ATTN_SCALE
    s = s + bias.reshape(hk, rep, half, 3 * half).astype(jnp.float32)
    qi = jnp.arange(half)
    kj = jnp.arange(3 * half)
    rel = kj[None, :] - half - qi[:, None]
    kpos = (jnp.arange(nb) * half)[:, None] + kj[None, :] - half
    mask = (jnp.abs(rel) <= half)[None] & ((kpos >= 0) & (kpos < n_valid))[:, None, :]
    s = jnp.where(mask[None, :, None, None], s, NEG_INF)
    m = jnp.max(s, axis=-1, keepdims=True)
    if sink is not None:
        sk = sink.astype(jnp.float32).reshape(hk, rep, 1, 1)
        m = jnp.maximum(m, sk)
    e = jnp.exp(s - m)
    den = jnp.sum(e, axis=-1, keepdims=True)
    if sink is not None:
        den = den + jnp.exp(sk - m)
    o = jnp.einsum('nbgrqk,nbkgd->nbgrqd', e, vb.astype(jnp.float32)) / den
    lse = (m + jnp.log(den))[..., 0]
    o = jnp.transpose(o, (0, 1, 4, 2, 3, 5)).reshape(n, L, hq, dh)
    lse = jnp.transpose(lse, (0, 1, 4, 2, 3)).reshape(n, L, hq)
    return o, lse


def dilated_mixture(q, k, v, table):
    n, t, h, dh = q.shape
    outs = []
    lses = []
    for window, dil in B_BRANCHES:
        half = (window // 2) // dil
        L = t // dil
        Lp = -(-L // half) * half

        def to_res(a):
            a = jnp.transpose(a.reshape(n, L, dil, h, dh), (0, 2, 1, 3, 4)).reshape(n * dil, L, h, dh)
            return jnp.pad(a, ((0, 0), (0, Lp - L), (0, 0), (0, 0)))

        o, lse = banded_attention(to_res(q), to_res(k), to_res(v), banded_bias(table, half, dil), half, L)
        o = jnp.transpose(o[:, :L].reshape(n, dil, L, h, dh), (0, 2, 1, 3, 4)).reshape(n, t, h, dh)
        lse = jnp.transpose(lse[:, :L].reshape(n, dil, L, h), (0, 2, 1, 3)).reshape(n, t, h)
        outs.append(o)
        lses.append(lse)
    w = jax.nn.softmax(jnp.stack(lses), axis=0)
    return jnp.sum(w[..., None] * jnp.stack(outs), axis=0)


def neighbourhood_attention(q, k, v, rpb):
    n, t, h, dh = q.shape
    rows = t // GRID_W
    kr = min(NA_ROWS, rows)
    kc = NA_COLS
    qg = q.reshape(n, rows, GRID_W, h, dh)
    kg = k.reshape(n, rows, GRID_W, h, dh)
    vg = v.reshape(n, rows, GRID_W, h, dh)
    r = jnp.arange(rows)
    row_idx = jnp.clip(r - kr // 2, 0, rows - kr)[:, None] + jnp.arange(kr)[None, :]
    kn = kg[:, row_idx]
    vn = vg[:, row_idx]
    c = jnp.arange(GRID_W)
    c0 = jnp.clip(c - kc // 2, 0, GRID_W - kc)
    col_ok = (c[None, :] >= c0[:, None]) & (c[None, :] < c0[:, None] + kc)
    d_row = row_idx - r[:, None] + NA_ROWS - 1
    d_col = jnp.clip(c[None, :] - c[:, None] + kc - 1, 0, 2 * kc - 2)
    bias = rpb[:, d_row[:, None, :, None], d_col[None, :, None, :]]
    s = jnp.einsum('nrchd,nrjwhd->nhrcjw', qg, kn, preferred_element_type=jnp.float32) * ATTN_SCALE
    s = s + bias.astype(jnp.float32)
    s = jnp.where(col_ok[:, None, :], s, NEG_INF)
    p = jax.nn.softmax(s, axis=(-2, -1))
    o = jnp.einsum('nhrcjw,nrjwhd->nrchd', p, vn.astype(jnp.float32))
    return o.reshape(n, t, h, dh)


def ab_mixer(x, w_in, sink, w_out, rel_bias):
    n, t, _ = x.shape
    hcat = x @ w_in
    cuts = [QA_W, QA_W + KVA_W, QA_W + 2 * KVA_W, QA_W + 2 * KVA_W + B_W, QA_W + 2 * KVA_W + 2 * B_W]
    qa, ka, va, qb, kb, vb = jnp.split(hcat, cuts, axis=-1)
    heads = lambda a, nh: a.reshape(n, t, nh, HEAD_DIM)
    oa, _ = banded_attention(heads(qa, A_HEADS), heads(ka, A_KV_HEADS), heads(va, A_KV_HEADS),
                             banded_bias(rel_bias[:, :A_HEADS], A_WINDOW, 1), A_WINDOW, t, sink)
    ob = dilated_mixture(heads(qb, B_HEADS), heads(kb, B_HEADS), heads(vb, B_HEADS), rel_bias[:, A_HEADS:])
    o = jnp.concatenate([oa.reshape(n, t, QA_W), ob.reshape(n, t, B_W)], axis=-1).astype(x.dtype)
    return o @ w_out


def c_mixer(x, w_in, rpb, w_out):
    n, t, _ = x.shape
    q, k, v = jnp.split(x @ w_in, 3, axis=-1)
    heads = lambda a: a.reshape(n, t, C_HEADS, HEAD_DIM)
    o = neighbourhood_attention(heads(q), heads(k), heads(v), rpb)
    return o.reshape(n, t, C_W).astype(x.dtype) @ w_out


def hier_moe(x, wg, bg, we, be, w_gate, w_up, w_down):
    n, t, d = x.shape
    xt = x.reshape(n * t, d)
    lg = (xt @ wg).astype(jnp.float32) + bg.astype(jnp.float32)
    g_sel = jnp.argmax(lg, axis=-1)
    g_gate = jnp.take_along_axis(jax.nn.softmax(lg, axis=-1), g_sel[:, None], axis=-1)
    le = ((xt @ we).astype(jnp.float32) + be.astype(jnp.float32)).reshape(-1, N_GROUPS, EXPERTS_PER_GROUP)
    le_sel = jnp.take_along_axis(le, g_sel[:, None, None], axis=1)[:, 0]
    tv, ti = lax.top_k(le_sel, EXPERT_TOP_K)
    tw = jax.nn.softmax(tv, axis=-1) * g_gate
    w_e = jnp.sum(jax.nn.one_hot(ti, EXPERTS_PER_GROUP, dtype=jnp.float32) * tw[..., None], axis=1)
    comb = (jax.nn.one_hot(g_sel, N_GROUPS, dtype=jnp.float32)[:, :, None] * w_e[:, None, :])
    comb = comb.reshape(-1, N_EXPERTS).astype(x.dtype)
    y = jnp.zeros_like(xt)
    for e in range(N_EXPERTS):
        hid = jax.nn.silu(xt @ w_gate[e]) * (xt @ w_up[e])
        y = y + comb[:, e:e + 1] * (hid @ w_down[e])
    return y.reshape(n, t, d)


def encoder_trunk(x, rel_bias, w_in_ab, a_sink, w_out_ab, w_in_c, c_rpb, w_out_c,
                  ln1_g, ln1_b, ln2_g, ln2_b, router_g_w, router_g_b, router_e_w, router_e_b,
                  w_gate, w_up, w_down):
    for l in range(DEPTH):
        i = l // 2
        if l % 2 == 0:
            h = ab_mixer(x, w_in_ab[i], a_sink[i], w_out_ab[i], rel_bias)
        else:
            h = c_mixer(x, w_in_c[i], c_rpb[i], w_out_c[i])
        x = layer_norm(DEEPNORM_ALPHA * x + h, ln1_g[l], ln1_b[l])
        f = hier_moe(x, router_g_w[l], router_g_b[l], router_e_w[l], router_e_b[l],
                     w_gate[l], w_up[l], w_down[l])
        x = layer_norm(DEEPNORM_ALPHA * x + f, ln2_g[l], ln2_b[l])
    return x


def setup_inputs(seed: int = 0) -> dict:
    key = jax.random.key(seed)
    ks = jax.random.split(key, 24)
    f32 = jnp.float32

    def nrm(k, shape, scale):
        return jax.random.normal(k, shape, f32) * scale

    return {
        'x_prompt': nrm(ks[0], (BATCH, SEQ, D_MODEL), 1.0),
        'x_sample': nrm(ks[1], (DEC_BATCH, DEC_SEQ, D_MODEL), 1.0),
        'rel_bias': nrm(ks[2], (REL_BUCKETS, A_HEADS + B_HEADS), 0.5),
        'w_in_ab': nrm(ks[3], (N_EVEN, D_MODEL, AB_IN), D_MODEL ** -0.5),
        'a_sink': nrm(ks[4], (N_EVEN, A_HEADS), 1.0),
        'w_out_ab': nrm(ks[5], (N_EVEN, MIX_W, D_MODEL), DEEPNORM_BETA * MIX_W ** -0.5),
        'w_in_c': nrm(ks[6], (N_ODD, D_MODEL, 3 * C_W), D_MODEL ** -0.5),
        'c_rpb': nrm(ks[7], (N_ODD, C_HEADS, 2 * NA_ROWS - 1, 2 * NA_COLS - 1), 0.5),
        'w_out_c': nrm(ks[8], (N_ODD, C_W, D_MODEL), DEEPNORM_BETA * C_W ** -0.5),
        'ln1_g': 1.0 + nrm(ks[9], (DEPTH, D_MODEL), 0.02),
        'ln1_b': nrm(ks[10], (DEPTH, D_MODEL), 0.02),
        'ln2_g': 1.0 + nrm(ks[11], (DEPTH, D_MODEL), 0.02),
        'ln2_b': nrm(ks[12], (DEPTH, D_MODEL), 0.02),
        'router_g_w': nrm(ks[13], (DEPTH, D_MODEL, N_GROUPS), D_MODEL ** -0.5),
        'router_g_b': nrm(ks[14], (DEPTH, N_GROUPS), 0.01),
        'router_e_w': nrm(ks[15], (DEPTH, D_MODEL, N_EXPERTS), D_MODEL ** -0.5),
        'router_e_b': nrm(ks[16], (DEPTH, N_EXPERTS), 0.01),
        'w_gate': nrm(ks[17], (DEPTH, N_EXPERTS, D_MODEL, D_EXPERT), D_MODEL ** -0.5),
        'w_up': nrm(ks[18], (DEPTH, N_EXPERTS, D_MODEL, D_EXPERT), D_MODEL ** -0.5),
        'w_down': nrm(ks[19], (DEPTH, N_EXPERTS, D_EXPERT, D_MODEL), DEEPNORM_BETA * D_EXPERT ** -0.5),
    }


def reference(x_prompt, x_sample, rel_bias, w_in_ab, a_sink, w_out_ab, w_in_c, c_rpb, w_out_c,
              ln1_g, ln1_b, ln2_g, ln2_b, router_g_w, router_g_b, router_e_w, router_e_b,
              w_gate, w_up, w_down):
    y_prompt = encoder_trunk(x_prompt, rel_bias, w_in_ab, a_sink, w_out_ab, w_in_c, c_rpb, w_out_c,
                             ln1_g, ln1_b, ln2_g, ln2_b, router_g_w, router_g_b, router_e_w, router_e_b,
                             w_gate, w_up, w_down)
    y_sample = encoder_trunk(x_sample, rel_bias, w_in_ab, a_sink, w_out_ab, w_in_c, c_rpb, w_out_c,
                             ln1_g, ln1_b, ln2_g, ln2_b, router_g_w, router_g_b, router_e_w, router_e_b,
                             w_gate, w_up, w_down)
    return (y_prompt, y_sample)
```

```python
import functools
import math

import numpy as np
import jax
import jax.numpy as jnp
from jax import lax
from jax.experimental import pallas as pl
from jax.experimental.pallas import tpu as pltpu

F32 = jnp.float32
BF16 = jnp.bfloat16

D_MODEL = 1024
DEPTH = 4
HEAD_DIM = 64
LANES = 128
A_HEADS = 8
A_KV_HEADS = 2
A_WINDOW = 128
B_HEADS = 8
B_BRANCHES = ((128, 1), (512, 4), (2048, 16))
B_HALF = 64
C_HEADS = 16
GRID_W = 64
NA_ROWS = 8
NA_COLS = 16
REL_BUCKETS = 32
REL_MAX_DIST = 1024
N_GROUPS = 4
EXPERTS_PER_GROUP = 4
N_EXPERTS = 16
D_EXPERT = 512
N_PAIRS = 6
N_CLASSES = N_GROUPS * N_PAIRS
DEEPNORM_ALPHA = (2.0 * DEPTH) ** 0.25
LN_EPS = 1e-5
ATTN_SCALE = HEAD_DIM ** -0.5
NEG_INF = -1e30

QA_W = A_HEADS * HEAD_DIM
KVA_W = A_KV_HEADS * HEAD_DIM
A_IN = QA_W + 2 * KVA_W
B_W = B_HEADS * HEAD_DIM
B_IN = 3 * B_W
C_W = C_HEADS * HEAD_DIM

ATT_TB = 1024
MM_TM = 512
MOE_TM = 256
ROW_TM = 512
VMEM_LIMIT = 56 * 1024 * 1024


def _cparams(sem):
    return pltpu.CompilerParams(dimension_semantics=sem, vmem_limit_bytes=VMEM_LIMIT)


def _segment_flags(tok0, size, seg_starts, seg_ends):
    is_first = functools.reduce(jnp.logical_or, [tok0 == s for s in seg_starts])
    is_last = functools.reduce(jnp.logical_or, [tok0 + size == e for e in seg_ends])
    return is_first, is_last


def _t5_bucket_np(rel):
    half_b = REL_BUCKETS // 2
    max_exact = half_b // 2
    n = np.abs(rel)
    large = max_exact + (np.log(np.maximum(n, max_exact).astype(np.float32) / max_exact)
                         / math.log(REL_MAX_DIST / max_exact) * (half_b - max_exact)).astype(np.int32)
    large = np.minimum(large, half_b - 1)
    return np.where(rel > 0, half_b, 0) + np.where(n < max_exact, n, large)


def _banded_bias(table, half, dil):
    rel = np.arange(3 * half)[None, :] - half - np.arange(half)[:, None]
    bias = jnp.transpose(table[_t5_bucket_np(rel * dil)], (2, 0, 1)).astype(F32)
    return jnp.where(jnp.asarray(np.abs(rel) <= half)[None], bias, NEG_INF)


def _bias_a(rel_bias):
    b = _banded_bias(rel_bias[:, :A_HEADS], A_WINDOW, 1)
    return b.reshape(A_KV_HEADS, 4 * A_WINDOW, 3 * A_WINDOW)


def _bias_b(rel_bias):
    per = [_banded_bias(rel_bias[:, A_HEADS:], B_HALF, d) for _, d in B_BRANCHES]
    b = jnp.stack(per, axis=1)
    return b.reshape(B_HEADS // 2, 2, len(B_BRANCHES), B_HALF, 3 * B_HALF).transpose(0, 2, 1, 3, 4) \
            .reshape(B_HEADS // 2, len(B_BRANCHES), 2 * B_HALF, 3 * B_HALF)


def _bias_c(rpb):
    shift = np.arange(NA_ROWS)[:, None, None, None]
    cq = np.arange(GRID_W)[None, :, None, None]
    j = np.arange(NA_ROWS)[None, None, :, None]
    w = np.arange(GRID_W)[None, None, None, :]
    d_row = np.broadcast_to(j + NA_ROWS - 1 - shift, (NA_ROWS, GRID_W, NA_ROWS, GRID_W))
    d_col = np.broadcast_to(np.clip(w - cq + NA_COLS - 1, 0, 2 * NA_COLS - 2), d_row.shape)
    c0 = np.clip(cq - NA_COLS // 2, 0, GRID_W - NA_COLS)
    ok = np.broadcast_to((w >= c0) & (w < c0 + NA_COLS), d_row.shape)
    d_row = np.clip(d_row, 0, 2 * NA_ROWS - 2)
    bias = rpb[:, d_row, d_col].astype(F32)
    bias = jnp.where(jnp.asarray(ok)[None], bias, NEG_INF)
    bias = bias.reshape(C_HEADS // 2, 2, NA_ROWS, GRID_W, NA_ROWS * GRID_W)
    return bias.transpose(0, 2, 1, 3, 4).reshape(C_HEADS // 2, NA_ROWS, 2 * GRID_W, NA_ROWS * GRID_W)


def _inproj_kernel(x_ref, w_ref, *o_refs, splits):
    x = x_ref[...].astype(BF16)
    for o_ref, (lo, hi) in zip(o_refs, splits):
        o_ref[...] = jnp.dot(x, w_ref[:, lo:hi], preferred_element_type=F32).astype(o_ref.dtype)


def _inproj(x, w, splits, dtypes):
    n = x.shape[0]
    return pl.pallas_call(
        functools.partial(_inproj_kernel, splits=splits),
        grid=(n // MM_TM,),
        in_specs=[pl.BlockSpec((MM_TM, D_MODEL), lambda i: (i, 0)),
                  pl.BlockSpec(w.shape, lambda i: (0, 0))],
        out_specs=[pl.BlockSpec((MM_TM, hi - lo), lambda i: (i, 0)) for lo, hi in splits],
        out_shape=[jax.ShapeDtypeStruct((n, hi - lo), dt) for (lo, hi), dt in zip(splits, dtypes)],
        compiler_params=_cparams(("parallel",)),
        name="inproj",
    )(x, w)


def _layer_norm(z, g, b):
    mu = jnp.mean(z, axis=-1, keepdims=True)
    zc = z - mu
    var = jnp.mean(zc * zc, axis=-1, keepdims=True)
    return zc * lax.rsqrt(var + LN_EPS) * g + b


def _outproj_ln_kernel(*refs, n_parts):
    o_refs = refs[:n_parts]
    w_refs = refs[n_parts:2 * n_parts]
    x_ref, g_ref, b_ref, out_ref = refs[2 * n_parts:]
    h = DEEPNORM_ALPHA * x_ref[...]
    for o_ref, w_ref in zip(o_refs, w_refs):
        h = h + jnp.dot(o_ref[...], w_ref[...], preferred_element_type=F32)
    out_ref[...] = _layer_norm(h, g_ref[...], b_ref[...])


def _outproj_ln(parts, weights, x, g, b):
    n = x.shape[0]
    n_parts = len(parts)
    return pl.pallas_call(
        functools.partial(_outproj_ln_kernel, n_parts=n_parts),
        grid=(n // MM_TM,),
        in_specs=([pl.BlockSpec((MM_TM, p.shape[1]), lambda i: (i, 0)) for p in parts]
                  + [pl.BlockSpec(w.shape, lambda i: (0, 0)) for w in weights]
                  + [pl.BlockSpec((MM_TM, D_MODEL), lambda i: (i, 0)),
                     pl.BlockSpec((1, D_MODEL), lambda i: (0, 0)),
                     pl.BlockSpec((1, D_MODEL), lambda i: (0, 0))]),
        out_specs=pl.BlockSpec((MM_TM, D_MODEL), lambda i: (i, 0)),
        out_shape=jax.ShapeDtypeStruct((n, D_MODEL), F32),
        compiler_params=_cparams(("parallel",)),
        name="outproj_ln",
    )(*parts, *weights, x, g.reshape(1, D_MODEL), b.reshape(1, D_MODEL))


def _attn_a_kernel(q_ref, kvm_ref, kvp_ref, kvn_ref, bias_ref, sink_ref, o_ref, kv_scr,
                   *, seg_starts, seg_ends):
    w = A_WINDOW
    n_sub = ATT_TB // w
    tok0 = pl.program_id(0) * ATT_TB
    is_first, is_last = _segment_flags(tok0, ATT_TB, seg_starts, seg_ends)
    kv_scr[0:w, :] = kvp_ref[...]
    kv_scr[w:w + ATT_TB, :] = kvm_ref[...]
    kv_scr[w + ATT_TB:, :] = kvn_ref[...]
    lane = lax.broadcasted_iota(jnp.int32, (1, LANES), 1)
    col = lax.broadcasted_iota(jnp.int32, (1, 3 * w), 1)
    pen_first = jnp.where(jnp.logical_and(col < w, is_first), NEG_INF, 0.0)
    pen_last = jnp.where(jnp.logical_and(col >= 2 * w, is_last), NEG_INF, 0.0)
    for j in range(n_sub):
        q = q_ref[j * w:(j + 1) * w, :]
        kv = kv_scr[j * w:j * w + 3 * w, :]
        k2 = kv[:, :LANES]
        v2 = kv[:, LANES:]
        outs = []
        for g in range(A_KV_HEADS):
            keep = (lane < HEAD_DIM) if g == 0 else (lane >= HEAD_DIM)
            qg = jnp.concatenate(
                [jnp.where(keep, q[:, c * LANES:(c + 1) * LANES], 0) for c in range(4)], axis=0)
            s = lax.dot_general(qg, k2, (((1,), (1,)), ((), ())), preferred_element_type=F32)
            s = s + bias_ref[g]
            if j == 0:
                s = s + pen_first
            if j == n_sub - 1:
                s = s + pen_last
            sink = sink_ref[g]
            m = jnp.maximum(jnp.max(s, axis=-1, keepdims=True), sink)
            e = jnp.exp(s - m)
            den = jnp.sum(e, axis=-1, keepdims=True) + jnp.exp(sink - m)
            pv = jnp.dot(e.astype(BF16), v2, preferred_element_type=F32)
            outs.append(pv * (1.0 / den))
        for c in range(4):
            oc = jnp.where(lane < HEAD_DIM, outs[0][c * w:(c + 1) * w], outs[1][c * w:(c + 1) * w])
            o_ref[j * w:(j + 1) * w, c * LANES:(c + 1) * LANES] = oc.astype(o_ref.dtype)


def _attn_a(a_qkv, bias, sink, seg_starts, seg_ends):
    n = a_qkv.shape[0]
    w = A_WINDOW
    sub = ATT_TB // w
    nhb = n // w
    kv_col = QA_W // (2 * LANES)
    return pl.pallas_call(
        functools.partial(_attn_a_kernel, seg_starts=seg_starts, seg_ends=seg_ends),
        grid=(n // ATT_TB,),
        in_specs=[pl.BlockSpec((ATT_TB, QA_W), lambda i: (i, 0)),
                  pl.BlockSpec((ATT_TB, 2 * LANES), lambda i: (i, kv_col)),
                  pl.BlockSpec((w, 2 * LANES), lambda i: (jnp.maximum(i * sub - 1, 0), kv_col)),
                  pl.BlockSpec((w, 2 * LANES), lambda i: (jnp.minimum((i + 1) * sub, nhb - 1), kv_col)),
                  pl.BlockSpec(bias.shape, lambda i: (0, 0, 0)),
                  pl.BlockSpec(sink.shape, lambda i: (0, 0, 0))],
        out_specs=pl.BlockSpec((ATT_TB, QA_W), lambda i: (i, 0)),
        out_shape=jax.ShapeDtypeStruct((n, QA_W), BF16),
        scratch_shapes=[pltpu.VMEM((ATT_TB + 2 * w, 2 * LANES), BF16)],
        compiler_params=_cparams(("parallel",)),
        name="attn_a",
    )(a_qkv, a_qkv, a_qkv, a_qkv, bias, sink)


def _attn_b_kernel(q_ref, kp_ref, km_ref, kn_ref, vp_ref, vm_ref, vn_ref, bias_ref, o_ref,
                   k_scr, v_scr, o_scr, m_scr, l_scr, *, seg_starts, seg_ends):
    tb = ATT_TB
    h = B_HALF
    tok0 = pl.program_id(1) * tb
    is_first, is_last = _segment_flags(tok0, tb, seg_starts, seg_ends)
    k_scr[0:tb, :] = kp_ref[...]
    k_scr[tb:2 * tb, :] = km_ref[...]
    k_scr[2 * tb:, :] = kn_ref[...]
    v_scr[0:tb, :] = vp_ref[...]
    v_scr[tb:2 * tb, :] = vm_ref[...]
    v_scr[2 * tb:, :] = vn_ref[...]
    lane = lax.broadcasted_iota(jnp.int32, (1, LANES), 1)
    low = lane < HEAD_DIM
    col = lax.broadcasted_iota(jnp.int32, (1, 3 * h), 1)

    for br, (_, d) in enumerate(B_BRANCHES):
        nb = tb // (h * d)
        bias = bias_ref[br]

        def tile(t, carry, d=d, nb=nb, br=br, bias=bias):
            r = t // nb
            b = t % nb
            row0 = r + h * d * b
            if d == 1:
                qs = pl.ds(row0, h)
                ks = pl.ds(tb + row0 - h, 3 * h)
            else:
                qs = pl.ds(row0, h, stride=d)
                ks = pl.ds(tb + row0 - h * d, 3 * h, stride=d)
            q = q_ref[qs, :].astype(BF16)
            k = k_scr[ks, :].astype(BF16)
            v = v_scr[ks, :].astype(BF16)
            qq = jnp.concatenate([jnp.where(low, q, 0), jnp.where(low, 0, q)], axis=0)
            s = lax.dot_general(qq, k, (((1,), (1,)), ((), ())), preferred_element_type=F32)
            pen = (jnp.where(jnp.logical_and(col < h, jnp.logical_and(is_first, b == 0)), NEG_INF, 0.0)
                   + jnp.where(jnp.logical_and(col >= 2 * h, jnp.logical_and(is_last, b == nb - 1)),
                               NEG_INF, 0.0))
            s = s + bias + pen
            m = jnp.max(s, axis=-1, keepdims=True)
            e = jnp.exp(s - m)
            l = jnp.sum(e, axis=-1, keepdims=True)
            pv = jnp.dot(e.astype(BF16), v, preferred_element_type=F32)
            o_scr[br, qs, :] = jnp.where(low, pv[:h], pv[h:])
            m_scr[br, qs, :] = jnp.where(low, m[:h], m[h:])
            l_scr[br, qs, :] = jnp.where(low, l[:h], l[h:])
            return carry

        lax.fori_loop(0, d * nb, tile, 0)

    m_all = jnp.maximum(jnp.maximum(m_scr[0], m_scr[1]), m_scr[2])
    num = jnp.zeros((tb, LANES), F32)
    den = jnp.zeros((tb, LANES), F32)
    for br in range(len(B_BRANCHES)):
        a = jnp.exp(m_scr[br] - m_all)
        num = num + a * o_scr[br]
        den = den + a * l_scr[br]
    o_ref[...] = (num / den).astype(o_ref.dtype)


def _attn_b(b_qkv, bias, seg_starts, seg_ends):
    n = b_qkv.shape[0]
    tb = ATT_TB
    nblk = n // tb
    npair = B_HEADS // 2
    prev = lambda i: jnp.maximum(i - 1, 0)
    nxt = lambda i: jnp.minimum(i + 1, nblk - 1)
    blk = lambda rowf, off: pl.BlockSpec((tb, LANES), lambda c, i: (rowf(i), off + c))
    same = lambda i: i
    stat = pltpu.VMEM((len(B_BRANCHES), tb, LANES), F32)
    return pl.pallas_call(
        functools.partial(_attn_b_kernel, seg_starts=seg_starts, seg_ends=seg_ends),
        grid=(npair, nblk),
        in_specs=[blk(same, 0),
                  blk(prev, npair), blk(same, npair), blk(nxt, npair),
                  blk(prev, 2 * npair), blk(same, 2 * npair), blk(nxt, 2 * npair),
                  pl.BlockSpec((None,) + bias.shape[1:], lambda c, i: (c, 0, 0, 0))],
        out_specs=pl.BlockSpec((tb, LANES), lambda c, i: (i, c)),
        out_shape=jax.ShapeDtypeStruct((n, B_W), BF16),
        scratch_shapes=[pltpu.VMEM((3 * tb, LANES), F32), pltpu.VMEM((3 * tb, LANES), F32),
                        stat, stat, stat],
        compiler_params=_cparams(("parallel", "parallel")),
        name="attn_b",
    )(b_qkv, b_qkv, b_qkv, b_qkv, b_qkv, b_qkv, b_qkv, bias)


C_HALO = (NA_ROWS // 2) * GRID_W


def _attn_c_kernel(q_ref, kp_ref, km_ref, kn_ref, vp_ref, vm_ref, vn_ref, bias_ref, o_ref,
                   k_scr, v_scr, *, seg_starts, seg_ends):
    tb = ATT_TB
    gw = GRID_W
    nkeys = NA_ROWS * gw
    tok0 = pl.program_id(1) * tb
    k_scr[0:C_HALO, :] = kp_ref[...]
    k_scr[C_HALO:C_HALO + tb, :] = km_ref[...]
    k_scr[C_HALO + tb:, :] = kn_ref[...]
    v_scr[0:C_HALO, :] = vp_ref[...]
    v_scr[C_HALO:C_HALO + tb, :] = vm_ref[...]
    v_scr[C_HALO + tb:, :] = vn_ref[...]
    seg_row0 = jnp.int32(0)
    seg_rows = jnp.int32(0)
    for s, e in zip(seg_starts, seg_ends):
        inside = jnp.logical_and(tok0 >= s, tok0 < e)
        seg_row0 = jnp.where(inside, s // gw, seg_row0)
        seg_rows = jnp.where(inside, (e - s) // gw, seg_rows)
    lane = lax.broadcasted_iota(jnp.int32, (1, LANES), 1)
    low = lane < HEAD_DIM

    def row(rr, carry):
        rs = tok0 // gw + rr - seg_row0
        start = jnp.clip(rs - NA_ROWS // 2, 0, seg_rows - NA_ROWS)
        shift = rs - start
        koff = pl.multiple_of((rr + NA_ROWS // 2 - shift) * gw, gw)
        q = q_ref[pl.ds(pl.multiple_of(rr * gw, gw), gw), :]
        k = k_scr[pl.ds(koff, nkeys), :]
        v = v_scr[pl.ds(koff, nkeys), :]
        qq = jnp.concatenate([jnp.where(low, q, 0), jnp.where(low, 0, q)], axis=0)
        s = lax.dot_general(qq, k, (((1,), (1,)), ((), ())), preferred_element_type=F32)
        s = s + bias_ref[shift]
        m = jnp.max(s, axis=-1, keepdims=True)
        e = jnp.exp(s - m)
        den = jnp.sum(e, axis=-1, keepdims=True)
        pv = jnp.dot(e.astype(BF16), v, preferred_element_type=F32) * (1.0 / den)
        o_ref[pl.ds(pl.multiple_of(rr * gw, gw), gw), :] = jnp.where(low, pv[:gw], pv[gw:]).astype(o_ref.dtype)
        return carry

    lax.fori_loop(0, tb // gw, row, 0)


def _attn_c(c_qkv, bias, seg_starts, seg_ends):
    n = c_qkv.shape[0]
    tb = ATT_TB
    npair = C_HEADS // 2
    sub = tb // C_HALO
    nhb = n // C_HALO
    main = lambda off: pl.BlockSpec((tb, LANES), lambda c, i: (i, off + c))
    prev = lambda off: pl.BlockSpec((C_HALO, LANES), lambda c, i: (jnp.maximum(i * sub - 1, 0), off + c))
    nxt = lambda off: pl.BlockSpec((C_HALO, LANES),
                                   lambda c, i: (jnp.minimum((i + 1) * sub, nhb - 1), off + c))
    return pl.pallas_call(
        functools.partial(_attn_c_kernel, seg_starts=seg_starts, seg_ends=seg_ends),
        grid=(npair, n // tb),
        in_specs=[main(0),
                  prev(npair), main(npair), nxt(npair),
                  prev(2 * npair), main(2 * npair), nxt(2 * npair),
                  pl.BlockSpec((None,) + bias.shape[1:], lambda c, i: (c, 0, 0, 0))],
        out_specs=pl.BlockSpec((tb, LANES), lambda c, i: (i, c)),
        out_shape=jax.ShapeDtypeStruct((n, C_W), BF16),
        scratch_shapes=[pltpu.VMEM((tb + 2 * C_HALO, LANES), BF16),
                        pltpu.VMEM((tb + 2 * C_HALO, LANES), BF16)],
        compiler_params=_cparams(("parallel", "parallel")),
        name="attn_c",
    )(c_qkv, c_qkv, c_qkv, c_qkv, c_qkv, c_qkv, c_qkv, bias)


def _router_kernel(x_ref, w_ref, b_ref, cls_ref, cnt_ref, run_scr):
    i = pl.program_id(0)
    tm = x_ref.shape[0]

    @pl.when(i == 0)
    def _():
        run_scr[...] = jnp.zeros_like(run_scr)

    logits = jnp.dot(x_ref[...], w_ref[...], preferred_element_type=F32,
                     precision=lax.Precision.HIGHEST) + b_ref[...]
    lane = lax.broadcasted_iota(jnp.int32, (tm, LANES), 1)
    big = jnp.int32(LANES)
    is_g = lane < N_GROUPS
    lg = jnp.where(is_g, logits, NEG_INF)
    mg = jnp.max(lg, axis=-1, keepdims=True)
    g_sel = jnp.min(jnp.where(jnp.logical_and(is_g, lg == mg), lane, big), axis=-1, keepdims=True)
    e_lo = N_GROUPS + g_sel * EXPERTS_PER_GROUP
    in_grp = jnp.logical_and(lane >= e_lo, lane < e_lo + EXPERTS_PER_GROUP)
    le = jnp.where(in_grp, logits, NEG_INF)
    v1 = jnp.max(le, axis=-1, keepdims=True)
    i1 = jnp.min(jnp.where(jnp.logical_and(in_grp, le == v1), lane, big), axis=-1, keepdims=True)
    rest = jnp.logical_and(in_grp, lane != i1)
    le2 = jnp.where(rest, logits, NEG_INF)
    v2 = jnp.max(le2, axis=-1, keepdims=True)
    i2 = jnp.min(jnp.where(jnp.logical_and(rest, le2 == v2), lane, big), axis=-1, keepdims=True)
    a = jnp.minimum(i1, i2) - e_lo
    b = jnp.maximum(i1, i2) - e_lo
    pair = a * 3 - jnp.where(a == 2, 1, 0) + (b - a - 1)
    cls = g_sel * N_PAIRS + pair
    onehot = (lane == cls)
    ri = lax.broadcasted_iota(jnp.int32, (tm, tm), 0)
    ci = lax.broadcasted_iota(jnp.int32, (tm, tm), 1)
    tril = (ci < ri).astype(BF16)
    before = jnp.dot(tril, onehot.astype(BF16), preferred_element_type=F32) + run_scr[...]
    rank = jnp.sum(jnp.where(onehot, before, 0.0), axis=-1, keepdims=True)
    run_scr[...] = run_scr[...] + jnp.sum(onehot.astype(F32), axis=0, keepdims=True)
    cls_ref[...] = jnp.where(lane == 0, cls, jnp.where(lane == 1, rank.astype(jnp.int32), 0))
    cnt_ref[...] = run_scr[...].astype(jnp.int32)


def _router(x, w, b):
    n = x.shape[0]
    tm = MM_TM
    return pl.pallas_call(
        _router_kernel,
        grid=(n // tm,),
        in_specs=[pl.BlockSpec((tm, D_MODEL), lambda i: (i, 0)),
                  pl.BlockSpec((D_MODEL, LANES), lambda i: (0, 0)),
                  pl.BlockSpec((1, LANES), lambda i: (0, 0))],
        out_specs=[pl.BlockSpec((tm, LANES), lambda i: (i, 0)),
                   pl.BlockSpec((1, LANES), lambda i: (0, 0))],
        out_shape=[jax.ShapeDtypeStruct((n, LANES), jnp.int32),
                   jax.ShapeDtypeStruct((1, LANES), jnp.int32)],
        scratch_shapes=[pltpu.VMEM((1, LANES), F32)],
        compiler_params=_cparams(("arbitrary",)),
        name="router",
    )(x, w, b)


SUBLANES = 8


def _tile_copy(src, src_tok, dst, dst_tok, sem):
    return pltpu.make_async_copy(src.at[pl.ds(pl.multiple_of(src_tok * SUBLANES, SUBLANES), SUBLANES), :],
                                 dst.at[pl.ds(pl.multiple_of(dst_tok * SUBLANES, SUBLANES), SUBLANES), :], sem)


def _dispatch_kernel(dest_ref, pad_ref, x_ref, xs_ref, rec_scr, zero_scr, sem, zsem):
    i = pl.program_id(0)
    tm = x_ref.shape[0]
    tile_rows = MOE_TM * SUBLANES

    def zero_copy(c):
        start = pl.multiple_of(pad_ref[c] * SUBLANES, tile_rows)
        return pltpu.make_async_copy(zero_scr, xs_ref.at[pl.ds(start, tile_rows), :], zsem)

    @pl.when(i == 0)
    def _():
        zero_scr[...] = jnp.zeros_like(zero_scr)
        for c in range(N_CLASSES):
            @pl.when(pad_ref[c] >= 0)
            def _():
                zero_copy(c).start()
        for c in range(N_CLASSES):
            @pl.when(pad_ref[c] >= 0)
            def _():
                zero_copy(c).wait()

    for j in range(D_MODEL // LANES):
        rec_scr[pl.ds(j, tm, stride=SUBLANES), :] = x_ref[:, j * LANES:(j + 1) * LANES]

    def issue(r, carry):
        _tile_copy(rec_scr, r, xs_ref, dest_ref[i * tm + r], sem).start()
        return carry

    lax.fori_loop(0, tm, issue, 0)

    def drain(r, carry):
        _tile_copy(rec_scr, r, xs_ref, dest_ref[i * tm + r], sem).wait()
        return carry

    lax.fori_loop(0, tm, drain, 0)


def _dispatch(dest, pad_start, x, n_sorted):
    n = x.shape[0]
    tm = ROW_TM
    return pl.pallas_call(
        _dispatch_kernel,
        grid_spec=pltpu.PrefetchScalarGridSpec(
            num_scalar_prefetch=2,
            grid=(n // tm,),
            in_specs=[pl.BlockSpec((tm, D_MODEL), lambda i, d, p: (i, 0))],
            out_specs=pl.BlockSpec(memory_space=pl.ANY),
            scratch_shapes=[pltpu.VMEM((tm * SUBLANES, LANES), F32),
                            pltpu.VMEM((MOE_TM * SUBLANES, LANES), F32),
                            pltpu.SemaphoreType.DMA, pltpu.SemaphoreType.DMA]),
        out_shape=jax.ShapeDtypeStruct((n_sorted * SUBLANES, LANES), F32),
        compiler_params=_cparams(("arbitrary",)),
        name="dispatch",
    )(dest, pad_start, x)


def _expert_kernel(ea_ref, eb_ref, nt_ref, xs_ref, wr_ref, br_ref,
                   wga_ref, wua_ref, wda_ref, wgb_ref, wub_ref, wdb_ref, g_ref, b_ref, ys_ref):
    p = pl.program_id(0)
    tm = MOE_TM

    @pl.when(p < nt_ref[0])
    def _():
        x = jnp.concatenate([xs_ref[pl.ds(j, tm, stride=SUBLANES), :] for j in range(D_MODEL // LANES)],
                            axis=1)
        xb = x.astype(BF16)

        logits = jnp.dot(xb, wr_ref[...], preferred_element_type=F32) + br_ref[...]
        lane = lax.broadcasted_iota(jnp.int32, (tm, LANES), 1)
        lane_a = N_GROUPS + ea_ref[p]
        lane_b = N_GROUPS + eb_ref[p]
        grp = ea_ref[p] // EXPERTS_PER_GROUP
        is_g = lane < N_GROUPS
        mg = jnp.max(jnp.where(is_g, logits, NEG_INF), axis=-1, keepdims=True)
        eg = jnp.where(is_g, jnp.exp(logits - mg), 0.0)
        g_gate = (jnp.sum(jnp.where(lane == grp, eg, 0.0), axis=-1, keepdims=True)
                  / jnp.sum(eg, axis=-1, keepdims=True))
        l_a = jnp.sum(jnp.where(lane == lane_a, logits, 0.0), axis=-1, keepdims=True)
        l_b = jnp.sum(jnp.where(lane == lane_b, logits, 0.0), axis=-1, keepdims=True)
        mx = jnp.maximum(l_a, l_b)
        p_a = jnp.exp(l_a - mx)
        p_b = jnp.exp(l_b - mx)
        scale = g_gate / (p_a + p_b)
        w_a = p_a * scale
        w_b = p_b * scale

        def expert(wg_ref, wu_ref, wd_ref):
            gate = jnp.dot(xb, wg_ref[...], preferred_element_type=F32)
            up = jnp.dot(xb, wu_ref[...], preferred_element_type=F32)
            hid = (gate * (1.0 / (1.0 + jnp.exp(-gate))) * up).astype(BF16)
            return jnp.dot(hid, wd_ref[...], preferred_element_type=F32)

        y = w_a * expert(wga_ref, wua_ref, wda_ref) + w_b * expert(wgb_ref, wub_ref, wdb_ref)
        out = _layer_norm(DEEPNORM_ALPHA * x + y, g_ref[...], b_ref[...])
        for j in range(D_MODEL // LANES):
            ys_ref[pl.ds(j, tm, stride=SUBLANES), :] = out[:, j * LANES:(j + 1) * LANES]


def _experts(layer, tile_ea, tile_eb, n_tiles, xs, w_router, b_router, w_gate, w_up, w_down, g, b):
    tm = MOE_TM
    rows = tm * SUBLANES
    n_grid = xs.shape[0] // rows
    last = lambda p, nt: jnp.minimum(p, nt[0] - 1)
    wspec_a = lambda shape: pl.BlockSpec((None, None) + shape, lambda p, ea, eb, nt: (layer, ea[p], 0, 0))
    wspec_b = lambda shape: pl.BlockSpec((None, None) + shape, lambda p, ea, eb, nt: (layer, eb[p], 0, 0))
    const = lambda shape: pl.BlockSpec(shape, lambda p, ea, eb, nt: (0, 0))
    up_shape = (D_MODEL, D_EXPERT)
    dn_shape = (D_EXPERT, D_MODEL)
    return pl.pallas_call(
        _expert_kernel,
        grid_spec=pltpu.PrefetchScalarGridSpec(
            num_scalar_prefetch=3,
            grid=(n_grid,),
            in_specs=[pl.BlockSpec((rows, LANES), lambda p, ea, eb, nt: (last(p, nt), 0)),
                      const((D_MODEL, LANES)), const((1, LANES)),
                      wspec_a(up_shape), wspec_a(up_shape), wspec_a(dn_shape),
                      wspec_b(up_shape), wspec_b(up_shape), wspec_b(dn_shape),
                      const((1, D_MODEL)), const((1, D_MODEL))],
            out_specs=pl.BlockSpec((rows, LANES), lambda p, ea, eb, nt: (last(p, nt), 0))),
        out_shape=jax.ShapeDtypeStruct(xs.shape, F32),
        compiler_params=_cparams(("arbitrary",)),
        name="experts",
    )(tile_ea, tile_eb, n_tiles, xs, w_router.astype(BF16), b_router,
      w_gate, w_up, w_down, w_gate, w_up, w_down, g.reshape(1, D_MODEL), b.reshape(1, D_MODEL))


def _gather_kernel(dest_ref, ys_ref, out_ref, rec_scr, sem):
    i = pl.program_id(0)
    tm = out_ref.shape[0]

    def issue(r, carry):
        _tile_copy(ys_ref, dest_ref[i * tm + r], rec_scr, r, sem).start()
        return carry

    lax.fori_loop(0, tm, issue, 0)

    def drain(r, carry):
        _tile_copy(ys_ref, dest_ref[i * tm + r], rec_scr, r, sem).wait()
        return carry

    lax.fori_loop(0, tm, drain, 0)
    for j in range(D_MODEL // LANES):
        out_ref[:, j * LANES:(j + 1) * LANES] = rec_scr[pl.ds(j, tm, stride=SUBLANES), :]


def _gather_rows(dest, ys, n):
    tm = ROW_TM
    return pl.pallas_call(
        _gather_kernel,
        grid_spec=pltpu.PrefetchScalarGridSpec(
            num_scalar_prefetch=1,
            grid=(n // tm,),
            in_specs=[pl.BlockSpec(memory_space=pl.ANY)],
            out_specs=pl.BlockSpec((tm, D_MODEL), lambda i, d: (i, 0)),
            scratch_shapes=[pltpu.VMEM((tm * SUBLANES, LANES), F32), pltpu.SemaphoreType.DMA]),
        out_shape=jax.ShapeDtypeStruct((n, D_MODEL), F32),
        compiler_params=_cparams(("arbitrary",)),
        name="gather_rows",
    )(dest, ys)


_PAIR_A = np.array([0, 0, 0, 1, 1, 2], np.int32)
_PAIR_B = np.array([1, 2, 3, 2, 3, 3], np.int32)


def _moe_layer(layer, x, w_router, b_router, w_gate, w_up, w_down, g, b):
    n = x.shape[0]
    tm = MOE_TM
    n_sorted = n + N_CLASSES * tm
    cls_rank, counts = _router(x, w_router, b_router)
    cls = cls_rank[:, 0]
    rank = cls_rank[:, 1]
    counts = counts[0, :N_CLASSES]
    padded = (counts + tm - 1) // tm * tm
    ends = jnp.cumsum(padded)
    offs = ends - padded
    dest = offs[cls] + rank
    pad_start = jnp.where(padded > 0, ends - tm, -1).astype(jnp.int32)
    tile_start = jnp.arange(n_sorted // tm, dtype=jnp.int32) * tm
    tile_cls = jnp.minimum(jnp.searchsorted(ends, tile_start, side="right"), N_CLASSES - 1).astype(jnp.int32)
    n_tiles = (ends[-1] // tm).astype(jnp.int32).reshape(1)
    last_cls = tile_cls[jnp.maximum(n_tiles[0] - 1, 0)]
    tile_cls = jnp.where(tile_start < ends[-1], tile_cls, last_cls)
    grp = tile_cls // N_PAIRS
    tile_ea = (grp * EXPERTS_PER_GROUP + jnp.asarray(_PAIR_A)[tile_cls % N_PAIRS]).astype(jnp.int32)
    tile_eb = (grp * EXPERTS_PER_GROUP + jnp.asarray(_PAIR_B)[tile_cls % N_PAIRS]).astype(jnp.int32)
    xs = _dispatch(dest.astype(jnp.int32), pad_start, x, n_sorted)
    ys = _experts(layer, tile_ea, tile_eb, n_tiles, xs, w_router, b_router, w_gate, w_up, w_down, g, b)
    return _gather_rows(dest.astype(jnp.int32), ys, n)


_A_ORDER = np.array([0, 4, 1, 5, 2, 6, 3, 7])


def _prep_ab(w_in, w_out):
    qa = w_in[:, :QA_W].reshape(D_MODEL, A_HEADS, HEAD_DIM)[:, _A_ORDER].reshape(D_MODEL, QA_W) * ATTN_SCALE
    kva = w_in[:, QA_W:A_IN]
    qb = w_in[:, A_IN:A_IN + B_W] * ATTN_SCALE
    kvb = w_in[:, A_IN + B_W:]
    w = jnp.concatenate([qa, kva, qb, kvb], axis=1).astype(BF16)
    wo_a = w_out[:QA_W].reshape(A_HEADS, HEAD_DIM, D_MODEL)[_A_ORDER].reshape(QA_W, D_MODEL).astype(BF16)
    wo_b = w_out[QA_W:].astype(BF16)
    return w, wo_a, wo_b


def _prep_c(w_in, w_out):
    w = jnp.concatenate([w_in[:, :C_W] * ATTN_SCALE, w_in[:, C_W:]], axis=1).astype(BF16)
    return w, w_out.astype(BF16)


def _sink_rows(a_sink):
    s = a_sink.reshape(A_KV_HEADS, 4, 1, 1)
    return jnp.broadcast_to(s, (A_KV_HEADS, 4, A_WINDOW, 1)).reshape(A_KV_HEADS, 4 * A_WINDOW, 1).astype(F32)


def _trunk(x, seg_starts, seg_ends, rel_bias, w_in_ab, a_sink, w_out_ab, w_in_c, c_rpb, w_out_c,
           ln1_g, ln1_b, ln2_g, ln2_b, router_g_w, router_g_b, router_e_w, router_e_b,
           w_gate, w_up, w_down):
    bias_a = _bias_a(rel_bias)
    bias_b = _bias_b(rel_bias)
    w_gate = w_gate.astype(BF16)
    w_up = w_up.astype(BF16)
    w_down = w_down.astype(BF16)
    for l in range(DEPTH):
        i = l // 2
        if l % 2 == 0:
            w, wo_a, wo_b = _prep_ab(w_in_ab[i], w_out_ab[i])
            a_qkv, b_qkv = _inproj(x, w, ((0, A_IN), (A_IN, A_IN + B_IN)), (BF16, F32))
            o_a = _attn_a(a_qkv, bias_a, _sink_rows(a_sink[i]), seg_starts, seg_ends)
            o_b = _attn_b(b_qkv, bias_b, seg_starts, seg_ends)
            x = _outproj_ln([o_a, o_b], [wo_a, wo_b], x, ln1_g[l], ln1_b[l])
        else:
            w, wo = _prep_c(w_in_c[i], w_out_c[i])
            (c_qkv,) = _inproj(x, w, ((0, 3 * C_W),), (BF16,))
            o_c = _attn_c(c_qkv, _bias_c(c_rpb[i]), seg_starts, seg_ends)
            x = _outproj_ln([o_c], [wo], x, ln1_g[l], ln1_b[l])
        pad = LANES - N_GROUPS - N_EXPERTS
        w_router = jnp.pad(jnp.concatenate([router_g_w[l], router_e_w[l]], axis=1), ((0, 0), (0, pad)))
        b_router = jnp.pad(jnp.concatenate([router_g_b[l], router_e_b[l]]), (0, pad)).reshape(1, LANES)
        x = _moe_layer(l, x, w_router, b_router, w_gate, w_up, w_down, ln2_g[l], ln2_b[l])
    return x


def kernel(x_prompt, x_sample, rel_bias, w_in_ab, a_sink, w_out_ab, w_in_c, c_rpb, w_out_c,
           ln1_g, ln1_b, ln2_g, ln2_b, router_g_w, router_g_b, router_e_w, router_e_b,
           w_gate, w_up, w_down):
    seqs = [x_prompt[b] for b in range(x_prompt.shape[0])] + [x_sample[b] for b in range(x_sample.shape[0])]
    lens = [s.shape[0] for s in seqs]
    seg_ends = tuple(int(v) for v in np.cumsum(lens))
    seg_starts = tuple(e - n for e, n in zip(seg_ends, lens))
    for n in lens:
        assert n % ATT_TB == 0 and n // GRID_W >= NA_ROWS
    x = jnp.concatenate(seqs, axis=0)
    y = _trunk(x, seg_starts, seg_ends, rel_bias, w_in_ab, a_sink, w_out_ab, w_in_c, c_rpb, w_out_c,
               ln1_g, ln1_b, ln2_g, ln2_b, router_g_w, router_g_b, router_e_w, router_e_b,
               w_gate, w_up, w_down)
    n_p = x_prompt.shape[0] * x_prompt.shape[1]
    return (y[:n_p].reshape(x_prompt.shape), y[n_p:].reshape(x_sample.shape))
```

```python
import functools
import math

import numpy as np
import jax
import jax.numpy as jnp
from jax import lax
from jax.experimental import pallas as pl
from jax.experimental.pallas import tpu as pltpu

F32 = jnp.float32
BF16 = jnp.bfloat16

D_MODEL = 1024
DEPTH = 4
HEAD_DIM = 64
LANES = 128
A_HEADS = 8
A_KV_HEADS = 2
A_WINDOW = 128
B_HEADS = 8
B_BRANCHES = ((128, 1), (512, 4), (2048, 16))
B_HALF = 64
C_HEADS = 16
GRID_W = 64
NA_ROWS = 8
NA_COLS = 16
REL_BUCKETS = 32
REL_MAX_DIST = 1024
N_GROUPS = 4
EXPERTS_PER_GROUP = 4
N_EXPERTS = 16
D_EXPERT = 512
N_PAIRS = 6
N_CLASSES = N_GROUPS * N_PAIRS
DEEPNORM_ALPHA = (2.0 * DEPTH) ** 0.25
LN_EPS = 1e-5
ATTN_SCALE = HEAD_DIM ** -0.5
NEG_INF = -1e30

QA_W = A_HEADS * HEAD_DIM
KVA_W = A_KV_HEADS * HEAD_DIM
A_IN = QA_W + 2 * KVA_W
B_W = B_HEADS * HEAD_DIM
B_IN = 3 * B_W
C_W = C_HEADS * HEAD_DIM

ATT_TB = 1024
MM_TM = 512
MOE_TM = 256
ROW_TM = 512
VMEM_LIMIT = 56 * 1024 * 1024


def _cparams(sem):
    return pltpu.CompilerParams(dimension_semantics=sem, vmem_limit_bytes=VMEM_LIMIT)


def _segment_flags(tok0, size, seg_starts, seg_ends):
    is_first = functools.reduce(jnp.logical_or, [tok0 == s for s in seg_starts])
    is_last = functools.reduce(jnp.logical_or, [tok0 + size == e for e in seg_ends])
    return is_first, is_last


def _t5_bucket_np(rel):
    half_b = REL_BUCKETS // 2
    max_exact = half_b // 2
    n = np.abs(rel)
    large = max_exact + (np.log(np.maximum(n, max_exact).astype(np.float32) / max_exact)
                         / math.log(REL_MAX_DIST / max_exact) * (half_b - max_exact)).astype(np.int32)
    large = np.minimum(large, half_b - 1)
    return np.where(rel > 0, half_b, 0) + np.where(n < max_exact, n, large)


def _banded_bias(table, half, dil):
    rel = np.arange(3 * half)[None, :] - half - np.arange(half)[:, None]
    bucket = jnp.asarray(_t5_bucket_np(rel * dil).astype(np.int32))
    hit = bucket[None] == jnp.arange(REL_BUCKETS, dtype=jnp.int32)[:, None, None]
    bias = jnp.sum(jnp.where(hit[:, None], table.astype(F32)[:, :, None, None], 0.0), axis=0)
    return jnp.where(jnp.asarray(np.abs(rel) <= half)[None], bias, NEG_INF)


def _bias_a(rel_bias):
    w = A_WINDOW
    ch = A_CHUNK
    b = _banded_bias(rel_bias[:, :A_HEADS], w, 1).reshape(A_KV_HEADS, 4, w // ch, ch, 3 * w)
    b = b.transpose(0, 2, 1, 3, 4).reshape(A_KV_HEADS, w // ch, 4 * ch, 3 * w)
    col = np.arange(3 * w)
    first = jnp.where(jnp.asarray(col < w), NEG_INF, b)
    last = jnp.where(jnp.asarray(col >= 2 * w), NEG_INF, b)
    return jnp.stack([b, first, last])


def _bias_b(rel_bias):
    per = [_banded_bias(rel_bias[:, A_HEADS:], B_HALF, d) for _, d in B_BRANCHES]
    b = jnp.stack(per, axis=1)
    return b.reshape(B_HEADS // 2, 2, len(B_BRANCHES), B_HALF, 3 * B_HALF).transpose(0, 2, 1, 3, 4) \
            .reshape(B_HEADS // 2, len(B_BRANCHES), 2 * B_HALF, 3 * B_HALF)


def _bias_c(rpb):
    gw = GRID_W
    n_dr = 2 * NA_ROWS - 1
    side = gw - NA_COLS
    p = jnp.concatenate([jnp.repeat(rpb[..., :1], side, axis=-1), rpb.astype(F32),
                         jnp.repeat(rpb[..., -1:], side + 1, axis=-1)], axis=-1)
    z = jnp.broadcast_to(p[:, :, None, :], (C_HEADS, n_dr, gw, 2 * gw)).reshape(C_HEADS, n_dr, 2 * gw * gw)
    t = z[:, :, gw - 1:gw - 1 + gw * (2 * gw - 1)].reshape(C_HEADS, n_dr, gw, 2 * gw - 1)[..., :gw]
    cq = np.arange(gw)[:, None]
    w = np.arange(gw)[None, :]
    c0 = np.clip(cq - NA_COLS // 2, 0, gw - NA_COLS)
    t = jnp.where(jnp.asarray((w >= c0) & (w < c0 + NA_COLS)), t, NEG_INF)
    bias = jnp.stack([jnp.transpose(t[:, NA_ROWS - 1 - s:2 * NA_ROWS - 1 - s], (0, 2, 1, 3))
                      for s in range(NA_ROWS)], axis=1)
    bias = bias.reshape(C_HEADS // 2, 2, NA_ROWS, GRID_W, NA_ROWS * GRID_W)
    return bias.transpose(0, 2, 1, 3, 4).reshape(C_HEADS // 2, NA_ROWS, 2 * GRID_W, NA_ROWS * GRID_W)


def _inproj_kernel(x_ref, w_ref, *o_refs, splits):
    x = x_ref[...].astype(BF16)
    for o_ref, (lo, hi) in zip(o_refs, splits):
        o_ref[...] = jnp.dot(x, w_ref[:, lo:hi], preferred_element_type=F32).astype(o_ref.dtype)


def _inproj(x, w, splits, dtypes):
    n = x.shape[0]
    return pl.pallas_call(
        functools.partial(_inproj_kernel, splits=splits),
        grid=(n // MM_TM,),
        in_specs=[pl.BlockSpec((MM_TM, D_MODEL), lambda i: (i, 0)),
                  pl.BlockSpec(w.shape, lambda i: (0, 0))],
        out_specs=[pl.BlockSpec((MM_TM, hi - lo), lambda i: (i, 0)) for lo, hi in splits],
        out_shape=[jax.ShapeDtypeStruct((n, hi - lo), dt) for (lo, hi), dt in zip(splits, dtypes)],
        compiler_params=_cparams(("parallel",)),
        name="inproj",
    )(x, w)


def _layer_norm(z, g, b):
    mu = jnp.mean(z, axis=-1, keepdims=True)
    zc = z - mu
    var = jnp.mean(zc * zc, axis=-1, keepdims=True)
    return zc * lax.rsqrt(var + LN_EPS) * g + b


def _outproj_ln_kernel(*refs, n_parts):
    o_refs = refs[:n_parts]
    w_refs = refs[n_parts:2 * n_parts]
    x_ref, g_ref, b_ref, out_ref = refs[2 * n_parts:]
    h = DEEPNORM_ALPHA * x_ref[...]
    for o_ref, w_ref in zip(o_refs, w_refs):
        h = h + jnp.dot(o_ref[...], w_ref[...], preferred_element_type=F32)
    out_ref[...] = _layer_norm(h, g_ref[...], b_ref[...])


def _outproj_ln(parts, weights, x, g, b):
    n = x.shape[0]
    n_parts = len(parts)
    return pl.pallas_call(
        functools.partial(_outproj_ln_kernel, n_parts=n_parts),
        grid=(n // MM_TM,),
        in_specs=([pl.BlockSpec((MM_TM, p.shape[1]), lambda i: (i, 0)) for p in parts]
                  + [pl.BlockSpec(w.shape, lambda i: (0, 0)) for w in weights]
                  + [pl.BlockSpec((MM_TM, D_MODEL), lambda i: (i, 0)),
                     pl.BlockSpec((1, D_MODEL), lambda i: (0, 0)),
                     pl.BlockSpec((1, D_MODEL), lambda i: (0, 0))]),
        out_specs=pl.BlockSpec((MM_TM, D_MODEL), lambda i: (i, 0)),
        out_shape=jax.ShapeDtypeStruct((n, D_MODEL), F32),
        compiler_params=_cparams(("parallel",)),
        name="outproj_ln",
    )(*parts, *weights, x, g.reshape(1, D_MODEL), b.reshape(1, D_MODEL))


def _pipelined(tiles, scores, finish):
    for t in tiles:
        finish(*t, scores(*t))


def _attn_a_kernel(q_ref, kvm_ref, kvp_ref, kvn_ref, bias_ref, sink_ref, o_ref, kv_scr,
                   *, seg_starts, seg_ends):
    w = A_WINDOW
    n_sub = ATT_TB // w
    tok0 = pl.program_id(0) * ATT_TB
    is_first, is_last = _segment_flags(tok0, ATT_TB, seg_starts, seg_ends)
    kv_scr[0:w, :] = kvp_ref[...]
    kv_scr[w:w + ATT_TB, :] = kvm_ref[...]
    kv_scr[w + ATT_TB:, :] = kvn_ref[...]
    low = lax.broadcasted_iota(jnp.int32, (1, LANES), 1) < HEAD_DIM
    ch = A_CHUNK
    per = w // ch

    sinks = [jnp.concatenate([jnp.full((ch, 1), sink_ref[c + 4 * g], F32) for c in range(4)], axis=0)
             for g in range(A_KV_HEADS)]

    def scores(t, g):
        j = t // per
        q = q_ref[t * ch:(t + 1) * ch, :]
        qg = jnp.concatenate([jnp.where(low, q[:, c * LANES:(c + 1) * LANES], 0) if g == 0
                              else jnp.where(low, 0, q[:, c * LANES:(c + 1) * LANES]) for c in range(4)], axis=0)
        k2 = kv_scr[j * w:(j + 3) * w, :LANES]
        return lax.dot_general(qg, k2, (((1,), (1,)), ((), ())), preferred_element_type=F32)

    def finish(t, g, s):
        j = t // per
        if j == 0:
            variant = jnp.where(is_first, 1, 0)
        elif j == n_sub - 1:
            variant = jnp.where(is_last, 2, 0)
        else:
            variant = 0
        s = s + bias_ref[variant, g, t % per]
        m = jnp.maximum(jnp.max(s, axis=-1, keepdims=True), sinks[g])
        e = jnp.exp(s - m)
        den = jnp.sum(e, axis=-1, keepdims=True) + jnp.exp(sinks[g] - m)
        v2 = kv_scr[j * w:(j + 3) * w, LANES:]
        return jnp.dot(e.astype(BF16), v2, preferred_element_type=F32) * (1.0 / den)

    outs = {}

    def finish_and_store(t, g, s):
        outs[g] = finish(t, g, s)
        if g == A_KV_HEADS - 1:
            for c in range(4):
                oc = jnp.where(low, outs[0][c * ch:(c + 1) * ch], outs[1][c * ch:(c + 1) * ch])
                o_ref[t * ch:(t + 1) * ch, c * LANES:(c + 1) * LANES] = oc.astype(o_ref.dtype)

    _pipelined([(t, g) for t in range(ATT_TB // ch) for g in range(A_KV_HEADS)], scores, finish_and_store)


A_CHUNK = 64


def _attn_a(a_qkv, bias, sink, seg_starts, seg_ends):
    n = a_qkv.shape[0]
    w = A_WINDOW
    sub = ATT_TB // w
    nhb = n // w
    kv_col = QA_W // (2 * LANES)
    return pl.pallas_call(
        functools.partial(_attn_a_kernel, seg_starts=seg_starts, seg_ends=seg_ends),
        grid=(n // ATT_TB,),
        in_specs=[pl.BlockSpec((ATT_TB, QA_W), lambda i: (i, 0)),
                  pl.BlockSpec((ATT_TB, 2 * LANES), lambda i: (i, kv_col)),
                  pl.BlockSpec((w, 2 * LANES), lambda i: (jnp.maximum(i * sub - 1, 0), kv_col)),
                  pl.BlockSpec((w, 2 * LANES), lambda i: (jnp.minimum((i + 1) * sub, nhb - 1), kv_col)),
                  pl.BlockSpec(bias.shape, lambda i: (0,) * bias.ndim),
                  pl.BlockSpec(memory_space=pltpu.SMEM)],
        out_specs=pl.BlockSpec((ATT_TB, QA_W), lambda i: (i, 0)),
        out_shape=jax.ShapeDtypeStruct((n, QA_W), BF16),
        scratch_shapes=[pltpu.VMEM((ATT_TB + 2 * w, 2 * LANES), BF16)],
        compiler_params=_cparams(("parallel",)),
        name="attn_a",
    )(a_qkv, a_qkv, a_qkv, a_qkv, bias, sink)


def _attn_b_kernel(q_ref, kp_ref, km_ref, kn_ref, vp_ref, vm_ref, vn_ref, bias_ref, o_ref,
                   k_scr, v_scr, o_scr, m_scr, l_scr, *, seg_starts, seg_ends):
    tb = ATT_TB
    h = B_HALF
    tok0 = pl.program_id(1) * tb
    is_first, is_last = _segment_flags(tok0, tb, seg_starts, seg_ends)
    k_scr[0:tb, :] = kp_ref[...]
    k_scr[tb:2 * tb, :] = km_ref[...]
    k_scr[2 * tb:, :] = kn_ref[...]
    v_scr[0:tb, :] = vp_ref[...]
    v_scr[tb:2 * tb, :] = vm_ref[...]
    v_scr[2 * tb:, :] = vn_ref[...]
    lane = lax.broadcasted_iota(jnp.int32, (1, LANES), 1)
    low = lane < HEAD_DIM
    col = lax.broadcasted_iota(jnp.int32, (1, 3 * h), 1)

    pen_first = jnp.where(jnp.logical_and(col < h, is_first), NEG_INF, 0.0)
    pen_last = jnp.where(jnp.logical_and(col >= 2 * h, is_last), NEG_INF, 0.0)

    def slices(d, r, b):
        row0 = r + h * d * b
        if d == 1:
            return pl.ds(row0, h), pl.ds(tb + row0 - h, 3 * h)
        return pl.ds(row0, h, stride=d), pl.ds(tb + row0 - h * d, 3 * h, stride=d)

    def scores(br, d, r, b):
        qs, ks = slices(d, r, b)
        q = q_ref[qs, :].astype(BF16)
        k = k_scr[ks, :].astype(BF16)
        qq = jnp.concatenate([jnp.where(low, q, 0), jnp.where(low, 0, q)], axis=0)
        return lax.dot_general(qq, k, (((1,), (1,)), ((), ())), preferred_element_type=F32)

    def finish(br, d, r, b, s):
        qs, ks = slices(d, r, b)
        s = s + bias_ref[br]
        if b == 0:
            s = s + pen_first
        if b == tb // (h * d) - 1:
            s = s + pen_last
        m = jnp.max(s, axis=-1, keepdims=True)
        e = jnp.exp(s - m)
        l = jnp.sum(e, axis=-1, keepdims=True)
        v = v_scr[ks, :].astype(BF16)
        pv = jnp.dot(e.astype(BF16), v, preferred_element_type=F32)
        o_scr[br, qs, :] = jnp.where(low, pv[:h], pv[h:])
        m_scr[br, qs, :] = jnp.where(low, m[:h], m[h:])
        l_scr[br, qs, :] = jnp.where(low, l[:h], l[h:])

    _pipelined([(br, d, r, b) for br, (_, d) in enumerate(B_BRANCHES)
                for r in range(d) for b in range(tb // (h * d))], scores, finish)

    m_all = jnp.maximum(jnp.maximum(m_scr[0], m_scr[1]), m_scr[2])
    num = jnp.zeros((tb, LANES), F32)
    den = jnp.zeros((tb, LANES), F32)
    for br in range(len(B_BRANCHES)):
        a = jnp.exp(m_scr[br] - m_all)
        num = num + a * o_scr[br]
        den = den + a * l_scr[br]
    o_ref[...] = (num / den).astype(o_ref.dtype)


def _attn_b(b_qkv, bias, seg_starts, seg_ends):
    n = b_qkv.shape[0]
    tb = ATT_TB
    nblk = n // tb
    npair = B_HEADS // 2
    prev = lambda i: jnp.maximum(i - 1, 0)
    nxt = lambda i: jnp.minimum(i + 1, nblk - 1)
    blk = lambda rowf, off: pl.BlockSpec((tb, LANES), lambda c, i: (rowf(i), off + c))
    same = lambda i: i
    stat = pltpu.VMEM((len(B_BRANCHES), tb, LANES), F32)
    return pl.pallas_call(
        functools.partial(_attn_b_kernel, seg_starts=seg_starts, seg_ends=seg_ends),
        grid=(npair, nblk),
        in_specs=[blk(same, 0),
                  blk(prev, npair), blk(same, npair), blk(nxt, npair),
                  blk(prev, 2 * npair), blk(same, 2 * npair), blk(nxt, 2 * npair),
                  pl.BlockSpec((None,) + bias.shape[1:], lambda c, i: (c, 0, 0, 0))],
        out_specs=pl.BlockSpec((tb, LANES), lambda c, i: (i, c)),
        out_shape=jax.ShapeDtypeStruct((n, B_W), BF16),
        scratch_shapes=[pltpu.VMEM((3 * tb, LANES), F32), pltpu.VMEM((3 * tb, LANES), F32),
                        stat, stat, stat],
        compiler_params=_cparams(("parallel", "parallel")),
        name="attn_b",
    )(b_qkv, b_qkv, b_qkv, b_qkv, b_qkv, b_qkv, b_qkv, bias)


C_HALO = (NA_ROWS // 2) * GRID_W

def _attn_c_kernel(q_ref, kp_ref, km_ref, kn_ref, vp_ref, vm_ref, vn_ref, bias_ref, o_ref,
                   k_scr, v_scr, *, seg_starts, seg_ends):
    tb = ATT_TB
    gw = GRID_W
    nkeys = NA_ROWS * gw
    tok0 = pl.program_id(1) * tb
    k_scr[0:C_HALO, :] = kp_ref[...]
    k_scr[C_HALO:C_HALO + tb, :] = km_ref[...]
    k_scr[C_HALO + tb:, :] = kn_ref[...]
    v_scr[0:C_HALO, :] = vp_ref[...]
    v_scr[C_HALO:C_HALO + tb, :] = vm_ref[...]
    v_scr[C_HALO + tb:, :] = vn_ref[...]
    seg_row0 = jnp.int32(0)
    seg_rows = jnp.int32(0)
    for s, e in zip(seg_starts, seg_ends):
        inside = jnp.logical_and(tok0 >= s, tok0 < e)
        seg_row0 = jnp.where(inside, s // gw, seg_row0)
        seg_rows = jnp.where(inside, (e - s) // gw, seg_rows)
    lane = lax.broadcasted_iota(jnp.int32, (1, LANES), 1)
    low = lane < HEAD_DIM

    def window(rr):
        rs = tok0 // gw + rr - seg_row0
        start = jnp.clip(rs - NA_ROWS // 2, 0, seg_rows - NA_ROWS)
        shift = rs - start
        return shift, pl.ds(pl.multiple_of((rr + NA_ROWS // 2 - shift) * gw, gw), nkeys)

    def scores(rr):
        _, ks = window(rr)
        q = q_ref[rr * gw:(rr + 1) * gw, :]
        qq = jnp.concatenate([jnp.where(low, q, 0), jnp.where(low, 0, q)], axis=0)
        return lax.dot_general(qq, k_scr[ks, :], (((1,), (1,)), ((), ())), preferred_element_type=F32)

    def finish(rr, s):
        shift, ks = window(rr)
        s = s + bias_ref[shift]
        m = jnp.max(s, axis=-1, keepdims=True)
        e = jnp.exp(s - m)
        den = jnp.sum(e, axis=-1, keepdims=True)
        pv = jnp.dot(e.astype(BF16), v_scr[ks, :], preferred_element_type=F32) * (1.0 / den)
        o_ref[rr * gw:(rr + 1) * gw, :] = jnp.where(low, pv[:gw], pv[gw:]).astype(o_ref.dtype)

    _pipelined([(rr,) for rr in range(tb // gw)], scores, finish)


def _attn_c(c_qkv, bias, seg_starts, seg_ends):
    n = c_qkv.shape[0]
    tb = ATT_TB
    npair = C_HEADS // 2
    sub = tb // C_HALO
    nhb = n // C_HALO
    main = lambda off: pl.BlockSpec((tb, LANES), lambda c, i: (i, off + c))
    prev = lambda off: pl.BlockSpec((C_HALO, LANES), lambda c, i: (jnp.maximum(i * sub - 1, 0), off + c))
    nxt = lambda off: pl.BlockSpec((C_HALO, LANES),
                                   lambda c, i: (jnp.minimum((i + 1) * sub, nhb - 1), off + c))
    return pl.pallas_call(
        functools.partial(_attn_c_kernel, seg_starts=seg_starts, seg_ends=seg_ends),
        grid=(npair, n // tb),
        in_specs=[main(0),
                  prev(npair), main(npair), nxt(npair),
                  prev(2 * npair), main(2 * npair), nxt(2 * npair),
                  pl.BlockSpec((None,) + bias.shape[1:], lambda c, i: (c, 0, 0, 0))],
        out_specs=pl.BlockSpec((tb, LANES), lambda c, i: (i, c)),
        out_shape=jax.ShapeDtypeStruct((n, C_W), BF16),
        scratch_shapes=[pltpu.VMEM((tb + 2 * C_HALO, LANES), BF16),
                        pltpu.VMEM((tb + 2 * C_HALO, LANES), BF16)],
        compiler_params=_cparams(("parallel", "parallel")),
        name="attn_c",
    )(c_qkv, c_qkv, c_qkv, c_qkv, c_qkv, c_qkv, c_qkv, bias)


def _router_kernel(x_ref, wh_ref, wl_ref, b_ref, info_ref, cnt_ref, run_scr):
    i = pl.program_id(0)
    tm = x_ref.shape[0]

    @pl.when(i == 0)
    def _():
        run_scr[...] = jnp.zeros_like(run_scr)

    x = x_ref[...]
    xh = x.astype(BF16)
    xl = (x - xh.astype(F32)).astype(BF16)
    logits = (jnp.dot(xh, wh_ref[...], preferred_element_type=F32)
              + jnp.dot(xl, wh_ref[...], preferred_element_type=F32)
              + jnp.dot(xh, wl_ref[...], preferred_element_type=F32)) + b_ref[...]
    lane = lax.broadcasted_iota(jnp.int32, (tm, LANES), 1)
    big = jnp.int32(LANES)
    is_g = lane < N_GROUPS
    lg = jnp.where(is_g, logits, NEG_INF)
    mg = jnp.max(lg, axis=-1, keepdims=True)
    g_sel = jnp.min(jnp.where(jnp.logical_and(is_g, lg == mg), lane, big), axis=-1, keepdims=True)
    e_lo = N_GROUPS + g_sel * EXPERTS_PER_GROUP
    in_grp = jnp.logical_and(lane >= e_lo, lane < e_lo + EXPERTS_PER_GROUP)
    le = jnp.where(in_grp, logits, NEG_INF)
    v1 = jnp.max(le, axis=-1, keepdims=True)
    i1 = jnp.min(jnp.where(jnp.logical_and(in_grp, le == v1), lane, big), axis=-1, keepdims=True)
    rest = jnp.logical_and(in_grp, lane != i1)
    le2 = jnp.where(rest, logits, NEG_INF)
    v2 = jnp.max(le2, axis=-1, keepdims=True)
    i2 = jnp.min(jnp.where(jnp.logical_and(rest, le2 == v2), lane, big), axis=-1, keepdims=True)
    a = jnp.minimum(i1, i2) - e_lo
    b = jnp.maximum(i1, i2) - e_lo
    pair = a * 3 - jnp.where(a == 2, 1, 0) + (b - a - 1)
    cls = g_sel * N_PAIRS + pair
    onehot = (lane == cls)
    ri = lax.broadcasted_iota(jnp.int32, (tm, tm), 0)
    ci = lax.broadcasted_iota(jnp.int32, (tm, tm), 1)
    tril = (ci < ri).astype(BF16)
    before = jnp.dot(tril, onehot.astype(BF16), preferred_element_type=F32) + run_scr[...]
    rank = jnp.sum(jnp.where(onehot, before, 0.0), axis=-1, keepdims=True)
    run_scr[...] = run_scr[...] + jnp.sum(onehot.astype(F32), axis=0, keepdims=True)
    rank_hi = jnp.floor(rank * (1.0 / RANK_BASE))
    rank_lo = rank - rank_hi * RANK_BASE
    cols = jnp.where(lane == 0, cls.astype(F32), jnp.where(lane == 1, rank_hi, jnp.where(lane == 2, rank_lo, 0.0)))
    pick = (lax.broadcasted_iota(jnp.int32, (SUBLANES, LANES), 0)
            == lax.broadcasted_iota(jnp.int32, (SUBLANES, LANES), 1)).astype(BF16)
    info_ref[...] = lax.dot_general(pick, cols.astype(BF16), (((1,), (1,)), ((), ())),
                                    preferred_element_type=F32)
    cnt_ref[...] = run_scr[...].astype(jnp.int32)


RANK_BASE = 256
SUBLANES = 8


def _router(x, w, b):
    n = x.shape[0]
    tm = MM_TM
    wh = w.astype(BF16)
    wl = (w - wh.astype(F32)).astype(BF16)
    return pl.pallas_call(
        _router_kernel,
        grid=(n // tm,),
        in_specs=[pl.BlockSpec((tm, D_MODEL), lambda i: (i, 0)),
                  pl.BlockSpec((D_MODEL, LANES), lambda i: (0, 0)),
                  pl.BlockSpec((D_MODEL, LANES), lambda i: (0, 0)),
                  pl.BlockSpec((1, LANES), lambda i: (0, 0))],
        out_specs=[pl.BlockSpec((None, SUBLANES, tm), lambda i: (i, 0, 0)),
                   pl.BlockSpec((1, LANES), lambda i: (0, 0))],
        out_shape=[jax.ShapeDtypeStruct((n // tm, SUBLANES, tm), F32),
                   jax.ShapeDtypeStruct((1, LANES), jnp.int32)],
        scratch_shapes=[pltpu.VMEM((1, LANES), F32)],
        compiler_params=_cparams(("arbitrary",)),
        name="router",
    )(x, wh, wl, b)


def _tile_copy(src, src_tok, dst, dst_tok, sem):
    return pltpu.make_async_copy(src.at[pl.ds(pl.multiple_of(src_tok * SUBLANES, SUBLANES), SUBLANES), :],
                                 dst.at[pl.ds(pl.multiple_of(dst_tok * SUBLANES, SUBLANES), SUBLANES), :], sem)


def _dispatch_kernel(dest_ref, pad_ref, x_ref, xs_ref, rec_scr, zero_scr, sem, zsem):
    i = pl.program_id(0)
    tm = x_ref.shape[0]
    tile_rows = MOE_TM * SUBLANES

    def zero_copy(c):
        start = pl.multiple_of(pad_ref[c] * SUBLANES, tile_rows)
        return pltpu.make_async_copy(zero_scr, xs_ref.at[pl.ds(start, tile_rows), :], zsem)

    @pl.when(i == 0)
    def _():
        zero_scr[...] = jnp.zeros_like(zero_scr)
        for c in range(N_CLASSES):
            @pl.when(pad_ref[c] >= 0)
            def _():
                zero_copy(c).start()
        for c in range(N_CLASSES):
            @pl.when(pad_ref[c] >= 0)
            def _():
                zero_copy(c).wait()

    for j in range(D_MODEL // LANES):
        rec_scr[pl.ds(j, tm, stride=SUBLANES), :] = x_ref[:, j * LANES:(j + 1) * LANES]

    def issue(r, carry):
        _tile_copy(rec_scr, r, xs_ref, dest_ref[i * tm + r], sem).start()
        return carry

    lax.fori_loop(0, tm, issue, 0)

    def drain(r, carry):
        _tile_copy(rec_scr, r, xs_ref, dest_ref[i * tm + r], sem).wait()
        return carry

    lax.fori_loop(0, tm, drain, 0)


def _dispatch(dest, pad_start, x, n_sorted):
    n = x.shape[0]
    tm = ROW_TM
    return pl.pallas_call(
        _dispatch_kernel,
        grid_spec=pltpu.PrefetchScalarGridSpec(
            num_scalar_prefetch=2,
            grid=(n // tm,),
            in_specs=[pl.BlockSpec((tm, D_MODEL), lambda i, d, p: (i, 0))],
            out_specs=pl.BlockSpec(memory_space=pl.ANY),
            scratch_shapes=[pltpu.VMEM((tm * SUBLANES, LANES), F32),
                            pltpu.VMEM((MOE_TM * SUBLANES, LANES), F32),
                            pltpu.SemaphoreType.DMA, pltpu.SemaphoreType.DMA]),
        out_shape=jax.ShapeDtypeStruct((n_sorted * SUBLANES, LANES), F32),
        compiler_params=_cparams(("arbitrary",)),
        name="dispatch",
    )(dest, pad_start, x)


def _expert_kernel(ea_ref, eb_ref, nt_ref, xs_ref, wr_ref, br_ref,
                   wga_ref, wua_ref, wda_ref, wgb_ref, wub_ref, wdb_ref, g_ref, b_ref, ys_ref):
    p = pl.program_id(0)
    tm = MOE_TM

    @pl.when(p < nt_ref[0])
    def _():
        x = jnp.concatenate([xs_ref[pl.ds(j, tm, stride=SUBLANES), :] for j in range(D_MODEL // LANES)],
                            axis=1)
        xb = x.astype(BF16)

        logits = jnp.dot(xb, wr_ref[...], preferred_element_type=F32) + br_ref[...]
        lane = lax.broadcasted_iota(jnp.int32, (tm, LANES), 1)
        lane_a = N_GROUPS + ea_ref[p]
        lane_b = N_GROUPS + eb_ref[p]
        grp = ea_ref[p] // EXPERTS_PER_GROUP
        is_g = lane < N_GROUPS
        mg = jnp.max(jnp.where(is_g, logits, NEG_INF), axis=-1, keepdims=True)
        eg = jnp.where(is_g, jnp.exp(logits - mg), 0.0)
        g_gate = (jnp.sum(jnp.where(lane == grp, eg, 0.0), axis=-1, keepdims=True)
                  / jnp.sum(eg, axis=-1, keepdims=True))
        l_a = jnp.sum(jnp.where(lane == lane_a, logits, 0.0), axis=-1, keepdims=True)
        l_b = jnp.sum(jnp.where(lane == lane_b, logits, 0.0), axis=-1, keepdims=True)
        mx = jnp.maximum(l_a, l_b)
        p_a = jnp.exp(l_a - mx)
        p_b = jnp.exp(l_b - mx)
        scale = g_gate / (p_a + p_b)
        w_a = p_a * scale
        w_b = p_b * scale

        def expert(wg_ref, wu_ref, wd_ref):
            gate = jnp.dot(xb, wg_ref[...], preferred_element_type=F32)
            up = jnp.dot(xb, wu_ref[...], preferred_element_type=F32)
            hid = (gate * (1.0 / (1.0 + jnp.exp(-gate))) * up).astype(BF16)
            return jnp.dot(hid, wd_ref[...], preferred_element_type=F32)

        y = w_a * expert(wga_ref, wua_ref, wda_ref) + w_b * expert(wgb_ref, wub_ref, wdb_ref)
        out = _layer_norm(DEEPNORM_ALPHA * x + y, g_ref[...], b_ref[...])
        for j in range(D_MODEL // LANES):
            ys_ref[pl.ds(j, tm, stride=SUBLANES), :] = out[:, j * LANES:(j + 1) * LANES]


def _experts(layer, tile_ea, tile_eb, n_tiles, xs, w_router, b_router, w_gate, w_up, w_down, g, b):
    tm = MOE_TM
    rows = tm * SUBLANES
    n_grid = xs.shape[0] // rows
    last = lambda p, nt: jnp.minimum(p, nt[0] - 1)
    wspec_a = lambda shape: pl.BlockSpec((None, None) + shape, lambda p, ea, eb, nt: (layer, ea[p], 0, 0))
    wspec_b = lambda shape: pl.BlockSpec((None, None) + shape, lambda p, ea, eb, nt: (layer, eb[p], 0, 0))
    const = lambda shape: pl.BlockSpec(shape, lambda p, ea, eb, nt: (0, 0))
    up_shape = (D_MODEL, D_EXPERT)
    dn_shape = (D_EXPERT, D_MODEL)
    return pl.pallas_call(
        _expert_kernel,
        grid_spec=pltpu.PrefetchScalarGridSpec(
            num_scalar_prefetch=3,
            grid=(n_grid,),
            in_specs=[pl.BlockSpec((rows, LANES), lambda p, ea, eb, nt: (last(p, nt), 0)),
                      const((D_MODEL, LANES)), const((1, LANES)),
                      wspec_a(up_shape), wspec_a(up_shape), wspec_a(dn_shape),
                      wspec_b(up_shape), wspec_b(up_shape), wspec_b(dn_shape),
                      const((1, D_MODEL)), const((1, D_MODEL))],
            out_specs=pl.BlockSpec((rows, LANES), lambda p, ea, eb, nt: (last(p, nt), 0))),
        out_shape=jax.ShapeDtypeStruct(xs.shape, F32),
        compiler_params=_cparams(("arbitrary",)),
        name="experts",
    )(tile_ea, tile_eb, n_tiles, xs, w_router.astype(BF16), b_router,
      w_gate, w_up, w_down, w_gate, w_up, w_down, g.reshape(1, D_MODEL), b.reshape(1, D_MODEL))


def _gather_kernel(dest_ref, ys_ref, out_ref, rec_scr, sem):
    i = pl.program_id(0)
    tm = out_ref.shape[0]

    def issue(r, carry):
        _tile_copy(ys_ref, dest_ref[i * tm + r], rec_scr, r, sem).start()
        return carry

    lax.fori_loop(0, tm, issue, 0)

    def drain(r, carry):
        _tile_copy(ys_ref, dest_ref[i * tm + r], rec_scr, r, sem).wait()
        return carry

    lax.fori_loop(0, tm, drain, 0)
    for j in range(D_MODEL // LANES):
        out_ref[:, j * LANES:(j + 1) * LANES] = rec_scr[pl.ds(j, tm, stride=SUBLANES), :]


def _gather_rows(dest, ys, n):
    tm = ROW_TM
    return pl.pallas_call(
        _gather_kernel,
        grid_spec=pltpu.PrefetchScalarGridSpec(
            num_scalar_prefetch=1,
            grid=(n // tm,),
            in_specs=[pl.BlockSpec(memory_space=pl.ANY)],
            out_specs=pl.BlockSpec((tm, D_MODEL), lambda i, d: (i, 0)),
            scratch_shapes=[pltpu.VMEM((tm * SUBLANES, LANES), F32), pltpu.SemaphoreType.DMA]),
        out_shape=jax.ShapeDtypeStruct((n, D_MODEL), F32),
        compiler_params=_cparams(("arbitrary",)),
        name="gather_rows",
    )(dest, ys)


_PAIR_A = np.array([0, 0, 0, 1, 1, 2], np.int32)
_PAIR_B = np.array([1, 2, 3, 2, 3, 3], np.int32)


def _moe_layer(layer, x, w_router, b_router, w_gate, w_up, w_down, g, b):
    n = x.shape[0]
    tm = MOE_TM
    n_sorted = n + N_CLASSES * tm
    info, counts = _router(x, w_router, b_router)
    cls = info[:, 0, :].reshape(n).astype(jnp.int32)
    rank = (info[:, 1, :] * RANK_BASE + info[:, 2, :]).reshape(n).astype(jnp.int32)
    counts = counts[0, :N_CLASSES]
    padded = (counts + tm - 1) // tm * tm
    classes = jnp.arange(N_CLASSES, dtype=jnp.int32)
    ends = jnp.sum(jnp.where(classes[None, :] <= classes[:, None], padded[None, :], 0), axis=1)
    offs = ends - padded
    total = ends[N_CLASSES - 1]
    dest = rank + jnp.sum(jnp.where(cls[:, None] == classes[None, :], offs[None, :], 0), axis=1)
    pad_start = jnp.where(padded > 0, ends - tm, -1).astype(jnp.int32)
    tile_start = jnp.arange(n_sorted // tm, dtype=jnp.int32) * tm
    tile_start = jnp.minimum(tile_start, total - tm)
    tile_cls = jnp.sum((ends[None, :] <= tile_start[:, None]).astype(jnp.int32), axis=1)
    pair = tile_cls % N_PAIRS
    pair_a = jnp.sum(jnp.where(pair[:, None] == np.arange(N_PAIRS)[None, :], _PAIR_A[None, :], 0), axis=1)
    pair_b = jnp.sum(jnp.where(pair[:, None] == np.arange(N_PAIRS)[None, :], _PAIR_B[None, :], 0), axis=1)
    grp = tile_cls // N_PAIRS
    tile_ea = (grp * EXPERTS_PER_GROUP + pair_a).astype(jnp.int32)
    tile_eb = (grp * EXPERTS_PER_GROUP + pair_b).astype(jnp.int32)
    n_tiles = (total // tm).astype(jnp.int32).reshape(1)
    xs = _dispatch(dest.astype(jnp.int32), pad_start, x, n_sorted)
    ys = _experts(layer, tile_ea, tile_eb, n_tiles, xs, w_router, b_router, w_gate, w_up, w_down, g, b)
    return _gather_rows(dest.astype(jnp.int32), ys, n)


_A_ORDER = np.array([0, 4, 1, 5, 2, 6, 3, 7])


def _prep_ab(w_in, w_out):
    qa = w_in[:, :QA_W].reshape(D_MODEL, A_HEADS, HEAD_DIM)[:, _A_ORDER].reshape(D_MODEL, QA_W) * ATTN_SCALE
    kva = w_in[:, QA_W:A_IN]
    qb = w_in[:, A_IN:A_IN + B_W] * ATTN_SCALE
    kvb = w_in[:, A_IN + B_W:]
    w = jnp.concatenate([qa, kva, qb, kvb], axis=1).astype(BF16)
    wo_a = w_out[:QA_W].reshape(A_HEADS, HEAD_DIM, D_MODEL)[_A_ORDER].reshape(QA_W, D_MODEL).astype(BF16)
    wo_b = w_out[QA_W:].astype(BF16)
    return w, wo_a, wo_b


def _prep_c(w_in, w_out):
    w = jnp.concatenate([w_in[:, :C_W] * ATTN_SCALE, w_in[:, C_W:]], axis=1).astype(BF16)
    return w, w_out.astype(BF16)


def _trunk(x, seg_starts, seg_ends, rel_bias, w_in_ab, a_sink, w_out_ab, w_in_c, c_rpb, w_out_c,
           ln1_g, ln1_b, ln2_g, ln2_b, router_g_w, router_g_b, router_e_w, router_e_b,
           w_gate, w_up, w_down):
    bias_a = _bias_a(rel_bias)
    bias_b = _bias_b(rel_bias)
    w_gate = w_gate.astype(BF16)
    w_up = w_up.astype(BF16)
    w_down = w_down.astype(BF16)
    for l in range(DEPTH):
        i = l // 2
        if l % 2 == 0:
            w, wo_a, wo_b = _prep_ab(w_in_ab[i], w_out_ab[i])
            a_qkv, b_qkv = _inproj(x, w, ((0, A_IN), (A_IN, A_IN + B_IN)), (BF16, F32))
            o_a = _attn_a(a_qkv, bias_a, a_sink[i].astype(F32), seg_starts, seg_ends)
            o_b = _attn_b(b_qkv, bias_b, seg_starts, seg_ends)
            x = _outproj_ln([o_a, o_b], [wo_a, wo_b], x, ln1_g[l], ln1_b[l])
        else:
            w, wo = _prep_c(w_in_c[i], w_out_c[i])
            (c_qkv,) = _inproj(x, w, ((0, 3 * C_W),), (BF16,))
            o_c = _attn_c(c_qkv, _bias_c(c_rpb[i]), seg_starts, seg_ends)
            x = _outproj_ln([o_c], [wo], x, ln1_g[l], ln1_b[l])
        pad = LANES - N_GROUPS - N_EXPERTS
        w_router = jnp.pad(jnp.concatenate([router_g_w[l], router_e_w[l]], axis=1), ((0, 0), (0, pad)))
        b_router = jnp.pad(jnp.concatenate([router_g_b[l], router_e_b[l]]), (0, pad)).reshape(1, LANES)
        x = _moe_layer(l, x, w_router, b_router, w_gate, w_up, w_down, ln2_g[l], ln2_b[l])
    return x


def kernel(x_prompt, x_sample, rel_bias, w_in_ab, a_sink, w_out_ab, w_in_c, c_rpb, w_out_c,
           ln1_g, ln1_b, ln2_g, ln2_b, router_g_w, router_g_b, router_e_w, router_e_b,
           w_gate, w_up, w_down):
    seqs = [x_prompt[b] for b in range(x_prompt.shape[0])] + [x_sample[b] for b in range(x_sample.shape[0])]
    lens = [s.shape[0] for s in seqs]
    seg_ends = tuple(int(v) for v in np.cumsum(lens))
    seg_starts = tuple(e - n for e, n in zip(seg_ends, lens))
    for n in lens:
        assert n % ATT_TB == 0 and n // GRID_W >= NA_ROWS
    x = jnp.concatenate(seqs, axis=0)
    y = _trunk(x, seg_starts, seg_ends, rel_bias, w_in_ab, a_sink, w_out_ab, w_in_c, c_rpb, w_out_c,
               ln1_g, ln1_b, ln2_g, ln2_b, router_g_w, router_g_b, router_e_w, router_e_b,
               w_gate, w_up, w_down)
    n_p = x_prompt.shape[0] * x_prompt.shape[1]
    return (y[:n_p].reshape(x_prompt.shape), y[n_p:].reshape(x_sample.shape))
```

```python
import functools
import math

import numpy as np
import jax
import jax.numpy as jnp
from jax import lax
from jax.experimental import pallas as pl
from jax.experimental.pallas import tpu as pltpu

F32 = jnp.float32
BF16 = jnp.bfloat16

D_MODEL = 1024
DEPTH = 4
HEAD_DIM = 64
LANES = 128
A_HEADS = 8
A_KV_HEADS = 2
A_WINDOW = 128
B_HEADS = 8
B_BRANCHES = ((128, 1), (512, 4), (2048, 16))
B_HALF = 64
C_HEADS = 16
GRID_W = 64
NA_ROWS = 8
NA_COLS = 16
REL_BUCKETS = 32
REL_MAX_DIST = 1024
N_GROUPS = 4
EXPERTS_PER_GROUP = 4
N_EXPERTS = 16
D_EXPERT = 512
N_PAIRS = 6
N_CLASSES = N_GROUPS * N_PAIRS
DEEPNORM_ALPHA = (2.0 * DEPTH) ** 0.25
LN_EPS = 1e-5
ATTN_SCALE = HEAD_DIM ** -0.5
NEG_INF = -1e30

QA_W = A_HEADS * HEAD_DIM
KVA_W = A_KV_HEADS * HEAD_DIM
A_IN = QA_W + 2 * KVA_W
B_W = B_HEADS * HEAD_DIM
B_IN = 3 * B_W
C_W = C_HEADS * HEAD_DIM

ATT_TB = 1024
MM_TM = 512
MOE_TM = 256
ROW_TM = 512
VMEM_LIMIT = 56 * 1024 * 1024


def _cparams(sem):
    return pltpu.CompilerParams(dimension_semantics=sem, vmem_limit_bytes=VMEM_LIMIT)


def _segment_flags(tok0, size, seg_starts, seg_ends):
    is_first = functools.reduce(jnp.logical_or, [tok0 == s for s in seg_starts])
    is_last = functools.reduce(jnp.logical_or, [tok0 + size == e for e in seg_ends])
    return is_first, is_last


def _t5_bucket_np(rel):
    half_b = REL_BUCKETS // 2
    max_exact = half_b // 2
    n = np.abs(rel)
    large = max_exact + (np.log(np.maximum(n, max_exact).astype(np.float32) / max_exact)
                         / math.log(REL_MAX_DIST / max_exact) * (half_b - max_exact)).astype(np.int32)
    large = np.minimum(large, half_b - 1)
    return np.where(rel > 0, half_b, 0) + np.where(n < max_exact, n, large)


def _banded_bias(table, half, dil):
    rel = np.arange(3 * half)[None, :] - half - np.arange(half)[:, None]
    bucket = jnp.asarray(_t5_bucket_np(rel * dil).astype(np.int32))
    hit = bucket[None] == jnp.arange(REL_BUCKETS, dtype=jnp.int32)[:, None, None]
    bias = jnp.sum(jnp.where(hit[:, None], table.astype(F32)[:, :, None, None], 0.0), axis=0)
    return jnp.where(jnp.asarray(np.abs(rel) <= half)[None], bias, NEG_INF)


def _bias_a(rel_bias):
    w = A_WINDOW
    ch = A_CHUNK
    b = _banded_bias(rel_bias[:, :A_HEADS], w, 1).reshape(A_KV_HEADS, 4, w // ch, ch, 3 * w)
    b = b.transpose(0, 2, 1, 3, 4).reshape(A_KV_HEADS, w // ch, 4 * ch, 3 * w)
    col = np.arange(3 * w)
    first = jnp.where(jnp.asarray(col < w), NEG_INF, b)
    last = jnp.where(jnp.asarray(col >= 2 * w), NEG_INF, b)
    return jnp.stack([b, first, last])


def _bias_b(rel_bias):
    per = [_banded_bias(rel_bias[:, A_HEADS:], B_HALF, d) for _, d in B_BRANCHES]
    b = jnp.stack(per, axis=1)
    return b.reshape(B_HEADS // 2, 2, len(B_BRANCHES), B_HALF, 3 * B_HALF).transpose(0, 2, 1, 3, 4) \
            .reshape(B_HEADS // 2, len(B_BRANCHES), 2 * B_HALF, 3 * B_HALF)


def _bias_c(rpb):
    gw = GRID_W
    n_dr = 2 * NA_ROWS - 1
    side = gw - NA_COLS
    p = jnp.concatenate([jnp.repeat(rpb[..., :1], side, axis=-1), rpb.astype(F32),
                         jnp.repeat(rpb[..., -1:], side + 1, axis=-1)], axis=-1)
    z = jnp.broadcast_to(p[:, :, None, :], (C_HEADS, n_dr, gw, 2 * gw)).reshape(C_HEADS, n_dr, 2 * gw * gw)
    t = z[:, :, gw - 1:gw - 1 + gw * (2 * gw - 1)].reshape(C_HEADS, n_dr, gw, 2 * gw - 1)[..., :gw]
    cq = np.arange(gw)[:, None]
    w = np.arange(gw)[None, :]
    c0 = np.clip(cq - NA_COLS // 2, 0, gw - NA_COLS)
    t = jnp.where(jnp.asarray((w >= c0) & (w < c0 + NA_COLS)), t, NEG_INF)
    bias = jnp.stack([jnp.transpose(t[:, NA_ROWS - 1 - s:2 * NA_ROWS - 1 - s], (0, 2, 1, 3))
                      for s in range(NA_ROWS)], axis=1)
    bias = bias.reshape(C_HEADS // 2, 2, NA_ROWS, GRID_W, NA_ROWS * GRID_W)
    return bias.transpose(0, 2, 1, 3, 4).reshape(C_HEADS // 2, NA_ROWS, 2 * GRID_W, NA_ROWS * GRID_W)


def _inproj_kernel(x_ref, w_ref, *o_refs, splits):
    x = x_ref[...].astype(BF16)
    for o_ref, (lo, hi) in zip(o_refs, splits):
        o_ref[...] = jnp.dot(x, w_ref[:, lo:hi], preferred_element_type=F32).astype(o_ref.dtype)


def _inproj(x, w, splits, dtypes):
    n = x.shape[0]
    return pl.pallas_call(
        functools.partial(_inproj_kernel, splits=splits),
        grid=(n // MM_TM,),
        in_specs=[pl.BlockSpec((MM_TM, D_MODEL), lambda i: (i, 0)),
                  pl.BlockSpec(w.shape, lambda i: (0, 0))],
        out_specs=[pl.BlockSpec((MM_TM, hi - lo), lambda i: (i, 0)) for lo, hi in splits],
        out_shape=[jax.ShapeDtypeStruct((n, hi - lo), dt) for (lo, hi), dt in zip(splits, dtypes)],
        compiler_params=_cparams(("parallel",)),
        name="inproj",
    )(x, w)


def _layer_norm(z, g, b):
    mu = jnp.mean(z, axis=-1, keepdims=True)
    zc = z - mu
    var = jnp.mean(zc * zc, axis=-1, keepdims=True)
    return zc * lax.rsqrt(var + LN_EPS) * g + b


def _outproj_ln_kernel(*refs, n_parts):
    o_refs = refs[:n_parts]
    w_refs = refs[n_parts:2 * n_parts]
    x_ref, g_ref, b_ref, out_ref = refs[2 * n_parts:]
    h = DEEPNORM_ALPHA * x_ref[...]
    for o_ref, w_ref in zip(o_refs, w_refs):
        h = h + jnp.dot(o_ref[...], w_ref[...], preferred_element_type=F32)
    out_ref[...] = _layer_norm(h, g_ref[...], b_ref[...])


def _outproj_ln(parts, weights, x, g, b):
    n = x.shape[0]
    n_parts = len(parts)
    return pl.pallas_call(
        functools.partial(_outproj_ln_kernel, n_parts=n_parts),
        grid=(n // MM_TM,),
        in_specs=([pl.BlockSpec((MM_TM, p.shape[1]), lambda i: (i, 0)) for p in parts]
                  + [pl.BlockSpec(w.shape, lambda i: (0, 0)) for w in weights]
                  + [pl.BlockSpec((MM_TM, D_MODEL), lambda i: (i, 0)),
                     pl.BlockSpec((1, D_MODEL), lambda i: (0, 0)),
                     pl.BlockSpec((1, D_MODEL), lambda i: (0, 0))]),
        out_specs=pl.BlockSpec((MM_TM, D_MODEL), lambda i: (i, 0)),
        out_shape=jax.ShapeDtypeStruct((n, D_MODEL), F32),
        compiler_params=_cparams(("parallel",)),
        name="outproj_ln",
    )(*parts, *weights, x, g.reshape(1, D_MODEL), b.reshape(1, D_MODEL))


ATT_DEPTH = 3


def _staged(n, weights, values):
    for i in range(min(ATT_DEPTH, n)):
        weights(i)
    for i in range(n):
        if i + ATT_DEPTH < n:
            weights(i + ATT_DEPTH)
        values(i)


def _attn_a_kernel(q_ref, kvm_ref, kvp_ref, kvn_ref, bias_ref, sink_ref, o_ref, kv_scr, p_scr,
                   *, seg_starts, seg_ends):
    w = A_WINDOW
    n_sub = ATT_TB // w
    tok0 = pl.program_id(0) * ATT_TB
    is_first, is_last = _segment_flags(tok0, ATT_TB, seg_starts, seg_ends)
    kv_scr[0:w, :] = kvp_ref[...]
    kv_scr[w:w + ATT_TB, :] = kvm_ref[...]
    kv_scr[w + ATT_TB:, :] = kvn_ref[...]
    low = lax.broadcasted_iota(jnp.int32, (1, LANES), 1) < HEAD_DIM
    ch = A_CHUNK
    per = w // ch

    sinks = [jnp.concatenate([jnp.full((ch, 1), sink_ref[c + 4 * g], F32) for c in range(4)], axis=0)
             for g in range(A_KV_HEADS)]

    def scores(t, g):
        j = t // per
        q = q_ref[t * ch:(t + 1) * ch, :]
        qg = jnp.concatenate([jnp.where(low, q[:, c * LANES:(c + 1) * LANES], 0) if g == 0
                              else jnp.where(low, 0, q[:, c * LANES:(c + 1) * LANES]) for c in range(4)], axis=0)
        k2 = kv_scr[j * w:(j + 3) * w, :LANES]
        return lax.dot_general(qg, k2, (((1,), (1,)), ((), ())), preferred_element_type=F32)

    tiles = [(t, g) for t in range(ATT_TB // ch) for g in range(A_KV_HEADS)]
    rows = 4 * ch
    rdens = {}
    outs = {}

    def weights(i):
        t, g = tiles[i]
        j = t // per
        if j == 0:
            variant = jnp.where(is_first, 1, 0)
        elif j == n_sub - 1:
            variant = jnp.where(is_last, 2, 0)
        else:
            variant = 0
        s = scores(t, g) + bias_ref[variant, g, t % per]
        m = jnp.maximum(jnp.max(s, axis=-1, keepdims=True), sinks[g])
        e = jnp.exp(s - m)
        rdens[i] = 1.0 / (jnp.sum(e, axis=-1, keepdims=True) + jnp.exp(sinks[g] - m))
        p_scr[i * rows:(i + 1) * rows, :] = e.astype(BF16)

    def values(i):
        t, g = tiles[i]
        j = t // per
        v2 = kv_scr[j * w:(j + 3) * w, LANES:]
        outs[g] = jnp.dot(p_scr[i * rows:(i + 1) * rows, :], v2, preferred_element_type=F32) * rdens[i]
        if g == A_KV_HEADS - 1:
            for c in range(4):
                oc = jnp.where(low, outs[0][c * ch:(c + 1) * ch], outs[1][c * ch:(c + 1) * ch])
                o_ref[t * ch:(t + 1) * ch, c * LANES:(c + 1) * LANES] = oc.astype(o_ref.dtype)

    _staged(len(tiles), weights, values)


A_CHUNK = 64


def _attn_a(a_qkv, bias, sink, seg_starts, seg_ends):
    n = a_qkv.shape[0]
    w = A_WINDOW
    sub = ATT_TB // w
    nhb = n // w
    kv_col = QA_W // (2 * LANES)
    return pl.pallas_call(
        functools.partial(_attn_a_kernel, seg_starts=seg_starts, seg_ends=seg_ends),
        grid=(n // ATT_TB,),
        in_specs=[pl.BlockSpec((ATT_TB, QA_W), lambda i: (i, 0)),
                  pl.BlockSpec((ATT_TB, 2 * LANES), lambda i: (i, kv_col)),
                  pl.BlockSpec((w, 2 * LANES), lambda i: (jnp.maximum(i * sub - 1, 0), kv_col)),
                  pl.BlockSpec((w, 2 * LANES), lambda i: (jnp.minimum((i + 1) * sub, nhb - 1), kv_col)),
                  pl.BlockSpec(bias.shape, lambda i: (0,) * bias.ndim),
                  pl.BlockSpec(memory_space=pltpu.SMEM)],
        out_specs=pl.BlockSpec((ATT_TB, QA_W), lambda i: (i, 0)),
        out_shape=jax.ShapeDtypeStruct((n, QA_W), BF16),
        scratch_shapes=[pltpu.VMEM((ATT_TB + 2 * w, 2 * LANES), BF16),
                        pltpu.VMEM((A_HEADS * ATT_TB, 3 * w), BF16)],
        compiler_params=_cparams(("parallel",)),
        name="attn_a",
    )(a_qkv, a_qkv, a_qkv, a_qkv, bias, sink)


def _attn_b_kernel(q_ref, kp_ref, km_ref, kn_ref, vp_ref, vm_ref, vn_ref, bias_ref, o_ref,
                   k_scr, v_scr, o_scr, m_scr, l_scr, p_scr, *, seg_starts, seg_ends):
    tb = ATT_TB
    h = B_HALF
    tok0 = pl.program_id(1) * tb
    is_first, is_last = _segment_flags(tok0, tb, seg_starts, seg_ends)
    k_scr[0:tb, :] = kp_ref[...]
    k_scr[tb:2 * tb, :] = km_ref[...]
    k_scr[2 * tb:, :] = kn_ref[...]
    v_scr[0:tb, :] = vp_ref[...]
    v_scr[tb:2 * tb, :] = vm_ref[...]
    v_scr[2 * tb:, :] = vn_ref[...]
    lane = lax.broadcasted_iota(jnp.int32, (1, LANES), 1)
    low = lane < HEAD_DIM
    col = lax.broadcasted_iota(jnp.int32, (1, 3 * h), 1)

    pen_first = jnp.where(jnp.logical_and(col < h, is_first), NEG_INF, 0.0)
    pen_last = jnp.where(jnp.logical_and(col >= 2 * h, is_last), NEG_INF, 0.0)

    def slices(d, r, b):
        row0 = r + h * d * b
        if d == 1:
            return pl.ds(row0, h), pl.ds(tb + row0 - h, 3 * h)
        return pl.ds(row0, h, stride=d), pl.ds(tb + row0 - h * d, 3 * h, stride=d)

    def scores(br, d, r, b):
        qs, ks = slices(d, r, b)
        q = q_ref[qs, :].astype(BF16)
        k = k_scr[ks, :].astype(BF16)
        qq = jnp.concatenate([jnp.where(low, q, 0), jnp.where(low, 0, q)], axis=0)
        return lax.dot_general(qq, k, (((1,), (1,)), ((), ())), preferred_element_type=F32)

    tiles = [(br, d, r, b) for br, (_, d) in enumerate(B_BRANCHES)
             for r in range(d) for b in range(tb // (h * d))]

    def weights(i):
        br, d, r, b = tiles[i]
        qs, _ = slices(d, r, b)
        s = scores(br, d, r, b) + bias_ref[br]
        if b == 0:
            s = s + pen_first
        if b == tb // (h * d) - 1:
            s = s + pen_last
        m = jnp.max(s, axis=-1, keepdims=True)
        e = jnp.exp(s - m)
        l = jnp.sum(e, axis=-1, keepdims=True)
        p_scr[i * 2 * h:(i + 1) * 2 * h, :] = e.astype(BF16)
        m_scr[br, qs, :] = jnp.where(low, m[:h], m[h:])
        l_scr[br, qs, :] = jnp.where(low, l[:h], l[h:])

    def values(i):
        br, d, r, b = tiles[i]
        qs, ks = slices(d, r, b)
        v = v_scr[ks, :].astype(BF16)
        pv = jnp.dot(p_scr[i * 2 * h:(i + 1) * 2 * h, :], v, preferred_element_type=F32)
        o_scr[br, qs, :] = jnp.where(low, pv[:h], pv[h:])

    _staged(len(tiles), weights, values)

    m_all = jnp.maximum(jnp.maximum(m_scr[0], m_scr[1]), m_scr[2])
    num = jnp.zeros((tb, LANES), F32)
    den = jnp.zeros((tb, LANES), F32)
    for br in range(len(B_BRANCHES)):
        a = jnp.exp(m_scr[br] - m_all)
        num = num + a * o_scr[br]
        den = den + a * l_scr[br]
    o_ref[...] = (num / den).astype(o_ref.dtype)


def _attn_b(b_qkv, bias, seg_starts, seg_ends):
    n = b_qkv.shape[0]
    tb = ATT_TB
    nblk = n // tb
    npair = B_HEADS // 2
    prev = lambda i: jnp.maximum(i - 1, 0)
    nxt = lambda i: jnp.minimum(i + 1, nblk - 1)
    blk = lambda rowf, off: pl.BlockSpec((tb, LANES), lambda c, i: (rowf(i), off + c))
    same = lambda i: i
    stat = pltpu.VMEM((len(B_BRANCHES), tb, LANES), F32)
    return pl.pallas_call(
        functools.partial(_attn_b_kernel, seg_starts=seg_starts, seg_ends=seg_ends),
        grid=(npair, nblk),
        in_specs=[blk(same, 0),
                  blk(prev, npair), blk(same, npair), blk(nxt, npair),
                  blk(prev, 2 * npair), blk(same, 2 * npair), blk(nxt, 2 * npair),
                  pl.BlockSpec((None,) + bias.shape[1:], lambda c, i: (c, 0, 0, 0))],
        out_specs=pl.BlockSpec((tb, LANES), lambda c, i: (i, c)),
        out_shape=jax.ShapeDtypeStruct((n, B_W), BF16),
        scratch_shapes=[pltpu.VMEM((3 * tb, LANES), F32), pltpu.VMEM((3 * tb, LANES), F32),
                        stat, stat, stat,
                        pltpu.VMEM((len(B_BRANCHES) * 2 * tb, 3 * B_HALF), BF16)],
        compiler_params=_cparams(("parallel", "parallel")),
        name="attn_b",
    )(b_qkv, b_qkv, b_qkv, b_qkv, b_qkv, b_qkv, b_qkv, bias)


C_HALO = (NA_ROWS // 2) * GRID_W

def _attn_c_kernel(q_ref, kp_ref, km_ref, kn_ref, vp_ref, vm_ref, vn_ref, bias_ref, o_ref,
                   k_scr, v_scr, p_scr, *, seg_starts, seg_ends):
    tb = ATT_TB
    gw = GRID_W
    nkeys = NA_ROWS * gw
    tok0 = pl.program_id(1) * tb
    k_scr[0:C_HALO, :] = kp_ref[...]
    k_scr[C_HALO:C_HALO + tb, :] = km_ref[...]
    k_scr[C_HALO + tb:, :] = kn_ref[...]
    v_scr[0:C_HALO, :] = vp_ref[...]
    v_scr[C_HALO:C_HALO + tb, :] = vm_ref[...]
    v_scr[C_HALO + tb:, :] = vn_ref[...]
    seg_row0 = jnp.int32(0)
    seg_rows = jnp.int32(0)
    for s, e in zip(seg_starts, seg_ends):
        inside = jnp.logical_and(tok0 >= s, tok0 < e)
        seg_row0 = jnp.where(inside, s // gw, seg_row0)
        seg_rows = jnp.where(inside, (e - s) // gw, seg_rows)
    lane = lax.broadcasted_iota(jnp.int32, (1, LANES), 1)
    low = lane < HEAD_DIM

    def window(rr):
        rs = tok0 // gw + rr - seg_row0
        start = jnp.clip(rs - NA_ROWS // 2, 0, seg_rows - NA_ROWS)
        shift = rs - start
        return shift, pl.ds(pl.multiple_of((rr + NA_ROWS // 2 - shift) * gw, gw), nkeys)

    def scores(rr):
        _, ks = window(rr)
        q = q_ref[rr * gw:(rr + 1) * gw, :]
        qq = jnp.concatenate([jnp.where(low, q, 0), jnp.where(low, 0, q)], axis=0)
        return lax.dot_general(qq, k_scr[ks, :], (((1,), (1,)), ((), ())), preferred_element_type=F32)

    n_rows = tb // gw
    rdens = {}

    def weights(rr):
        shift, _ = window(rr)
        s = scores(rr) + bias_ref[shift]
        m = jnp.max(s, axis=-1, keepdims=True)
        e = jnp.exp(s - m)
        rdens[rr] = 1.0 / jnp.sum(e, axis=-1, keepdims=True)
        p_scr[rr * 2 * gw:(rr + 1) * 2 * gw, :] = e.astype(BF16)

    def values(rr):
        _, ks = window(rr)
        pv = jnp.dot(p_scr[rr * 2 * gw:(rr + 1) * 2 * gw, :], v_scr[ks, :], preferred_element_type=F32) * rdens[rr]
        o_ref[rr * gw:(rr + 1) * gw, :] = jnp.where(low, pv[:gw], pv[gw:]).astype(o_ref.dtype)

    _staged(n_rows, weights, values)


def _attn_c(c_qkv, bias, seg_starts, seg_ends):
    n = c_qkv.shape[0]
    tb = ATT_TB
    npair = C_HEADS // 2
    sub = tb // C_HALO
    nhb = n // C_HALO
    main = lambda off: pl.BlockSpec((tb, LANES), lambda c, i: (i, off + c))
    prev = lambda off: pl.BlockSpec((C_HALO, LANES), lambda c, i: (jnp.maximum(i * sub - 1, 0), off + c))
    nxt = lambda off: pl.BlockSpec((C_HALO, LANES),
                                   lambda c, i: (jnp.minimum((i + 1) * sub, nhb - 1), off + c))
    return pl.pallas_call(
        functools.partial(_attn_c_kernel, seg_starts=seg_starts, seg_ends=seg_ends),
        grid=(npair, n // tb),
        in_specs=[main(0),
                  prev(npair), main(npair), nxt(npair),
                  prev(2 * npair), main(2 * npair), nxt(2 * npair),
                  pl.BlockSpec((None,) + bias.shape[1:], lambda c, i: (c, 0, 0, 0))],
        out_specs=pl.BlockSpec((tb, LANES), lambda c, i: (i, c)),
        out_shape=jax.ShapeDtypeStruct((n, C_W), BF16),
        scratch_shapes=[pltpu.VMEM((tb + 2 * C_HALO, LANES), BF16),
                        pltpu.VMEM((tb + 2 * C_HALO, LANES), BF16),
                        pltpu.VMEM((2 * tb, NA_ROWS * GRID_W), BF16)],
        compiler_params=_cparams(("parallel", "parallel")),
        name="attn_c",
    )(c_qkv, c_qkv, c_qkv, c_qkv, c_qkv, c_qkv, c_qkv, bias)


def _router_kernel(x_ref, wh_ref, wl_ref, b_ref, info_ref, cnt_ref, run_scr):
    i = pl.program_id(0)
    tm = x_ref.shape[0]

    @pl.when(i == 0)
    def _():
        run_scr[...] = jnp.zeros_like(run_scr)

    x = x_ref[...]
    xh = x.astype(BF16)
    xl = (x - xh.astype(F32)).astype(BF16)
    logits = (jnp.dot(xh, wh_ref[...], preferred_element_type=F32)
              + jnp.dot(xl, wh_ref[...], preferred_element_type=F32)
              + jnp.dot(xh, wl_ref[...], preferred_element_type=F32)) + b_ref[...]
    lane = lax.broadcasted_iota(jnp.int32, (tm, LANES), 1)
    big = jnp.int32(LANES)
    is_g = lane < N_GROUPS
    lg = jnp.where(is_g, logits, NEG_INF)
    mg = jnp.max(lg, axis=-1, keepdims=True)
    g_sel = jnp.min(jnp.where(jnp.logical_and(is_g, lg == mg), lane, big), axis=-1, keepdims=True)
    e_lo = N_GROUPS + g_sel * EXPERTS_PER_GROUP
    in_grp = jnp.logical_and(lane >= e_lo, lane < e_lo + EXPERTS_PER_GROUP)
    le = jnp.where(in_grp, logits, NEG_INF)
    v1 = jnp.max(le, axis=-1, keepdims=True)
    i1 = jnp.min(jnp.where(jnp.logical_and(in_grp, le == v1), lane, big), axis=-1, keepdims=True)
    rest = jnp.logical_and(in_grp, lane != i1)
    le2 = jnp.where(rest, logits, NEG_INF)
    v2 = jnp.max(le2, axis=-1, keepdims=True)
    i2 = jnp.min(jnp.where(jnp.logical_and(rest, le2 == v2), lane, big), axis=-1, keepdims=True)
    a = jnp.minimum(i1, i2) - e_lo
    b = jnp.maximum(i1, i2) - e_lo
    pair = a * 3 - jnp.where(a == 2, 1, 0) + (b - a - 1)
    cls = g_sel * N_PAIRS + pair
    onehot = (lane == cls)
    ri = lax.broadcasted_iota(jnp.int32, (tm, tm), 0)
    ci = lax.broadcasted_iota(jnp.int32, (tm, tm), 1)
    tril = (ci < ri).astype(BF16)
    before = jnp.dot(tril, onehot.astype(BF16), preferred_element_type=F32) + run_scr[...]
    rank = jnp.sum(jnp.where(onehot, before, 0.0), axis=-1, keepdims=True)
    run_scr[...] = run_scr[...] + jnp.sum(onehot.astype(F32), axis=0, keepdims=True)
    rank_hi = jnp.floor(rank * (1.0 / RANK_BASE))
    rank_lo = rank - rank_hi * RANK_BASE
    cols = jnp.where(lane == 0, cls.astype(F32), jnp.where(lane == 1, rank_hi, jnp.where(lane == 2, rank_lo, 0.0)))
    pick = (lax.broadcasted_iota(jnp.int32, (SUBLANES, LANES), 0)
            == lax.broadcasted_iota(jnp.int32, (SUBLANES, LANES), 1)).astype(BF16)
    info_ref[...] = lax.dot_general(pick, cols.astype(BF16), (((1,), (1,)), ((), ())),
                                    preferred_element_type=F32)
    cnt_ref[...] = run_scr[...].astype(jnp.int32)


RANK_BASE = 256
SUBLANES = 8


def _router(x, w, b):
    n = x.shape[0]
    tm = MM_TM
    wh = w.astype(BF16)
    wl = (w - wh.astype(F32)).astype(BF16)
    return pl.pallas_call(
        _router_kernel,
        grid=(n // tm,),
        in_specs=[pl.BlockSpec((tm, D_MODEL), lambda i: (i, 0)),
                  pl.BlockSpec((D_MODEL, LANES), lambda i: (0, 0)),
                  pl.BlockSpec((D_MODEL, LANES), lambda i: (0, 0)),
                  pl.BlockSpec((1, LANES), lambda i: (0, 0))],
        out_specs=[pl.BlockSpec((None, SUBLANES, tm), lambda i: (i, 0, 0)),
                   pl.BlockSpec((1, LANES), lambda i: (0, 0))],
        out_shape=[jax.ShapeDtypeStruct((n // tm, SUBLANES, tm), F32),
                   jax.ShapeDtypeStruct((1, LANES), jnp.int32)],
        scratch_shapes=[pltpu.VMEM((1, LANES), F32)],
        compiler_params=_cparams(("arbitrary",)),
        name="router",
    )(x, wh, wl, b)


def _tile_copy(src, src_tok, dst, dst_tok, sem):
    return pltpu.make_async_copy(src.at[pl.ds(pl.multiple_of(src_tok * SUBLANES, SUBLANES), SUBLANES), :],
                                 dst.at[pl.ds(pl.multiple_of(dst_tok * SUBLANES, SUBLANES), SUBLANES), :], sem)


ROW_UNROLL = 8


def _start_rows(copy, n):
    def body(g, carry):
        for u in range(ROW_UNROLL):
            copy(g * ROW_UNROLL + u).start(priority=u % 2)
        return carry

    lax.fori_loop(0, n // ROW_UNROLL, body, 0)


def _dispatch_kernel(dest_ref, pad_ref, x_ref, xs_ref, rec_scr, zero_scr, sems, zsem):
    i = pl.program_id(0)
    last = pl.num_programs(0) - 1
    tm = x_ref.shape[0]
    rows = tm * SUBLANES
    slot = i % 2
    tile_rows = MOE_TM * SUBLANES

    def zero_copy(c):
        start = pl.multiple_of(pad_ref[c] * SUBLANES, tile_rows)
        return pltpu.make_async_copy(zero_scr, xs_ref.at[pl.ds(start, tile_rows), :], zsem)

    @pl.when(i == 0)
    def _():
        zero_scr[...] = jnp.zeros_like(zero_scr)
        for c in range(N_CLASSES):
            @pl.when(pad_ref[c] >= 0)
            def _():
                zero_copy(c).start()
        for c in range(N_CLASSES):
            @pl.when(pad_ref[c] >= 0)
            def _():
                zero_copy(c).wait()

    for j in range(D_MODEL // LANES):
        rec_scr[slot, pl.ds(j, tm, stride=SUBLANES), :] = x_ref[:, j * LANES:(j + 1) * LANES]

    _start_rows(lambda r: _tile_copy(rec_scr.at[slot], r, xs_ref, dest_ref[i * tm + r], sems.at[slot]), tm)

    def wait_step(s):
        pltpu.make_async_copy(rec_scr.at[s], xs_ref.at[pl.ds(0, rows), :], sems.at[s]).wait()

    @pl.when(i > 0)
    def _():
        wait_step(1 - slot)

    @pl.when(i == last)
    def _():
        wait_step(slot)


def _dispatch(dest, pad_start, x, n_sorted):
    n = x.shape[0]
    tm = ROW_TM
    return pl.pallas_call(
        _dispatch_kernel,
        grid_spec=pltpu.PrefetchScalarGridSpec(
            num_scalar_prefetch=2,
            grid=(n // tm,),
            in_specs=[pl.BlockSpec((tm, D_MODEL), lambda i, d, p: (i, 0))],
            out_specs=pl.BlockSpec(memory_space=pl.ANY),
            scratch_shapes=[pltpu.VMEM((2, tm * SUBLANES, LANES), F32),
                            pltpu.VMEM((MOE_TM * SUBLANES, LANES), F32),
                            pltpu.SemaphoreType.DMA((2,)), pltpu.SemaphoreType.DMA]),
        out_shape=jax.ShapeDtypeStruct((n_sorted * SUBLANES, LANES), F32),
        compiler_params=_cparams(("arbitrary",)),
        name="dispatch",
    )(dest, pad_start, x)


def _expert_kernel(ea_ref, eb_ref, nt_ref, xs_ref, wr_ref, br_ref,
                   wga_ref, wua_ref, wda_ref, wgb_ref, wub_ref, wdb_ref, g_ref, b_ref, ys_ref):
    p = pl.program_id(0)
    tm = MOE_TM

    @pl.when(p < nt_ref[0])
    def _():
        x = jnp.concatenate([xs_ref[pl.ds(j, tm, stride=SUBLANES), :] for j in range(D_MODEL // LANES)],
                            axis=1)
        xb = x.astype(BF16)

        logits = jnp.dot(xb, wr_ref[...], preferred_element_type=F32) + br_ref[...]
        lane = lax.broadcasted_iota(jnp.int32, (tm, LANES), 1)
        lane_a = N_GROUPS + ea_ref[p]
        lane_b = N_GROUPS + eb_ref[p]
        grp = ea_ref[p] // EXPERTS_PER_GROUP
        is_g = lane < N_GROUPS
        mg = jnp.max(jnp.where(is_g, logits, NEG_INF), axis=-1, keepdims=True)
        eg = jnp.where(is_g, jnp.exp(logits - mg), 0.0)
        g_gate = (jnp.sum(jnp.where(lane == grp, eg, 0.0), axis=-1, keepdims=True)
                  / jnp.sum(eg, axis=-1, keepdims=True))
        l_a = jnp.sum(jnp.where(lane == lane_a, logits, 0.0), axis=-1, keepdims=True)
        l_b = jnp.sum(jnp.where(lane == lane_b, logits, 0.0), axis=-1, keepdims=True)
        mx = jnp.maximum(l_a, l_b)
        p_a = jnp.exp(l_a - mx)
        p_b = jnp.exp(l_b - mx)
        scale = g_gate / (p_a + p_b)
        w_a = p_a * scale
        w_b = p_b * scale

        def expert(wg_ref, wu_ref, wd_ref):
            gate = jnp.dot(xb, wg_ref[...], preferred_element_type=F32)
            up = jnp.dot(xb, wu_ref[...], preferred_element_type=F32)
            hid = (gate * (1.0 / (1.0 + jnp.exp(-gate))) * up).astype(BF16)
            return jnp.dot(hid, wd_ref[...], preferred_element_type=F32)

        y = w_a * expert(wga_ref, wua_ref, wda_ref) + w_b * expert(wgb_ref, wub_ref, wdb_ref)
        out = _layer_norm(DEEPNORM_ALPHA * x + y, g_ref[...], b_ref[...])
        for j in range(D_MODEL // LANES):
            ys_ref[pl.ds(j, tm, stride=SUBLANES), :] = out[:, j * LANES:(j + 1) * LANES]


def _experts(layer, tile_ea, tile_eb, n_tiles, xs, w_router, b_router, w_gate, w_up, w_down, g, b):
    tm = MOE_TM
    rows = tm * SUBLANES
    n_grid = xs.shape[0] // rows
    last = lambda p, nt: jnp.maximum(jnp.minimum(p, nt[0] - 1), 0)
    wspec_a = lambda shape: pl.BlockSpec((None, None) + shape, lambda p, ea, eb, nt: (layer, ea[p], 0, 0))
    wspec_b = lambda shape: pl.BlockSpec((None, None) + shape, lambda p, ea, eb, nt: (layer, eb[p], 0, 0))
    const = lambda shape: pl.BlockSpec(shape, lambda p, ea, eb, nt: (0, 0))
    up_shape = (D_MODEL, D_EXPERT)
    dn_shape = (D_EXPERT, D_MODEL)
    return pl.pallas_call(
        _expert_kernel,
        grid_spec=pltpu.PrefetchScalarGridSpec(
            num_scalar_prefetch=3,
            grid=(n_grid,),
            in_specs=[pl.BlockSpec((rows, LANES), lambda p, ea, eb, nt: (last(p, nt), 0)),
                      const((D_MODEL, LANES)), const((1, LANES)),
                      wspec_a(up_shape), wspec_a(up_shape), wspec_a(dn_shape),
                      wspec_b(up_shape), wspec_b(up_shape), wspec_b(dn_shape),
                      const((1, D_MODEL)), const((1, D_MODEL))],
            out_specs=pl.BlockSpec((rows, LANES), lambda p, ea, eb, nt: (last(p, nt), 0))),
        out_shape=jax.ShapeDtypeStruct(xs.shape, F32),
        compiler_params=_cparams(("arbitrary",)),
        name="experts",
    )(tile_ea, tile_eb, n_tiles, xs, w_router.astype(BF16), b_router,
      w_gate, w_up, w_down, w_gate, w_up, w_down, g.reshape(1, D_MODEL), b.reshape(1, D_MODEL))


def _gather_kernel(dest_ref, ys_ref, out_ref, rec_scr, sems):
    i = pl.program_id(0)
    n_steps = pl.num_programs(0)
    tm = out_ref.shape[0]
    rows = tm * SUBLANES
    slot = i % 2

    def fetch(s):
        _start_rows(lambda r: _tile_copy(ys_ref, dest_ref[s * tm + r], rec_scr.at[s % 2], r, sems.at[s % 2]), tm)

    @pl.when(i == 0)
    def _():
        fetch(i)

    @pl.when(i + 1 < n_steps)
    def _():
        fetch(i + 1)

    pltpu.make_async_copy(ys_ref.at[pl.ds(0, rows), :], rec_scr.at[slot], sems.at[slot]).wait()
    for j in range(D_MODEL // LANES):
        out_ref[:, j * LANES:(j + 1) * LANES] = rec_scr[slot, pl.ds(j, tm, stride=SUBLANES), :]


def _gather_rows(dest, ys, n):
    tm = ROW_TM
    return pl.pallas_call(
        _gather_kernel,
        grid_spec=pltpu.PrefetchScalarGridSpec(
            num_scalar_prefetch=1,
            grid=(n // tm,),
            in_specs=[pl.BlockSpec(memory_space=pl.ANY)],
            out_specs=pl.BlockSpec((tm, D_MODEL), lambda i, d: (i, 0)),
            scratch_shapes=[pltpu.VMEM((2, tm * SUBLANES, LANES), F32), pltpu.SemaphoreType.DMA((2,))]),
        out_shape=jax.ShapeDtypeStruct((n, D_MODEL), F32),
        compiler_params=_cparams(("arbitrary",)),
        name="gather_rows",
    )(dest, ys)


_PAIR_A = np.array([0, 0, 0, 1, 1, 2], np.int32)
_PAIR_B = np.array([1, 2, 3, 2, 3, 3], np.int32)


def _moe_layer(layer, x, w_router, b_router, w_gate, w_up, w_down, g, b):
    n = x.shape[0]
    tm = MOE_TM
    n_sorted = n + N_CLASSES * tm
    info, counts = _router(x, w_router, b_router)
    cls = info[:, 0, :].reshape(n).astype(jnp.int32)
    rank = (info[:, 1, :] * RANK_BASE + info[:, 2, :]).reshape(n).astype(jnp.int32)
    counts = counts[0, :N_CLASSES]
    padded = (counts + tm - 1) // tm * tm
    classes = jnp.arange(N_CLASSES, dtype=jnp.int32)
    ends = jnp.sum(jnp.where(classes[None, :] <= classes[:, None], padded[None, :], 0), axis=1)
    offs = ends - padded
    total = ends[N_CLASSES - 1]
    dest = rank + jnp.sum(jnp.where(cls[:, None] == classes[None, :], offs[None, :], 0), axis=1)
    pad_start = jnp.where(padded > 0, ends - tm, -1).astype(jnp.int32)
    tile_start = jnp.arange(n_sorted // tm, dtype=jnp.int32) * tm
    tile_start = jnp.minimum(tile_start, total - tm)
    tile_cls = jnp.sum((ends[None, :] <= tile_start[:, None]).astype(jnp.int32), axis=1)
    pair = tile_cls % N_PAIRS
    pair_a = jnp.sum(jnp.where(pair[:, None] == np.arange(N_PAIRS)[None, :], _PAIR_A[None, :], 0), axis=1)
    pair_b = jnp.sum(jnp.where(pair[:, None] == np.arange(N_PAIRS)[None, :], _PAIR_B[None, :], 0), axis=1)
    grp = tile_cls // N_PAIRS
    tile_ea = (grp * EXPERTS_PER_GROUP + pair_a).astype(jnp.int32)
    tile_eb = (grp * EXPERTS_PER_GROUP + pair_b).astype(jnp.int32)
    n_tiles = (total // tm).astype(jnp.int32).reshape(1)
    xs = _dispatch(dest.astype(jnp.int32), pad_start, x, n_sorted)
    ys = _experts(layer, tile_ea, tile_eb, n_tiles, xs, w_router, b_router, w_gate, w_up, w_down, g, b)
    return _gather_rows(dest.astype(jnp.int32), ys, n)


_A_ORDER = np.array([0, 4, 1, 5, 2, 6, 3, 7])


def _prep_ab(w_in, w_out):
    qa = w_in[:, :QA_W].reshape(D_MODEL, A_HEADS, HEAD_DIM)[:, _A_ORDER].reshape(D_MODEL, QA_W) * ATTN_SCALE
    kva = w_in[:, QA_W:A_IN]
    qb = w_in[:, A_IN:A_IN + B_W] * ATTN_SCALE
    kvb = w_in[:, A_IN + B_W:]
    w = jnp.concatenate([qa, kva, qb, kvb], axis=1).astype(BF16)
    wo_a = w_out[:QA_W].reshape(A_HEADS, HEAD_DIM, D_MODEL)[_A_ORDER].reshape(QA_W, D_MODEL).astype(BF16)
    wo_b = w_out[QA_W:].astype(BF16)
    return w, wo_a, wo_b


def _prep_c(w_in, w_out):
    w = jnp.concatenate([w_in[:, :C_W] * ATTN_SCALE, w_in[:, C_W:]], axis=1).astype(BF16)
    return w, w_out.astype(BF16)


def _trunk(x, seg_starts, seg_ends, rel_bias, w_in_ab, a_sink, w_out_ab, w_in_c, c_rpb, w_out_c,
           ln1_g, ln1_b, ln2_g, ln2_b, router_g_w, router_g_b, router_e_w, router_e_b,
           w_gate, w_up, w_down):
    bias_a = _bias_a(rel_bias)
    bias_b = _bias_b(rel_bias)
    w_gate = w_gate.astype(BF16)
    w_up = w_up.astype(BF16)
    w_down = w_down.astype(BF16)
    for l in range(DEPTH):
        i = l // 2
        if l % 2 == 0:
            w, wo_a, wo_b = _prep_ab(w_in_ab[i], w_out_ab[i])
            a_qkv, b_qkv = _inproj(x, w, ((0, A_IN), (A_IN, A_IN + B_IN)), (BF16, F32))
            o_a = _attn_a(a_qkv, bias_a, a_sink[i].astype(F32), seg_starts, seg_ends)
            o_b = _attn_b(b_qkv, bias_b, seg_starts, seg_ends)
            x = _outproj_ln([o_a, o_b], [wo_a, wo_b], x, ln1_g[l], ln1_b[l])
        else:
            w, wo = _prep_c(w_in_c[i], w_out_c[i])
            (c_qkv,) = _inproj(x, w, ((0, 3 * C_W),), (BF16,))
            o_c = _attn_c(c_qkv, _bias_c(c_rpb[i]), seg_starts, seg_ends)
            x = _outproj_ln([o_c], [wo], x, ln1_g[l], ln1_b[l])
        pad = LANES - N_GROUPS - N_EXPERTS
        w_router = jnp.pad(jnp.concatenate([router_g_w[l], router_e_w[l]], axis=1), ((0, 0), (0, pad)))
        b_router = jnp.pad(jnp.concatenate([router_g_b[l], router_e_b[l]]), (0, pad)).reshape(1, LANES)
        x = _moe_layer(l, x, w_router, b_router, w_gate, w_up, w_down, ln2_g[l], ln2_b[l])
    return x


def kernel(x_prompt, x_sample, rel_bias, w_in_ab, a_sink, w_out_ab, w_in_c, c_rpb, w_out_c,
           ln1_g, ln1_b, ln2_g, ln2_b, router_g_w, router_g_b, router_e_w, router_e_b,
           w_gate, w_up, w_down):
    seqs = [x_prompt[b] for b in range(x_prompt.shape[0])] + [x_sample[b] for b in range(x_sample.shape[0])]
    lens = [s.shape[0] for s in seqs]
    seg_ends = tuple(int(v) for v in np.cumsum(lens))
    seg_starts = tuple(e - n for e, n in zip(seg_ends, lens))
    for n in lens:
        assert n % ATT_TB == 0 and n // GRID_W >= NA_ROWS
    x = jnp.concatenate(seqs, axis=0)
    y = _trunk(x, seg_starts, seg_ends, rel_bias, w_in_ab, a_sink, w_out_ab, w_in_c, c_rpb, w_out_c,
               ln1_g, ln1_b, ln2_g, ln2_b, router_g_w, router_g_b, router_e_w, router_e_b,
               w_gate, w_up, w_down)
    n_p = x_prompt.shape[0] * x_prompt.shape[1]
    return (y[:n_p].reshape(x_prompt.shape), y[n_p:].reshape(x_sample.shape))
```

```python
import functools
import math

import numpy as np
import jax
import jax.numpy as jnp
from jax import lax
from jax.experimental import pallas as pl
from jax.experimental.pallas import tpu as pltpu

F32 = jnp.float32
BF16 = jnp.bfloat16

D_MODEL = 1024
DEPTH = 4
HEAD_DIM = 64
LANES = 128
A_HEADS = 8
A_KV_HEADS = 2
A_WINDOW = 128
B_HEADS = 8
B_BRANCHES = ((128, 1), (512, 4), (2048, 16))
B_HALF = 64
C_HEADS = 16
GRID_W = 64
NA_ROWS = 8
NA_COLS = 16
REL_BUCKETS = 32
REL_MAX_DIST = 1024
N_GROUPS = 4
EXPERTS_PER_GROUP = 4
N_EXPERTS = 16
D_EXPERT = 512
N_PAIRS = 6
N_CLASSES = N_GROUPS * N_PAIRS
DEEPNORM_ALPHA = (2.0 * DEPTH) ** 0.25
LN_EPS = 1e-5
LOG2E = math.log2(math.e)
ATTN_SCALE = HEAD_DIM ** -0.5 * LOG2E
NEG_INF = -1e30

QA_W = A_HEADS * HEAD_DIM
KVA_W = A_KV_HEADS * HEAD_DIM
A_IN = QA_W + 2 * KVA_W
B_W = B_HEADS * HEAD_DIM
B_IN = 3 * B_W
C_W = C_HEADS * HEAD_DIM

ATT_TB = 1024
MM_TM = 512
MOE_TM = 256
ROW_TM = 512
VMEM_LIMIT = 56 * 1024 * 1024


def _cparams(sem):
    return pltpu.CompilerParams(dimension_semantics=sem, vmem_limit_bytes=VMEM_LIMIT)


def _segment_flags(tok0, size, seg_starts, seg_ends):
    is_first = functools.reduce(jnp.logical_or, [tok0 == s for s in seg_starts])
    is_last = functools.reduce(jnp.logical_or, [tok0 + size == e for e in seg_ends])
    return is_first, is_last


def _t5_bucket_np(rel):
    half_b = REL_BUCKETS // 2
    max_exact = half_b // 2
    n = np.abs(rel)
    large = max_exact + (np.log(np.maximum(n, max_exact).astype(np.float32) / max_exact)
                         / math.log(REL_MAX_DIST / max_exact) * (half_b - max_exact)).astype(np.int32)
    large = np.minimum(large, half_b - 1)
    return np.where(rel > 0, half_b, 0) + np.where(n < max_exact, n, large)


def _banded_bias(table, half, dil):
    rel = np.arange(3 * half)[None, :] - half - np.arange(half)[:, None]
    bucket = jnp.asarray(_t5_bucket_np(rel * dil).astype(np.int32))
    hit = bucket[None] == jnp.arange(REL_BUCKETS, dtype=jnp.int32)[:, None, None]
    bias = jnp.sum(jnp.where(hit[:, None], table.astype(F32)[:, :, None, None], 0.0), axis=0)
    return jnp.where(jnp.asarray(np.abs(rel) <= half)[None], bias * LOG2E, NEG_INF)


def _bias_a(rel_bias):
    w = A_WINDOW
    ch = A_CHUNK
    b = _banded_bias(rel_bias[:, :A_HEADS], w, 1).reshape(A_KV_HEADS, 4, w // ch, ch, 3 * w)
    b = b.transpose(0, 2, 1, 3, 4).reshape(A_KV_HEADS, w // ch, 4 * ch, 3 * w)
    col = np.arange(3 * w)
    first = jnp.where(jnp.asarray(col < w), NEG_INF, b)
    last = jnp.where(jnp.asarray(col >= 2 * w), NEG_INF, b)
    return jnp.stack([b, first, last])


def _bias_b(rel_bias):
    per = [_banded_bias(rel_bias[:, A_HEADS:], B_HALF, d) for _, d in B_BRANCHES]
    b = jnp.stack(per, axis=1)
    return b.reshape(B_HEADS // 2, 2, len(B_BRANCHES), B_HALF, 3 * B_HALF).transpose(0, 2, 1, 3, 4) \
            .reshape(B_HEADS // 2, len(B_BRANCHES), 2 * B_HALF, 3 * B_HALF)


def _bias_c(rpb):
    gw = GRID_W
    n_dr = 2 * NA_ROWS - 1
    side = gw - NA_COLS
    p = jnp.concatenate([jnp.repeat(rpb[..., :1], side, axis=-1), rpb.astype(F32),
                         jnp.repeat(rpb[..., -1:], side + 1, axis=-1)], axis=-1)
    z = jnp.broadcast_to(p[:, :, None, :], (C_HEADS, n_dr, gw, 2 * gw)).reshape(C_HEADS, n_dr, 2 * gw * gw)
    t = z[:, :, gw - 1:gw - 1 + gw * (2 * gw - 1)].reshape(C_HEADS, n_dr, gw, 2 * gw - 1)[..., :gw]
    cq = np.arange(gw)[:, None]
    w = np.arange(gw)[None, :]
    c0 = np.clip(cq - NA_COLS // 2, 0, gw - NA_COLS)
    t = jnp.where(jnp.asarray((w >= c0) & (w < c0 + NA_COLS)), t * LOG2E, NEG_INF)
    bias = jnp.stack([jnp.transpose(t[:, NA_ROWS - 1 - s:2 * NA_ROWS - 1 - s], (0, 2, 1, 3))
                      for s in range(NA_ROWS)], axis=1)
    bias = bias.reshape(C_HEADS // 2, 2, NA_ROWS, GRID_W, NA_ROWS * GRID_W)
    return bias.transpose(0, 2, 1, 3, 4).reshape(C_HEADS // 2, NA_ROWS, 2 * GRID_W, NA_ROWS * GRID_W)


def _segment_blocks(segs, tm):
    return tuple(a.shape[0] // tm for a in segs)


def _segment_specs(seg_blocks, tm, width):
    specs, start = [], 0
    for nb in seg_blocks:
        specs.append(pl.BlockSpec((tm, width), lambda i, *_, s=start, nb=nb: (jnp.clip(i - s, 0, nb - 1), 0)))
        start += nb
    return specs


def _segment_rows(i, refs, seg_blocks):
    x, start = refs[0][...], seg_blocks[0]
    for ref, nb in zip(refs[1:], seg_blocks[1:]):
        x = jnp.where(i >= start, ref[...], x)
        start += nb
    return x


def _inproj_kernel(*refs, splits, seg_blocks):
    n_seg = len(seg_blocks)
    w_ref = refs[n_seg]
    o_refs = refs[n_seg + 1:]
    x = _segment_rows(pl.program_id(0), refs[:n_seg], seg_blocks).astype(BF16)
    for o_ref, (lo, hi) in zip(o_refs, splits):
        o_ref[...] = jnp.dot(x, w_ref[:, lo:hi], preferred_element_type=F32).astype(o_ref.dtype)


def _inproj(xs, w, splits, dtypes):
    seg_blocks = _segment_blocks(xs, MM_TM)
    n = sum(seg_blocks) * MM_TM
    return pl.pallas_call(
        functools.partial(_inproj_kernel, splits=splits, seg_blocks=seg_blocks),
        grid=(n // MM_TM,),
        in_specs=_segment_specs(seg_blocks, MM_TM, D_MODEL) + [pl.BlockSpec(w.shape, lambda i: (0, 0))],
        out_specs=[pl.BlockSpec((MM_TM, hi - lo), lambda i: (i, 0)) for lo, hi in splits],
        out_shape=[jax.ShapeDtypeStruct((n, hi - lo), dt) for (lo, hi), dt in zip(splits, dtypes)],
        compiler_params=_cparams(("parallel",)),
        name="inproj",
    )(*xs, w)


def _layer_norm(z, g, b):
    mu = jnp.mean(z, axis=-1, keepdims=True)
    zc = z - mu
    var = jnp.mean(zc * zc, axis=-1, keepdims=True)
    return zc * lax.rsqrt(var + LN_EPS) * g + b


def _outproj_ln_kernel(*refs, n_parts, seg_blocks):
    n_seg = len(seg_blocks)
    o_refs = refs[:n_parts]
    w_refs = refs[n_parts:2 * n_parts]
    x_refs = refs[2 * n_parts:2 * n_parts + n_seg]
    g_ref, b_ref, out_ref = refs[2 * n_parts + n_seg:]
    h = DEEPNORM_ALPHA * _segment_rows(pl.program_id(0), x_refs, seg_blocks)
    for o_ref, w_ref in zip(o_refs, w_refs):
        h = h + jnp.dot(o_ref[...], w_ref[...], preferred_element_type=F32)
    out_ref[...] = _layer_norm(h, g_ref[...], b_ref[...])


def _outproj_ln(parts, weights, xs, g, b):
    seg_blocks = _segment_blocks(xs, MM_TM)
    n = sum(seg_blocks) * MM_TM
    n_parts = len(parts)
    return pl.pallas_call(
        functools.partial(_outproj_ln_kernel, n_parts=n_parts, seg_blocks=seg_blocks),
        grid=(n // MM_TM,),
        in_specs=([pl.BlockSpec((MM_TM, p.shape[1]), lambda i: (i, 0)) for p in parts]
                  + [pl.BlockSpec(w.shape, lambda i: (0, 0)) for w in weights]
                  + _segment_specs(seg_blocks, MM_TM, D_MODEL)
                  + [pl.BlockSpec((1, D_MODEL), lambda i: (0, 0)),
                     pl.BlockSpec((1, D_MODEL), lambda i: (0, 0))]),
        out_specs=pl.BlockSpec((MM_TM, D_MODEL), lambda i: (i, 0)),
        out_shape=jax.ShapeDtypeStruct((n, D_MODEL), F32),
        compiler_params=_cparams(("parallel",)),
        name="outproj_ln",
    )(*parts, *weights, *xs, g.reshape(1, D_MODEL), b.reshape(1, D_MODEL))


ATT_DEPTH = 3


def _staged(n, weights, values):
    for i in range(min(ATT_DEPTH, n)):
        weights(i)
    for i in range(n):
        if i + ATT_DEPTH < n:
            weights(i + ATT_DEPTH)
        values(i)


def _attn_a_kernel(q_ref, kvm_ref, kvp_ref, kvn_ref, bias_ref, sink_ref, o_ref, kv_scr, p_scr,
                   *, seg_starts, seg_ends):
    w = A_WINDOW
    n_sub = ATT_TB // w
    tok0 = pl.program_id(0) * ATT_TB
    is_first, is_last = _segment_flags(tok0, ATT_TB, seg_starts, seg_ends)
    kv_scr[0:w, :] = kvp_ref[...]
    kv_scr[w:w + ATT_TB, :] = kvm_ref[...]
    kv_scr[w + ATT_TB:, :] = kvn_ref[...]
    low = lax.broadcasted_iota(jnp.int32, (1, LANES), 1) < HEAD_DIM
    ch = A_CHUNK
    per = w // ch

    sinks = [jnp.concatenate([jnp.full((ch, 1), sink_ref[c + 4 * g], F32) for c in range(4)], axis=0)
             for g in range(A_KV_HEADS)]

    def scores(t, g):
        j = t // per
        q = q_ref[t * ch:(t + 1) * ch, :]
        qg = jnp.concatenate([jnp.where(low, q[:, c * LANES:(c + 1) * LANES], 0) if g == 0
                              else jnp.where(low, 0, q[:, c * LANES:(c + 1) * LANES]) for c in range(4)], axis=0)
        k2 = kv_scr[j * w:(j + 3) * w, :LANES]
        return lax.dot_general(qg, k2, (((1,), (1,)), ((), ())), preferred_element_type=F32)

    tiles = [(t, g) for t in range(ATT_TB // ch) for g in range(A_KV_HEADS)]
    rows = 4 * ch
    rdens = {}
    outs = {}

    def weights(i):
        t, g = tiles[i]
        j = t // per
        if j == 0:
            variant = jnp.where(is_first, 1, 0)
        elif j == n_sub - 1:
            variant = jnp.where(is_last, 2, 0)
        else:
            variant = 0
        s = scores(t, g) + bias_ref[variant, g, t % per]
        m = jnp.maximum(jnp.max(s, axis=-1, keepdims=True), sinks[g])
        e = jnp.exp2(s - m)
        rdens[i] = 1.0 / (jnp.sum(e, axis=-1, keepdims=True) + jnp.exp2(sinks[g] - m))
        p_scr[i * rows:(i + 1) * rows, :] = e.astype(BF16)

    def values(i):
        t, g = tiles[i]
        j = t // per
        v2 = kv_scr[j * w:(j + 3) * w, LANES:]
        outs[g] = jnp.dot(p_scr[i * rows:(i + 1) * rows, :], v2, preferred_element_type=F32) * rdens[i]
        if g == A_KV_HEADS - 1:
            for c in range(4):
                oc = jnp.where(low, outs[0][c * ch:(c + 1) * ch], outs[1][c * ch:(c + 1) * ch])
                o_ref[t * ch:(t + 1) * ch, c * LANES:(c + 1) * LANES] = oc.astype(o_ref.dtype)

    _staged(len(tiles), weights, values)


A_CHUNK = 64


def _attn_a(a_qkv, bias, sink, seg_starts, seg_ends):
    n = a_qkv.shape[0]
    w = A_WINDOW
    sub = ATT_TB // w
    nhb = n // w
    kv_col = QA_W // (2 * LANES)
    return pl.pallas_call(
        functools.partial(_attn_a_kernel, seg_starts=seg_starts, seg_ends=seg_ends),
        grid=(n // ATT_TB,),
        in_specs=[pl.BlockSpec((ATT_TB, QA_W), lambda i: (i, 0)),
                  pl.BlockSpec((ATT_TB, 2 * LANES), lambda i: (i, kv_col)),
                  pl.BlockSpec((w, 2 * LANES), lambda i: (jnp.maximum(i * sub - 1, 0), kv_col)),
                  pl.BlockSpec((w, 2 * LANES), lambda i: (jnp.minimum((i + 1) * sub, nhb - 1), kv_col)),
                  pl.BlockSpec(bias.shape, lambda i: (0,) * bias.ndim),
                  pl.BlockSpec(memory_space=pltpu.SMEM)],
        out_specs=pl.BlockSpec((ATT_TB, QA_W), lambda i: (i, 0)),
        out_shape=jax.ShapeDtypeStruct((n, QA_W), BF16),
        scratch_shapes=[pltpu.VMEM((ATT_TB + 2 * w, 2 * LANES), BF16),
                        pltpu.VMEM((A_HEADS * ATT_TB, 3 * w), BF16)],
        compiler_params=_cparams(("parallel",)),
        name="attn_a",
    )(a_qkv, a_qkv, a_qkv, a_qkv, bias, sink)


def _attn_b_kernel(q_ref, kp_ref, km_ref, kn_ref, vp_ref, vm_ref, vn_ref, bias_ref, o_ref,
                   k_scr, v_scr, o_scr, m_scr, l_scr, p_scr, *, seg_starts, seg_ends):
    tb = ATT_TB
    h = B_HALF
    tok0 = pl.program_id(1) * tb
    is_first, is_last = _segment_flags(tok0, tb, seg_starts, seg_ends)
    k_scr[0:tb, :] = kp_ref[...]
    k_scr[tb:2 * tb, :] = km_ref[...]
    k_scr[2 * tb:, :] = kn_ref[...]
    v_scr[0:tb, :] = vp_ref[...]
    v_scr[tb:2 * tb, :] = vm_ref[...]
    v_scr[2 * tb:, :] = vn_ref[...]
    lane = lax.broadcasted_iota(jnp.int32, (1, LANES), 1)
    low = lane < HEAD_DIM
    col = lax.broadcasted_iota(jnp.int32, (1, 3 * h), 1)

    pen_first = jnp.where(jnp.logical_and(col < h, is_first), NEG_INF, 0.0)
    pen_last = jnp.where(jnp.logical_and(col >= 2 * h, is_last), NEG_INF, 0.0)

    def slices(d, r, b):
        row0 = r + h * d * b
        if d == 1:
            return pl.ds(row0, h), pl.ds(tb + row0 - h, 3 * h)
        return pl.ds(row0, h, stride=d), pl.ds(tb + row0 - h * d, 3 * h, stride=d)

    def scores(br, d, r, b):
        qs, ks = slices(d, r, b)
        q = q_ref[qs, :].astype(BF16)
        k = k_scr[ks, :].astype(BF16)
        qq = jnp.concatenate([jnp.where(low, q, 0), jnp.where(low, 0, q)], axis=0)
        return lax.dot_general(qq, k, (((1,), (1,)), ((), ())), preferred_element_type=F32)

    tiles = [(br, d, r, b) for br, (_, d) in enumerate(B_BRANCHES)
             for r in range(d) for b in range(tb // (h * d))]

    def weights(i):
        br, d, r, b = tiles[i]
        qs, _ = slices(d, r, b)
        s = scores(br, d, r, b) + bias_ref[br]
        if b == 0:
            s = s + pen_first
        if b == tb // (h * d) - 1:
            s = s + pen_last
        m = jnp.max(s, axis=-1, keepdims=True)
        e = jnp.exp2(s - m)
        l = jnp.sum(e, axis=-1, keepdims=True)
        p_scr[i * 2 * h:(i + 1) * 2 * h, :] = e.astype(BF16)
        m_scr[br, qs, :] = jnp.where(low, m[:h], m[h:])
        l_scr[br, qs, :] = jnp.where(low, l[:h], l[h:])

    def values(i):
        br, d, r, b = tiles[i]
        qs, ks = slices(d, r, b)
        v = v_scr[ks, :].astype(BF16)
        pv = jnp.dot(p_scr[i * 2 * h:(i + 1) * 2 * h, :], v, preferred_element_type=F32)
        o_scr[br, qs, :] = jnp.where(low, pv[:h], pv[h:])

    _staged(len(tiles), weights, values)

    m_all = jnp.maximum(jnp.maximum(m_scr[0], m_scr[1]), m_scr[2])
    num = jnp.zeros((tb, LANES), F32)
    den = jnp.zeros((tb, LANES), F32)
    for br in range(len(B_BRANCHES)):
        a = jnp.exp2(m_scr[br] - m_all)
        num = num + a * o_scr[br]
        den = den + a * l_scr[br]
    o_ref[...] = (num / den).astype(o_ref.dtype)


def _attn_b(b_qkv, bias, seg_starts, seg_ends):
    n = b_qkv.shape[0]
    tb = ATT_TB
    nblk = n // tb
    npair = B_HEADS // 2
    prev = lambda i: jnp.maximum(i - 1, 0)
    nxt = lambda i: jnp.minimum(i + 1, nblk - 1)
    blk = lambda rowf, off: pl.BlockSpec((tb, LANES), lambda c, i: (rowf(i), off + c))
    same = lambda i: i
    stat = pltpu.VMEM((len(B_BRANCHES), tb, LANES), F32)
    return pl.pallas_call(
        functools.partial(_attn_b_kernel, seg_starts=seg_starts, seg_ends=seg_ends),
        grid=(npair, nblk),
        in_specs=[blk(same, 0),
                  blk(prev, npair), blk(same, npair), blk(nxt, npair),
                  blk(prev, 2 * npair), blk(same, 2 * npair), blk(nxt, 2 * npair),
                  pl.BlockSpec((None,) + bias.shape[1:], lambda c, i: (c, 0, 0, 0))],
        out_specs=pl.BlockSpec((tb, LANES), lambda c, i: (i, c)),
        out_shape=jax.ShapeDtypeStruct((n, B_W), BF16),
        scratch_shapes=[pltpu.VMEM((3 * tb, LANES), F32), pltpu.VMEM((3 * tb, LANES), F32),
                        stat, stat, stat,
                        pltpu.VMEM((len(B_BRANCHES) * 2 * tb, 3 * B_HALF), BF16)],
        compiler_params=_cparams(("parallel", "parallel")),
        name="attn_b",
    )(b_qkv, b_qkv, b_qkv, b_qkv, b_qkv, b_qkv, b_qkv, bias)


C_HALO = (NA_ROWS // 2) * GRID_W

def _attn_c_kernel(q_ref, kp_ref, km_ref, kn_ref, vp_ref, vm_ref, vn_ref, bias_ref, o_ref,
                   k_scr, v_scr, p_scr, *, seg_starts, seg_ends):
    tb = ATT_TB
    gw = GRID_W
    nkeys = NA_ROWS * gw
    tok0 = pl.program_id(1) * tb
    k_scr[0:C_HALO, :] = kp_ref[...]
    k_scr[C_HALO:C_HALO + tb, :] = km_ref[...]
    k_scr[C_HALO + tb:, :] = kn_ref[...]
    v_scr[0:C_HALO, :] = vp_ref[...]
    v_scr[C_HALO:C_HALO + tb, :] = vm_ref[...]
    v_scr[C_HALO + tb:, :] = vn_ref[...]
    seg_row0 = jnp.int32(0)
    seg_rows = jnp.int32(0)
    for s, e in zip(seg_starts, seg_ends):
        inside = jnp.logical_and(tok0 >= s, tok0 < e)
        seg_row0 = jnp.where(inside, s // gw, seg_row0)
        seg_rows = jnp.where(inside, (e - s) // gw, seg_rows)
    lane = lax.broadcasted_iota(jnp.int32, (1, LANES), 1)
    low = lane < HEAD_DIM

    def window(rr):
        rs = tok0 // gw + rr - seg_row0
        start = jnp.clip(rs - NA_ROWS // 2, 0, seg_rows - NA_ROWS)
        shift = rs - start
        return shift, pl.ds(pl.multiple_of((rr + NA_ROWS // 2 - shift) * gw, gw), nkeys)

    def scores(rr):
        _, ks = window(rr)
        q = q_ref[rr * gw:(rr + 1) * gw, :]
        qq = jnp.concatenate([jnp.where(low, q, 0), jnp.where(low, 0, q)], axis=0)
        return lax.dot_general(qq, k_scr[ks, :], (((1,), (1,)), ((), ())), preferred_element_type=F32)

    n_rows = tb // gw
    rdens = {}

    def weights(rr):
        shift, _ = window(rr)
        s = scores(rr) + bias_ref[shift]
        m = jnp.max(s, axis=-1, keepdims=True)
        e = jnp.exp2(s - m)
        rdens[rr] = 1.0 / jnp.sum(e, axis=-1, keepdims=True)
        p_scr[rr * 2 * gw:(rr + 1) * 2 * gw, :] = e.astype(BF16)

    def values(rr):
        _, ks = window(rr)
        pv = jnp.dot(p_scr[rr * 2 * gw:(rr + 1) * 2 * gw, :], v_scr[ks, :], preferred_element_type=F32) * rdens[rr]
        o_ref[rr * gw:(rr + 1) * gw, :] = jnp.where(low, pv[:gw], pv[gw:]).astype(o_ref.dtype)

    _staged(n_rows, weights, values)


def _attn_c(c_qkv, bias, seg_starts, seg_ends):
    n = c_qkv.shape[0]
    tb = ATT_TB
    npair = C_HEADS // 2
    sub = tb // C_HALO
    nhb = n // C_HALO
    main = lambda off: pl.BlockSpec((tb, LANES), lambda c, i: (i, off + c))
    prev = lambda off: pl.BlockSpec((C_HALO, LANES), lambda c, i: (jnp.maximum(i * sub - 1, 0), off + c))
    nxt = lambda off: pl.BlockSpec((C_HALO, LANES),
                                   lambda c, i: (jnp.minimum((i + 1) * sub, nhb - 1), off + c))
    return pl.pallas_call(
        functools.partial(_attn_c_kernel, seg_starts=seg_starts, seg_ends=seg_ends),
        grid=(npair, n // tb),
        in_specs=[main(0),
                  prev(npair), main(npair), nxt(npair),
                  prev(2 * npair), main(2 * npair), nxt(2 * npair),
                  pl.BlockSpec((None,) + bias.shape[1:], lambda c, i: (c, 0, 0, 0))],
        out_specs=pl.BlockSpec((tb, LANES), lambda c, i: (i, c)),
        out_shape=jax.ShapeDtypeStruct((n, C_W), BF16),
        scratch_shapes=[pltpu.VMEM((tb + 2 * C_HALO, LANES), BF16),
                        pltpu.VMEM((tb + 2 * C_HALO, LANES), BF16),
                        pltpu.VMEM((2 * tb, NA_ROWS * GRID_W), BF16)],
        compiler_params=_cparams(("parallel", "parallel")),
        name="attn_c",
    )(c_qkv, c_qkv, c_qkv, c_qkv, c_qkv, c_qkv, c_qkv, bias)


def _router_kernel(x_ref, wh_ref, wl_ref, b_ref, info_ref, cnt_ref, run_scr):
    i = pl.program_id(0)
    tm = x_ref.shape[0]

    @pl.when(i == 0)
    def _():
        run_scr[...] = jnp.zeros_like(run_scr)

    x = x_ref[...]
    xh = x.astype(BF16)
    xl = (x - xh.astype(F32)).astype(BF16)
    logits = (jnp.dot(xh, wh_ref[...], preferred_element_type=F32)
              + jnp.dot(xl, wh_ref[...], preferred_element_type=F32)
              + jnp.dot(xh, wl_ref[...], preferred_element_type=F32)) + b_ref[...]
    lane = lax.broadcasted_iota(jnp.int32, (tm, LANES), 1)
    big = jnp.int32(LANES)
    is_g = lane < N_GROUPS
    lg = jnp.where(is_g, logits, NEG_INF)
    mg = jnp.max(lg, axis=-1, keepdims=True)
    g_sel = jnp.min(jnp.where(jnp.logical_and(is_g, lg == mg), lane, big), axis=-1, keepdims=True)
    e_lo = N_GROUPS + g_sel * EXPERTS_PER_GROUP
    in_grp = jnp.logical_and(lane >= e_lo, lane < e_lo + EXPERTS_PER_GROUP)
    le = jnp.where(in_grp, logits, NEG_INF)
    v1 = jnp.max(le, axis=-1, keepdims=True)
    i1 = jnp.min(jnp.where(jnp.logical_and(in_grp, le == v1), lane, big), axis=-1, keepdims=True)
    rest = jnp.logical_and(in_grp, lane != i1)
    le2 = jnp.where(rest, logits, NEG_INF)
    v2 = jnp.max(le2, axis=-1, keepdims=True)
    i2 = jnp.min(jnp.where(jnp.logical_and(rest, le2 == v2), lane, big), axis=-1, keepdims=True)
    a = jnp.minimum(i1, i2) - e_lo
    b = jnp.maximum(i1, i2) - e_lo
    pair = a * 3 - jnp.where(a == 2, 1, 0) + (b - a - 1)
    cls = g_sel * N_PAIRS + pair
    onehot = (lane == cls)
    ri = lax.broadcasted_iota(jnp.int32, (tm, tm), 0)
    ci = lax.broadcasted_iota(jnp.int32, (tm, tm), 1)
    tril = (ci < ri).astype(BF16)
    before = jnp.dot(tril, onehot.astype(BF16), preferred_element_type=F32) + run_scr[...]
    rank = jnp.sum(jnp.where(onehot, before, 0.0), axis=-1, keepdims=True)
    run_scr[...] = run_scr[...] + jnp.sum(onehot.astype(F32), axis=0, keepdims=True)
    rank_hi = jnp.floor(rank * (1.0 / RANK_BASE))
    rank_lo = rank - rank_hi * RANK_BASE
    cols = jnp.where(lane == 0, cls.astype(F32), jnp.where(lane == 1, rank_hi, jnp.where(lane == 2, rank_lo, 0.0)))
    pick = (lax.broadcasted_iota(jnp.int32, (SUBLANES, LANES), 0)
            == lax.broadcasted_iota(jnp.int32, (SUBLANES, LANES), 1)).astype(BF16)
    info_ref[...] = lax.dot_general(pick, cols.astype(BF16), (((1,), (1,)), ((), ())),
                                    preferred_element_type=F32)
    cnt_ref[...] = run_scr[...].astype(jnp.int32)


RANK_BASE = 256
SUBLANES = 8


def _router(x, w, b):
    n = x.shape[0]
    tm = MM_TM
    wh = w.astype(BF16)
    wl = (w - wh.astype(F32)).astype(BF16)
    return pl.pallas_call(
        _router_kernel,
        grid=(n // tm,),
        in_specs=[pl.BlockSpec((tm, D_MODEL), lambda i: (i, 0)),
                  pl.BlockSpec((D_MODEL, LANES), lambda i: (0, 0)),
                  pl.BlockSpec((D_MODEL, LANES), lambda i: (0, 0)),
                  pl.BlockSpec((1, LANES), lambda i: (0, 0))],
        out_specs=[pl.BlockSpec((None, SUBLANES, tm), lambda i: (i, 0, 0)),
                   pl.BlockSpec((1, LANES), lambda i: (0, 0))],
        out_shape=[jax.ShapeDtypeStruct((n // tm, SUBLANES, tm), F32),
                   jax.ShapeDtypeStruct((1, LANES), jnp.int32)],
        scratch_shapes=[pltpu.VMEM((1, LANES), F32)],
        compiler_params=_cparams(("arbitrary",)),
        name="router",
    )(x, wh, wl, b)


def _tile_copy(src, src_tok, dst, dst_tok, sem):
    return pltpu.make_async_copy(src.at[pl.ds(pl.multiple_of(src_tok * SUBLANES, SUBLANES), SUBLANES), :],
                                 dst.at[pl.ds(pl.multiple_of(dst_tok * SUBLANES, SUBLANES), SUBLANES), :], sem)


ROW_UNROLL = 8


def _start_rows(copy, n):
    def body(g, carry):
        for u in range(ROW_UNROLL):
            copy(g * ROW_UNROLL + u).start(priority=u % 2)
        return carry

    lax.fori_loop(0, n // ROW_UNROLL, body, 0)


def _dispatch_kernel(dest_ref, pad_ref, x_ref, xs_ref, rec_scr, zero_scr, sems, zsem):
    i = pl.program_id(0)
    last = pl.num_programs(0) - 1
    tm = x_ref.shape[0]
    rows = tm * SUBLANES
    slot = i % 2
    tile_rows = MOE_TM * SUBLANES

    def zero_copy(c):
        start = pl.multiple_of(pad_ref[c] * SUBLANES, tile_rows)
        return pltpu.make_async_copy(zero_scr, xs_ref.at[pl.ds(start, tile_rows), :], zsem)

    @pl.when(i == 0)
    def _():
        zero_scr[...] = jnp.zeros_like(zero_scr)
        for c in range(N_CLASSES):
            @pl.when(pad_ref[c] >= 0)
            def _():
                zero_copy(c).start()
        for c in range(N_CLASSES):
            @pl.when(pad_ref[c] >= 0)
            def _():
                zero_copy(c).wait()

    for j in range(D_MODEL // LANES):
        rec_scr[slot, pl.ds(j, tm, stride=SUBLANES), :] = x_ref[:, j * LANES:(j + 1) * LANES]

    _start_rows(lambda r: _tile_copy(rec_scr.at[slot], r, xs_ref, dest_ref[i * tm + r], sems.at[slot]), tm)

    def wait_step(s):
        pltpu.make_async_copy(rec_scr.at[s], xs_ref.at[pl.ds(0, rows), :], sems.at[s]).wait()

    @pl.when(i > 0)
    def _():
        wait_step(1 - slot)

    @pl.when(i == last)
    def _():
        wait_step(slot)


def _dispatch(dest, pad_start, x, n_sorted):
    n = x.shape[0]
    tm = ROW_TM
    return pl.pallas_call(
        _dispatch_kernel,
        grid_spec=pltpu.PrefetchScalarGridSpec(
            num_scalar_prefetch=2,
            grid=(n // tm,),
            in_specs=[pl.BlockSpec((tm, D_MODEL), lambda i, d, p: (i, 0))],
            out_specs=pl.BlockSpec(memory_space=pl.ANY),
            scratch_shapes=[pltpu.VMEM((2, tm * SUBLANES, LANES), F32),
                            pltpu.VMEM((MOE_TM * SUBLANES, LANES), F32),
                            pltpu.SemaphoreType.DMA((2,)), pltpu.SemaphoreType.DMA]),
        out_shape=jax.ShapeDtypeStruct((n_sorted * SUBLANES, LANES), F32),
        compiler_params=_cparams(("arbitrary",)),
        name="dispatch",
    )(dest, pad_start, x)


def _expert_kernel(ea_ref, eb_ref, nt_ref, xs_ref, wr_ref, br_ref,
                   wga_ref, wua_ref, wda_ref, wgb_ref, wub_ref, wdb_ref, g_ref, b_ref, ys_ref, hid_scr):
    p = pl.program_id(0)
    tm = MOE_TM

    @pl.when(p < nt_ref[0])
    def _():
        x = jnp.concatenate([xs_ref[pl.ds(j, tm, stride=SUBLANES), :] for j in range(D_MODEL // LANES)],
                            axis=1)
        xb = x.astype(BF16)

        logits = jnp.dot(xb, wr_ref[...], preferred_element_type=F32) + br_ref[...]
        lane = lax.broadcasted_iota(jnp.int32, (tm, LANES), 1)
        lane_a = N_GROUPS + ea_ref[p]
        lane_b = N_GROUPS + eb_ref[p]
        grp = ea_ref[p] // EXPERTS_PER_GROUP
        is_g = lane < N_GROUPS
        mg = jnp.max(jnp.where(is_g, logits, NEG_INF), axis=-1, keepdims=True)
        eg = jnp.where(is_g, jnp.exp(logits - mg), 0.0)
        g_gate = (jnp.sum(jnp.where(lane == grp, eg, 0.0), axis=-1, keepdims=True)
                  / jnp.sum(eg, axis=-1, keepdims=True))
        l_a = jnp.sum(jnp.where(lane == lane_a, logits, 0.0), axis=-1, keepdims=True)
        l_b = jnp.sum(jnp.where(lane == lane_b, logits, 0.0), axis=-1, keepdims=True)
        mx = jnp.maximum(l_a, l_b)
        p_a = jnp.exp(l_a - mx)
        p_b = jnp.exp(l_b - mx)
        scale = g_gate / (p_a + p_b)
        w_a = p_a * scale
        w_b = p_b * scale

        for e, (wg_ref, wu_ref) in enumerate(((wga_ref, wua_ref), (wgb_ref, wub_ref))):
            gate = jnp.dot(xb, wg_ref[...], preferred_element_type=F32)
            up = jnp.dot(xb, wu_ref[...], preferred_element_type=F32)
            hid_scr[e] = (gate * (1.0 / (1.0 + jnp.exp(-gate))) * up).astype(BF16)
        y = (w_a * jnp.dot(hid_scr[0], wda_ref[...], preferred_element_type=F32)
             + w_b * jnp.dot(hid_scr[1], wdb_ref[...], preferred_element_type=F32))
        out = _layer_norm(DEEPNORM_ALPHA * x + y, g_ref[...], b_ref[...])
        for j in range(D_MODEL // LANES):
            ys_ref[pl.ds(j, tm, stride=SUBLANES), :] = out[:, j * LANES:(j + 1) * LANES]


def _experts(layer, tile_ea, tile_eb, n_tiles, xs, w_router, b_router, w_gate, w_up, w_down, g, b):
    tm = MOE_TM
    rows = tm * SUBLANES
    n_grid = xs.shape[0] // rows
    last = lambda p, nt: jnp.maximum(jnp.minimum(p, nt[0] - 1), 0)
    wspec_a = lambda shape: pl.BlockSpec((None, None) + shape, lambda p, ea, eb, nt: (layer, ea[p], 0, 0))
    wspec_b = lambda shape: pl.BlockSpec((None, None) + shape, lambda p, ea, eb, nt: (layer, eb[p], 0, 0))
    const = lambda shape: pl.BlockSpec(shape, lambda p, ea, eb, nt: (0, 0))
    up_shape = (D_MODEL, D_EXPERT)
    dn_shape = (D_EXPERT, D_MODEL)
    return pl.pallas_call(
        _expert_kernel,
        grid_spec=pltpu.PrefetchScalarGridSpec(
            num_scalar_prefetch=3,
            grid=(n_grid,),
            in_specs=[pl.BlockSpec((rows, LANES), lambda p, ea, eb, nt: (last(p, nt), 0)),
                      const((D_MODEL, LANES)), const((1, LANES)),
                      wspec_a(up_shape), wspec_a(up_shape), wspec_a(dn_shape),
                      wspec_b(up_shape), wspec_b(up_shape), wspec_b(dn_shape),
                      const((1, D_MODEL)), const((1, D_MODEL))],
            out_specs=pl.BlockSpec((rows, LANES), lambda p, ea, eb, nt: (last(p, nt), 0)),
            scratch_shapes=[pltpu.VMEM((2, tm, D_EXPERT), BF16)]),
        out_shape=jax.ShapeDtypeStruct(xs.shape, F32),
        compiler_params=_cparams(("arbitrary",)),
        name="experts",
    )(tile_ea, tile_eb, n_tiles, xs, w_router.astype(BF16), b_router,
      w_gate, w_up, w_down, w_gate, w_up, w_down, g.reshape(1, D_MODEL), b.reshape(1, D_MODEL))


def _gather_kernel(dest_ref, ys_ref, *refs, seg_blocks):
    out_refs = refs[:len(seg_blocks)]
    rec_scr, sems = refs[len(seg_blocks):]
    i = pl.program_id(0)
    n_steps = pl.num_programs(0)
    tm = out_refs[0].shape[0]
    rows = tm * SUBLANES
    slot = i % 2

    def fetch(s):
        _start_rows(lambda r: _tile_copy(ys_ref, dest_ref[s * tm + r], rec_scr.at[s % 2], r, sems.at[s % 2]), tm)

    @pl.when(i == 0)
    def _():
        fetch(i)

    @pl.when(i + 1 < n_steps)
    def _():
        fetch(i + 1)

    pltpu.make_async_copy(ys_ref.at[pl.ds(0, rows), :], rec_scr.at[slot], sems.at[slot]).wait()

    start = 0
    for out_ref, nb in zip(out_refs, seg_blocks):
        @pl.when(jnp.logical_and(i >= start, i < start + nb))
        def _(out_ref=out_ref):
            for j in range(D_MODEL // LANES):
                out_ref[:, j * LANES:(j + 1) * LANES] = rec_scr[slot, pl.ds(j, tm, stride=SUBLANES), :]
        start += nb


def _gather_rows(dest, ys, seg_rows):
    tm = ROW_TM
    seg_blocks = tuple(r // tm for r in seg_rows)
    return pl.pallas_call(
        functools.partial(_gather_kernel, seg_blocks=seg_blocks),
        grid_spec=pltpu.PrefetchScalarGridSpec(
            num_scalar_prefetch=1,
            grid=(sum(seg_blocks),),
            in_specs=[pl.BlockSpec(memory_space=pl.ANY)],
            out_specs=_segment_specs(seg_blocks, tm, D_MODEL),
            scratch_shapes=[pltpu.VMEM((2, tm * SUBLANES, LANES), F32), pltpu.SemaphoreType.DMA((2,))]),
        out_shape=[jax.ShapeDtypeStruct((r, D_MODEL), F32) for r in seg_rows],
        compiler_params=_cparams(("arbitrary",)),
        name="gather_rows",
    )(dest, ys)


_PAIR_A = np.array([0, 0, 0, 1, 1, 2], np.int32)
_PAIR_B = np.array([1, 2, 3, 2, 3, 3], np.int32)


def _moe_layer(layer, x, w_router, b_router, w_gate, w_up, w_down, g, b, out_rows):
    n = x.shape[0]
    tm = MOE_TM
    n_sorted = n + N_CLASSES * tm
    info, counts = _router(x, w_router, b_router)
    cls = info[:, 0, :].reshape(n).astype(jnp.int32)
    rank = (info[:, 1, :] * RANK_BASE + info[:, 2, :]).reshape(n).astype(jnp.int32)
    counts = counts[0, :N_CLASSES]
    padded = (counts + tm - 1) // tm * tm
    classes = jnp.arange(N_CLASSES, dtype=jnp.int32)
    ends = jnp.sum(jnp.where(classes[None, :] <= classes[:, None], padded[None, :], 0), axis=1)
    offs = ends - padded
    total = ends[N_CLASSES - 1]
    dest = rank + jnp.sum(jnp.where(cls[:, None] == classes[None, :], offs[None, :], 0), axis=1)
    pad_start = jnp.where(padded > 0, ends - tm, -1).astype(jnp.int32)
    tile_start = jnp.arange(n_sorted // tm, dtype=jnp.int32) * tm
    tile_start = jnp.minimum(tile_start, total - tm)
    tile_cls = jnp.sum((ends[None, :] <= tile_start[:, None]).astype(jnp.int32), axis=1)
    pair = tile_cls % N_PAIRS
    pair_a = jnp.sum(jnp.where(pair[:, None] == np.arange(N_PAIRS)[None, :], _PAIR_A[None, :], 0), axis=1)
    pair_b = jnp.sum(jnp.where(pair[:, None] == np.arange(N_PAIRS)[None, :], _PAIR_B[None, :], 0), axis=1)
    grp = tile_cls // N_PAIRS
    tile_ea = (grp * EXPERTS_PER_GROUP + pair_a).astype(jnp.int32)
    tile_eb = (grp * EXPERTS_PER_GROUP + pair_b).astype(jnp.int32)
    n_tiles = (total // tm).astype(jnp.int32).reshape(1)
    xs = _dispatch(dest.astype(jnp.int32), pad_start, x, n_sorted)
    ys = _experts(layer, tile_ea, tile_eb, n_tiles, xs, w_router, b_router, w_gate, w_up, w_down, g, b)
    return _gather_rows(dest.astype(jnp.int32), ys, out_rows)


_A_ORDER = np.array([0, 4, 1, 5, 2, 6, 3, 7])


def _prep_ab(w_in, w_out):
    qa = w_in[:, :QA_W].reshape(D_MODEL, A_HEADS, HEAD_DIM)[:, _A_ORDER].reshape(D_MODEL, QA_W) * ATTN_SCALE
    kva = w_in[:, QA_W:A_IN]
    qb = w_in[:, A_IN:A_IN + B_W] * ATTN_SCALE
    kvb = w_in[:, A_IN + B_W:]
    w = jnp.concatenate([qa, kva, qb, kvb], axis=1).astype(BF16)
    wo_a = w_out[:QA_W].reshape(A_HEADS, HEAD_DIM, D_MODEL)[_A_ORDER].reshape(QA_W, D_MODEL).astype(BF16)
    wo_b = w_out[QA_W:].astype(BF16)
    return w, wo_a, wo_b


def _prep_c(w_in, w_out):
    w = jnp.concatenate([w_in[:, :C_W] * ATTN_SCALE, w_in[:, C_W:]], axis=1).astype(BF16)
    return w, w_out.astype(BF16)


def _trunk(xs, seg_starts, seg_ends, rel_bias, w_in_ab, a_sink, w_out_ab, w_in_c, c_rpb, w_out_c,
           ln1_g, ln1_b, ln2_g, ln2_b, router_g_w, router_g_b, router_e_w, router_e_b,
           w_gate, w_up, w_down):
    io_rows = tuple(a.shape[0] for a in xs)
    n = sum(io_rows)
    bias_a = _bias_a(rel_bias)
    bias_b = _bias_b(rel_bias)
    w_gate = w_gate.astype(BF16)
    w_up = w_up.astype(BF16)
    w_down = w_down.astype(BF16)
    for l in range(DEPTH):
        i = l // 2
        if l % 2 == 0:
            w, wo_a, wo_b = _prep_ab(w_in_ab[i], w_out_ab[i])
            a_qkv, b_qkv = _inproj(xs, w, ((0, A_IN), (A_IN, A_IN + B_IN)), (BF16, F32))
            o_a = _attn_a(a_qkv, bias_a, a_sink[i].astype(F32) * LOG2E, seg_starts, seg_ends)
            o_b = _attn_b(b_qkv, bias_b, seg_starts, seg_ends)
            x = _outproj_ln([o_a, o_b], [wo_a, wo_b], xs, ln1_g[l], ln1_b[l])
        else:
            w, wo = _prep_c(w_in_c[i], w_out_c[i])
            (c_qkv,) = _inproj(xs, w, ((0, 3 * C_W),), (BF16,))
            o_c = _attn_c(c_qkv, _bias_c(c_rpb[i]), seg_starts, seg_ends)
            x = _outproj_ln([o_c], [wo], xs, ln1_g[l], ln1_b[l])
        pad = LANES - N_GROUPS - N_EXPERTS
        w_router = jnp.pad(jnp.concatenate([router_g_w[l], router_e_w[l]], axis=1), ((0, 0), (0, pad)))
        b_router = jnp.pad(jnp.concatenate([router_g_b[l], router_e_b[l]]), (0, pad)).reshape(1, LANES)
        xs = _moe_layer(l, x, w_router, b_router, w_gate, w_up, w_down, ln2_g[l], ln2_b[l],
                        io_rows if l == DEPTH - 1 else (n,))
    return xs


def kernel(x_prompt, x_sample, rel_bias, w_in_ab, a_sink, w_out_ab, w_in_c, c_rpb, w_out_c,
           ln1_g, ln1_b, ln2_g, ln2_b, router_g_w, router_g_b, router_e_w, router_e_b,
           w_gate, w_up, w_down):
    lens = [x_prompt.shape[1]] * x_prompt.shape[0] + [x_sample.shape[1]] * x_sample.shape[0]
    seg_ends = tuple(int(v) for v in np.cumsum(lens))
    seg_starts = tuple(e - n for e, n in zip(seg_ends, lens))
    for n in lens:
        assert n % ATT_TB == 0 and n // GRID_W >= NA_ROWS
    xs = [x_prompt.reshape(-1, D_MODEL), x_sample.reshape(-1, D_MODEL)]
    y_p, y_s = _trunk(xs, seg_starts, seg_ends, rel_bias, w_in_ab, a_sink, w_out_ab, w_in_c, c_rpb, w_out_c,
                      ln1_g, ln1_b, ln2_g, ln2_b, router_g_w, router_g_b, router_e_w, router_e_b,
                      w_gate, w_up, w_down)
    return (y_p.reshape(x_prompt.shape), y_s.reshape(x_sample.shape))
```

```python
import functools
import math

import numpy as np
import jax
import jax.numpy as jnp
from jax import lax
from jax.experimental import pallas as pl
from jax.experimental.pallas import tpu as pltpu

F32 = jnp.float32
BF16 = jnp.bfloat16

D_MODEL = 1024
DEPTH = 4
HEAD_DIM = 64
LANES = 128
A_HEADS = 8
A_KV_HEADS = 2
A_WINDOW = 128
B_HEADS = 8
B_BRANCHES = ((128, 1), (512, 4), (2048, 16))
B_HALF = 64
C_HEADS = 16
GRID_W = 64
NA_ROWS = 8
NA_COLS = 16
REL_BUCKETS = 32
REL_MAX_DIST = 1024
N_GROUPS = 4
EXPERTS_PER_GROUP = 4
N_EXPERTS = 16
D_EXPERT = 512
N_PAIRS = 6
N_CLASSES = N_GROUPS * N_PAIRS
DEEPNORM_ALPHA = (2.0 * DEPTH) ** 0.25
LN_EPS = 1e-5
LOG2E = math.log2(math.e)
ATTN_SCALE = HEAD_DIM ** -0.5 * LOG2E
NEG_INF = -1e30

QA_W = A_HEADS * HEAD_DIM
KVA_W = A_KV_HEADS * HEAD_DIM
A_IN = QA_W + 2 * KVA_W
B_W = B_HEADS * HEAD_DIM
B_IN = 3 * B_W
C_W = C_HEADS * HEAD_DIM

ATT_TB = 1024
MM_TM = 512
MOE_TM = 256
ROW_TM = 512
VMEM_LIMIT = 56 * 1024 * 1024


def _cparams(sem):
    return pltpu.CompilerParams(dimension_semantics=sem, vmem_limit_bytes=VMEM_LIMIT)


def _segment_flags(tok0, size, seg_starts, seg_ends):
    is_first = functools.reduce(jnp.logical_or, [tok0 == s for s in seg_starts])
    is_last = functools.reduce(jnp.logical_or, [tok0 + size == e for e in seg_ends])
    return is_first, is_last


def _t5_bucket_np(rel):
    half_b = REL_BUCKETS // 2
    max_exact = half_b // 2
    n = np.abs(rel)
    large = max_exact + (np.log(np.maximum(n, max_exact).astype(np.float32) / max_exact)
                         / math.log(REL_MAX_DIST / max_exact) * (half_b - max_exact)).astype(np.int32)
    large = np.minimum(large, half_b - 1)
    return np.where(rel > 0, half_b, 0) + np.where(n < max_exact, n, large)


def _banded_bias(table, half, dil):
    rel = np.arange(3 * half)[None, :] - half - np.arange(half)[:, None]
    bucket = jnp.asarray(_t5_bucket_np(rel * dil).astype(np.int32))
    hit = bucket[None] == jnp.arange(REL_BUCKETS, dtype=jnp.int32)[:, None, None]
    bias = jnp.sum(jnp.where(hit[:, None], table.astype(F32)[:, :, None, None], 0.0), axis=0)
    return jnp.where(jnp.asarray(np.abs(rel) <= half)[None], bias * LOG2E, NEG_INF)


def _bias_a(rel_bias):
    w = A_WINDOW
    ch = A_CHUNK
    b = _banded_bias(rel_bias[:, :A_HEADS], w, 1).reshape(A_KV_HEADS, 4, w // ch, ch, 3 * w)
    b = b.transpose(0, 2, 1, 3, 4).reshape(A_KV_HEADS, w // ch, 4 * ch, 3 * w)
    col = np.arange(3 * w)
    first = jnp.where(jnp.asarray(col < w), NEG_INF, b)
    last = jnp.where(jnp.asarray(col >= 2 * w), NEG_INF, b)
    return jnp.stack([b, first, last])


def _bias_b(rel_bias):
    per = [_banded_bias(rel_bias[:, A_HEADS:], B_HALF, d) for _, d in B_BRANCHES]
    b = jnp.stack(per, axis=1)
    return b.reshape(B_HEADS // 2, 2, len(B_BRANCHES), B_HALF, 3 * B_HALF).transpose(0, 2, 1, 3, 4) \
            .reshape(B_HEADS // 2, len(B_BRANCHES), 2 * B_HALF, 3 * B_HALF)


def _bias_c(rpb):
    gw = GRID_W
    n_dr = 2 * NA_ROWS - 1
    side = gw - NA_COLS
    p = jnp.concatenate([jnp.repeat(rpb[..., :1], side, axis=-1), rpb.astype(F32),
                         jnp.repeat(rpb[..., -1:], side + 1, axis=-1)], axis=-1)
    z = jnp.broadcast_to(p[:, :, None, :], (C_HEADS, n_dr, gw, 2 * gw)).reshape(C_HEADS, n_dr, 2 * gw * gw)
    t = z[:, :, gw - 1:gw - 1 + gw * (2 * gw - 1)].reshape(C_HEADS, n_dr, gw, 2 * gw - 1)[..., :gw]
    cq = np.arange(gw)[:, None]
    w = np.arange(gw)[None, :]
    c0 = np.clip(cq - NA_COLS // 2, 0, gw - NA_COLS)
    t = jnp.where(jnp.asarray((w >= c0) & (w < c0 + NA_COLS)), t * LOG2E, NEG_INF)
    bias = jnp.stack([jnp.transpose(t[:, NA_ROWS - 1 - s:2 * NA_ROWS - 1 - s], (0, 2, 1, 3))
                      for s in range(NA_ROWS)], axis=1)
    bias = bias.reshape(C_HEADS // 2, 2, NA_ROWS, GRID_W, NA_ROWS * GRID_W)
    return bias.transpose(0, 2, 1, 3, 4).reshape(C_HEADS // 2, NA_ROWS, 2 * GRID_W, NA_ROWS * GRID_W)


def _segment_blocks(segs, tm):
    return tuple(a.shape[0] // tm for a in segs)


def _segment_specs(seg_blocks, tm, width):
    specs, start = [], 0
    for nb in seg_blocks:
        specs.append(pl.BlockSpec((tm, width), lambda i, *_, s=start, nb=nb: (jnp.clip(i - s, 0, nb - 1), 0)))
        start += nb
    return specs


def _segment_rows(i, refs, seg_blocks):
    x, start = refs[0][...], seg_blocks[0]
    for ref, nb in zip(refs[1:], seg_blocks[1:]):
        x = jnp.where(i >= start, ref[...], x)
        start += nb
    return x


def _inproj_kernel(*refs, splits, seg_blocks):
    n_seg = len(seg_blocks)
    w_ref = refs[n_seg]
    o_refs = refs[n_seg + 1:]
    x = _segment_rows(pl.program_id(0), refs[:n_seg], seg_blocks).astype(BF16)
    for o_ref, (lo, hi) in zip(o_refs, splits):
        o_ref[...] = jnp.dot(x, w_ref[:, lo:hi], preferred_element_type=F32).astype(o_ref.dtype)


def _inproj(xs, w, splits, dtypes):
    seg_blocks = _segment_blocks(xs, MM_TM)
    n = sum(seg_blocks) * MM_TM
    return pl.pallas_call(
        functools.partial(_inproj_kernel, splits=splits, seg_blocks=seg_blocks),
        grid=(n // MM_TM,),
        in_specs=_segment_specs(seg_blocks, MM_TM, D_MODEL) + [pl.BlockSpec(w.shape, lambda i: (0, 0))],
        out_specs=[pl.BlockSpec((MM_TM, hi - lo), lambda i: (i, 0)) for lo, hi in splits],
        out_shape=[jax.ShapeDtypeStruct((n, hi - lo), dt) for (lo, hi), dt in zip(splits, dtypes)],
        compiler_params=_cparams(("parallel",)),
        name="inproj",
    )(*xs, w)


def _layer_norm(z, g, b):
    mu = jnp.mean(z, axis=-1, keepdims=True)
    zc = z - mu
    var = jnp.mean(zc * zc, axis=-1, keepdims=True)
    return zc * lax.rsqrt(var + LN_EPS) * g + b


def _outproj_ln_kernel(*refs, n_parts, seg_blocks):
    n_seg = len(seg_blocks)
    o_refs = refs[:n_parts]
    w_refs = refs[n_parts:2 * n_parts]
    x_refs = refs[2 * n_parts:2 * n_parts + n_seg]
    g_ref, b_ref, wh_ref, wl_ref, br_ref, out_ref, info_ref, cnt_ref, run_scr = refs[2 * n_parts + n_seg:]
    i = pl.program_id(0)
    h = DEEPNORM_ALPHA * _segment_rows(i, x_refs, seg_blocks)
    for o_ref, w_ref in zip(o_refs, w_refs):
        h = h + jnp.dot(o_ref[...], w_ref[...], preferred_element_type=F32)
    out = _layer_norm(h, g_ref[...], b_ref[...])
    out_ref[...] = out
    _route(i, out, wh_ref, wl_ref, br_ref, info_ref, cnt_ref, run_scr)


def _outproj_ln(parts, weights, xs, g, b, w_router, b_router):
    tm = MM_TM
    seg_blocks = _segment_blocks(xs, tm)
    n = sum(seg_blocks) * tm
    n_parts = len(parts)
    wh = w_router.astype(BF16)
    wl = (w_router - wh.astype(F32)).astype(BF16)
    const = lambda shape: pl.BlockSpec(shape, lambda i: (0, 0))
    return pl.pallas_call(
        functools.partial(_outproj_ln_kernel, n_parts=n_parts, seg_blocks=seg_blocks),
        grid=(n // tm,),
        in_specs=([pl.BlockSpec((tm, p.shape[1]), lambda i: (i, 0)) for p in parts]
                  + [const(w.shape) for w in weights]
                  + _segment_specs(seg_blocks, tm, D_MODEL)
                  + [const((1, D_MODEL)), const((1, D_MODEL)),
                     const((D_MODEL, LANES)), const((D_MODEL, LANES)), const((1, LANES))]),
        out_specs=[pl.BlockSpec((tm, D_MODEL), lambda i: (i, 0)),
                   pl.BlockSpec((None, SUBLANES, tm), lambda i: (i, 0, 0)),
                   const((1, LANES))],
        out_shape=[jax.ShapeDtypeStruct((n, D_MODEL), F32),
                   jax.ShapeDtypeStruct((n // tm, SUBLANES, tm), F32),
                   jax.ShapeDtypeStruct((1, LANES), jnp.int32)],
        scratch_shapes=[pltpu.VMEM((1, LANES), F32)],
        compiler_params=_cparams(("arbitrary",)),
        name="outproj_ln",
    )(*parts, *weights, *xs, g.reshape(1, D_MODEL), b.reshape(1, D_MODEL), wh, wl, b_router)


ATT_DEPTH = 3


def _staged(n, weights, values):
    for i in range(min(ATT_DEPTH, n)):
        weights(i)
    for i in range(n):
        if i + ATT_DEPTH < n:
            weights(i + ATT_DEPTH)
        values(i)


def _attn_a_kernel(q_ref, kvm_ref, kvp_ref, kvn_ref, bias_ref, sink_ref, o_ref, kv_scr, p_scr,
                   *, seg_starts, seg_ends):
    w = A_WINDOW
    n_sub = ATT_TB // w
    tok0 = pl.program_id(0) * ATT_TB
    is_first, is_last = _segment_flags(tok0, ATT_TB, seg_starts, seg_ends)
    kv_scr[0:w, :] = kvp_ref[...]
    kv_scr[w:w + ATT_TB, :] = kvm_ref[...]
    kv_scr[w + ATT_TB:, :] = kvn_ref[...]
    low = lax.broadcasted_iota(jnp.int32, (1, LANES), 1) < HEAD_DIM
    ch = A_CHUNK
    per = w // ch

    sinks = [jnp.concatenate([jnp.full((ch, 1), sink_ref[c + 4 * g], F32) for c in range(4)], axis=0)
             for g in range(A_KV_HEADS)]

    def scores(t, g):
        j = t // per
        q = q_ref[t * ch:(t + 1) * ch, :]
        qg = jnp.concatenate([jnp.where(low, q[:, c * LANES:(c + 1) * LANES], 0) if g == 0
                              else jnp.where(low, 0, q[:, c * LANES:(c + 1) * LANES]) for c in range(4)], axis=0)
        k2 = kv_scr[j * w:(j + 3) * w, :LANES]
        return lax.dot_general(qg, k2, (((1,), (1,)), ((), ())), preferred_element_type=F32)

    tiles = [(t, g) for t in range(ATT_TB // ch) for g in range(A_KV_HEADS)]
    rows = 4 * ch
    rdens = {}
    outs = {}

    def weights(i):
        t, g = tiles[i]
        j = t // per
        if j == 0:
            variant = jnp.where(is_first, 1, 0)
        elif j == n_sub - 1:
            variant = jnp.where(is_last, 2, 0)
        else:
            variant = 0
        s = scores(t, g) + bias_ref[variant, g, t % per]
        m = jnp.maximum(jnp.max(s, axis=-1, keepdims=True), sinks[g])
        e = jnp.exp2(s - m)
        rdens[i] = 1.0 / (jnp.sum(e, axis=-1, keepdims=True) + jnp.exp2(sinks[g] - m))
        p_scr[i * rows:(i + 1) * rows, :] = e.astype(BF16)

    def values(i):
        t, g = tiles[i]
        j = t // per
        v2 = kv_scr[j * w:(j + 3) * w, LANES:]
        outs[g] = jnp.dot(p_scr[i * rows:(i + 1) * rows, :], v2, preferred_element_type=F32) * rdens[i]
        if g == A_KV_HEADS - 1:
            for c in range(4):
                oc = jnp.where(low, outs[0][c * ch:(c + 1) * ch], outs[1][c * ch:(c + 1) * ch])
                o_ref[t * ch:(t + 1) * ch, c * LANES:(c + 1) * LANES] = oc.astype(o_ref.dtype)

    _staged(len(tiles), weights, values)


A_CHUNK = 64


def _attn_a(a_qkv, bias, sink, seg_starts, seg_ends):
    n = a_qkv.shape[0]
    w = A_WINDOW
    sub = ATT_TB // w
    nhb = n // w
    kv_col = QA_W // (2 * LANES)
    return pl.pallas_call(
        functools.partial(_attn_a_kernel, seg_starts=seg_starts, seg_ends=seg_ends),
        grid=(n // ATT_TB,),
        in_specs=[pl.BlockSpec((ATT_TB, QA_W), lambda i: (i, 0)),
                  pl.BlockSpec((ATT_TB, 2 * LANES), lambda i: (i, kv_col)),
                  pl.BlockSpec((w, 2 * LANES), lambda i: (jnp.maximum(i * sub - 1, 0), kv_col)),
                  pl.BlockSpec((w, 2 * LANES), lambda i: (jnp.minimum((i + 1) * sub, nhb - 1), kv_col)),
                  pl.BlockSpec(bias.shape, lambda i: (0,) * bias.ndim),
                  pl.BlockSpec(memory_space=pltpu.SMEM)],
        out_specs=pl.BlockSpec((ATT_TB, QA_W), lambda i: (i, 0)),
        out_shape=jax.ShapeDtypeStruct((n, QA_W), BF16),
        scratch_shapes=[pltpu.VMEM((ATT_TB + 2 * w, 2 * LANES), BF16),
                        pltpu.VMEM((A_HEADS * ATT_TB, 3 * w), BF16)],
        compiler_params=_cparams(("parallel",)),
        name="attn_a",
    )(a_qkv, a_qkv, a_qkv, a_qkv, bias, sink)


def _attn_b_kernel(q_ref, kp_ref, km_ref, kn_ref, vp_ref, vm_ref, vn_ref, bias_ref, o_ref,
                   k_scr, v_scr, o_scr, m_scr, l_scr, p_scr, *, seg_starts, seg_ends):
    tb = ATT_TB
    h = B_HALF
    tok0 = pl.program_id(1) * tb
    is_first, is_last = _segment_flags(tok0, tb, seg_starts, seg_ends)
    k_scr[0:tb, :] = kp_ref[...]
    k_scr[tb:2 * tb, :] = km_ref[...]
    k_scr[2 * tb:, :] = kn_ref[...]
    v_scr[0:tb, :] = vp_ref[...]
    v_scr[tb:2 * tb, :] = vm_ref[...]
    v_scr[2 * tb:, :] = vn_ref[...]
    lane = lax.broadcasted_iota(jnp.int32, (1, LANES), 1)
    low = lane < HEAD_DIM
    col = lax.broadcasted_iota(jnp.int32, (1, 3 * h), 1)

    pen_first = jnp.where(jnp.logical_and(col < h, is_first), NEG_INF, 0.0)
    pen_last = jnp.where(jnp.logical_and(col >= 2 * h, is_last), NEG_INF, 0.0)

    def slices(d, r, b):
        row0 = r + h * d * b
        if d == 1:
            return pl.ds(row0, h), pl.ds(tb + row0 - h, 3 * h)
        return pl.ds(row0, h, stride=d), pl.ds(tb + row0 - h * d, 3 * h, stride=d)

    def scores(br, d, r, b):
        qs, ks = slices(d, r, b)
        q = q_ref[qs, :].astype(BF16)
        k = k_scr[ks, :].astype(BF16)
        qq = jnp.concatenate([jnp.where(low, q, 0), jnp.where(low, 0, q)], axis=0)
        return lax.dot_general(qq, k, (((1,), (1,)), ((), ())), preferred_element_type=F32)

    tiles = [(br, d, r, b) for br, (_, d) in enumerate(B_BRANCHES)
             for r in range(d) for b in range(tb // (h * d))]

    def weights(i):
        br, d, r, b = tiles[i]
        qs, _ = slices(d, r, b)
        s = scores(br, d, r, b) + bias_ref[br]
        if b == 0:
            s = s + pen_first
        if b == tb // (h * d) - 1:
            s = s + pen_last
        m = jnp.max(s, axis=-1, keepdims=True)
        e = jnp.exp2(s - m)
        l = jnp.sum(e, axis=-1, keepdims=True)
        p_scr[i * 2 * h:(i + 1) * 2 * h, :] = e.astype(BF16)
        m_scr[br, qs, :] = jnp.where(low, m[:h], m[h:])
        l_scr[br, qs, :] = jnp.where(low, l[:h], l[h:])

    def values(i):
        br, d, r, b = tiles[i]
        qs, ks = slices(d, r, b)
        v = v_scr[ks, :].astype(BF16)
        pv = jnp.dot(p_scr[i * 2 * h:(i + 1) * 2 * h, :], v, preferred_element_type=F32)
        o_scr[br, qs, :] = jnp.where(low, pv[:h], pv[h:])

    _staged(len(tiles), weights, values)

    m_all = jnp.maximum(jnp.maximum(m_scr[0], m_scr[1]), m_scr[2])
    num = jnp.zeros((tb, LANES), F32)
    den = jnp.zeros((tb, LANES), F32)
    for br in range(len(B_BRANCHES)):
        a = jnp.exp2(m_scr[br] - m_all)
        num = num + a * o_scr[br]
        den = den + a * l_scr[br]
    o_ref[...] = (num / den).astype(o_ref.dtype)


def _attn_b(b_qkv, bias, seg_starts, seg_ends):
    n = b_qkv.shape[0]
    tb = ATT_TB
    nblk = n // tb
    npair = B_HEADS // 2
    prev = lambda i: jnp.maximum(i - 1, 0)
    nxt = lambda i: jnp.minimum(i + 1, nblk - 1)
    blk = lambda rowf, off: pl.BlockSpec((tb, LANES), lambda c, i: (rowf(i), off + c))
    same = lambda i: i
    stat = pltpu.VMEM((len(B_BRANCHES), tb, LANES), F32)
    return pl.pallas_call(
        functools.partial(_attn_b_kernel, seg_starts=seg_starts, seg_ends=seg_ends),
        grid=(npair, nblk),
        in_specs=[blk(same, 0),
                  blk(prev, npair), blk(same, npair), blk(nxt, npair),
                  blk(prev, 2 * npair), blk(same, 2 * npair), blk(nxt, 2 * npair),
                  pl.BlockSpec((None,) + bias.shape[1:], lambda c, i: (c, 0, 0, 0))],
        out_specs=pl.BlockSpec((tb, LANES), lambda c, i: (i, c)),
        out_shape=jax.ShapeDtypeStruct((n, B_W), BF16),
        scratch_shapes=[pltpu.VMEM((3 * tb, LANES), F32), pltpu.VMEM((3 * tb, LANES), F32),
                        stat, stat, stat,
                        pltpu.VMEM((len(B_BRANCHES) * 2 * tb, 3 * B_HALF), BF16)],
        compiler_params=_cparams(("parallel", "parallel")),
        name="attn_b",
    )(b_qkv, b_qkv, b_qkv, b_qkv, b_qkv, b_qkv, b_qkv, bias)


C_HALO = (NA_ROWS // 2) * GRID_W

def _attn_c_kernel(q_ref, kp_ref, km_ref, kn_ref, vp_ref, vm_ref, vn_ref, bias_ref, o_ref,
                   k_scr, v_scr, p_scr, *, seg_starts, seg_ends):
    tb = ATT_TB
    gw = GRID_W
    nkeys = NA_ROWS * gw
    tok0 = pl.program_id(1) * tb
    k_scr[0:C_HALO, :] = kp_ref[...]
    k_scr[C_HALO:C_HALO + tb, :] = km_ref[...]
    k_scr[C_HALO + tb:, :] = kn_ref[...]
    v_scr[0:C_HALO, :] = vp_ref[...]
    v_scr[C_HALO:C_HALO + tb, :] = vm_ref[...]
    v_scr[C_HALO + tb:, :] = vn_ref[...]
    seg_row0 = jnp.int32(0)
    seg_rows = jnp.int32(0)
    for s, e in zip(seg_starts, seg_ends):
        inside = jnp.logical_and(tok0 >= s, tok0 < e)
        seg_row0 = jnp.where(inside, s // gw, seg_row0)
        seg_rows = jnp.where(inside, (e - s) // gw, seg_rows)
    lane = lax.broadcasted_iota(jnp.int32, (1, LANES), 1)
    low = lane < HEAD_DIM

    def window(rr):
        rs = tok0 // gw + rr - seg_row0
        start = jnp.clip(rs - NA_ROWS // 2, 0, seg_rows - NA_ROWS)
        shift = rs - start
        return shift, pl.ds(pl.multiple_of((rr + NA_ROWS // 2 - shift) * gw, gw), nkeys)

    def scores(rr):
        _, ks = window(rr)
        q = q_ref[rr * gw:(rr + 1) * gw, :]
        qq = jnp.concatenate([jnp.where(low, q, 0), jnp.where(low, 0, q)], axis=0)
        return lax.dot_general(qq, k_scr[ks, :], (((1,), (1,)), ((), ())), preferred_element_type=F32)

    n_rows = tb // gw
    rdens = {}

    def weights(rr):
        shift, _ = window(rr)
        s = scores(rr) + bias_ref[shift]
        m = jnp.max(s, axis=-1, keepdims=True)
        e = jnp.exp2(s - m)
        rdens[rr] = 1.0 / jnp.sum(e, axis=-1, keepdims=True)
        p_scr[rr * 2 * gw:(rr + 1) * 2 * gw, :] = e.astype(BF16)

    def values(rr):
        _, ks = window(rr)
        pv = jnp.dot(p_scr[rr * 2 * gw:(rr + 1) * 2 * gw, :], v_scr[ks, :], preferred_element_type=F32) * rdens[rr]
        o_ref[rr * gw:(rr + 1) * gw, :] = jnp.where(low, pv[:gw], pv[gw:]).astype(o_ref.dtype)

    _staged(n_rows, weights, values)


def _attn_c(c_qkv, bias, seg_starts, seg_ends):
    n = c_qkv.shape[0]
    tb = ATT_TB
    npair = C_HEADS // 2
    sub = tb // C_HALO
    nhb = n // C_HALO
    main = lambda off: pl.BlockSpec((tb, LANES), lambda c, i: (i, off + c))
    prev = lambda off: pl.BlockSpec((C_HALO, LANES), lambda c, i: (jnp.maximum(i * sub - 1, 0), off + c))
    nxt = lambda off: pl.BlockSpec((C_HALO, LANES),
                                   lambda c, i: (jnp.minimum((i + 1) * sub, nhb - 1), off + c))
    return pl.pallas_call(
        functools.partial(_attn_c_kernel, seg_starts=seg_starts, seg_ends=seg_ends),
        grid=(npair, n // tb),
        in_specs=[main(0),
                  prev(npair), main(npair), nxt(npair),
                  prev(2 * npair), main(2 * npair), nxt(2 * npair),
                  pl.BlockSpec((None,) + bias.shape[1:], lambda c, i: (c, 0, 0, 0))],
        out_specs=pl.BlockSpec((tb, LANES), lambda c, i: (i, c)),
        out_shape=jax.ShapeDtypeStruct((n, C_W), BF16),
        scratch_shapes=[pltpu.VMEM((tb + 2 * C_HALO, LANES), BF16),
                        pltpu.VMEM((tb + 2 * C_HALO, LANES), BF16),
                        pltpu.VMEM((2 * tb, NA_ROWS * GRID_W), BF16)],
        compiler_params=_cparams(("parallel", "parallel")),
        name="attn_c",
    )(c_qkv, c_qkv, c_qkv, c_qkv, c_qkv, c_qkv, c_qkv, bias)


def _route(i, x, wh_ref, wl_ref, b_ref, info_ref, cnt_ref, run_scr):
    tm = x.shape[0]

    @pl.when(i == 0)
    def _():
        run_scr[...] = jnp.zeros_like(run_scr)

    xh = x.astype(BF16)
    xl = (x - xh.astype(F32)).astype(BF16)
    logits = (jnp.dot(xh, wh_ref[...], preferred_element_type=F32)
              + jnp.dot(xl, wh_ref[...], preferred_element_type=F32)
              + jnp.dot(xh, wl_ref[...], preferred_element_type=F32)) + b_ref[...]
    lane = lax.broadcasted_iota(jnp.int32, (tm, LANES), 1)
    big = jnp.int32(LANES)
    is_g = lane < N_GROUPS
    lg = jnp.where(is_g, logits, NEG_INF)
    mg = jnp.max(lg, axis=-1, keepdims=True)
    g_sel = jnp.min(jnp.where(jnp.logical_and(is_g, lg == mg), lane, big), axis=-1, keepdims=True)
    e_lo = N_GROUPS + g_sel * EXPERTS_PER_GROUP
    in_grp = jnp.logical_and(lane >= e_lo, lane < e_lo + EXPERTS_PER_GROUP)
    le = jnp.where(in_grp, logits, NEG_INF)
    v1 = jnp.max(le, axis=-1, keepdims=True)
    i1 = jnp.min(jnp.where(jnp.logical_and(in_grp, le == v1), lane, big), axis=-1, keepdims=True)
    rest = jnp.logical_and(in_grp, lane != i1)
    le2 = jnp.where(rest, logits, NEG_INF)
    v2 = jnp.max(le2, axis=-1, keepdims=True)
    i2 = jnp.min(jnp.where(jnp.logical_and(rest, le2 == v2), lane, big), axis=-1, keepdims=True)
    a = jnp.minimum(i1, i2) - e_lo
    b = jnp.maximum(i1, i2) - e_lo
    pair = a * 3 - jnp.where(a == 2, 1, 0) + (b - a - 1)
    cls = g_sel * N_PAIRS + pair
    onehot = (lane == cls)
    ri = lax.broadcasted_iota(jnp.int32, (tm, tm), 0)
    ci = lax.broadcasted_iota(jnp.int32, (tm, tm), 1)
    tril = (ci < ri).astype(BF16)
    before = jnp.dot(tril, onehot.astype(BF16), preferred_element_type=F32) + run_scr[...]
    rank = jnp.sum(jnp.where(onehot, before, 0.0), axis=-1, keepdims=True)
    run_scr[...] = run_scr[...] + jnp.sum(onehot.astype(F32), axis=0, keepdims=True)
    rank_hi = jnp.floor(rank * (1.0 / RANK_BASE))
    rank_lo = rank - rank_hi * RANK_BASE
    cols = jnp.where(lane == 0, cls.astype(F32), jnp.where(lane == 1, rank_hi, jnp.where(lane == 2, rank_lo, 0.0)))
    pick = (lax.broadcasted_iota(jnp.int32, (SUBLANES, LANES), 0)
            == lax.broadcasted_iota(jnp.int32, (SUBLANES, LANES), 1)).astype(BF16)
    info_ref[...] = lax.dot_general(pick, cols.astype(BF16), (((1,), (1,)), ((), ())),
                                    preferred_element_type=F32)
    cnt_ref[...] = run_scr[...].astype(jnp.int32)


RANK_BASE = 256
SUBLANES = 8


def _tile_copy(src, src_tok, dst, dst_tok, sem):
    return pltpu.make_async_copy(src.at[pl.ds(pl.multiple_of(src_tok * SUBLANES, SUBLANES), SUBLANES), :],
                                 dst.at[pl.ds(pl.multiple_of(dst_tok * SUBLANES, SUBLANES), SUBLANES), :], sem)


ROW_UNROLL = 8


def _start_rows(copy, n):
    def body(g, carry):
        for u in range(ROW_UNROLL):
            copy(g * ROW_UNROLL + u).start(priority=u % 2)
        return carry

    lax.fori_loop(0, n // ROW_UNROLL, body, 0)


def _dispatch_kernel(dest_ref, pad_ref, x_ref, xs_ref, rec_scr, zero_scr, sems, zsem):
    i = pl.program_id(0)
    last = pl.num_programs(0) - 1
    tm = x_ref.shape[0]
    rows = tm * SUBLANES
    slot = i % 2
    tile_rows = MOE_TM * SUBLANES

    def zero_copy(c):
        start = pl.multiple_of(pad_ref[c] * SUBLANES, tile_rows)
        return pltpu.make_async_copy(zero_scr, xs_ref.at[pl.ds(start, tile_rows), :], zsem)

    @pl.when(i == 0)
    def _():
        zero_scr[...] = jnp.zeros_like(zero_scr)
        for c in range(pad_ref.shape[0]):
            @pl.when(pad_ref[c] >= 0)
            def _():
                zero_copy(c).start()
        for c in range(pad_ref.shape[0]):
            @pl.when(pad_ref[c] >= 0)
            def _():
                zero_copy(c).wait()

    for j in range(D_MODEL // LANES):
        rec_scr[slot, pl.ds(j, tm, stride=SUBLANES), :] = x_ref[:, j * LANES:(j + 1) * LANES]

    _start_rows(lambda r: _tile_copy(rec_scr.at[slot], r, xs_ref, dest_ref[i * tm + r], sems.at[slot]), tm)

    def wait_step(s):
        pltpu.make_async_copy(rec_scr.at[s], xs_ref.at[pl.ds(0, rows), :], sems.at[s]).wait()

    @pl.when(i > 0)
    def _():
        wait_step(1 - slot)

    @pl.when(i == last)
    def _():
        wait_step(slot)


def _dispatch(dest, pad_start, x, n_sorted):
    n = x.shape[0]
    tm = ROW_TM
    return pl.pallas_call(
        _dispatch_kernel,
        grid_spec=pltpu.PrefetchScalarGridSpec(
            num_scalar_prefetch=2,
            grid=(n // tm,),
            in_specs=[pl.BlockSpec((tm, D_MODEL), lambda i, d, p: (i, 0))],
            out_specs=pl.BlockSpec(memory_space=pl.ANY),
            scratch_shapes=[pltpu.VMEM((2, tm * SUBLANES, LANES), F32),
                            pltpu.VMEM((MOE_TM * SUBLANES, LANES), F32),
                            pltpu.SemaphoreType.DMA((2,)), pltpu.SemaphoreType.DMA]),
        out_shape=jax.ShapeDtypeStruct((n_sorted * SUBLANES, LANES), F32),
        compiler_params=_cparams(("arbitrary",)),
        name="dispatch",
    )(dest, pad_start, x)


def _expert_kernel(ea_ref, eb_ref, nt_ref, xs_ref, wr_ref, br_ref,
                   wga_ref, wua_ref, wda_ref, wgb_ref, wub_ref, wdb_ref, g_ref, b_ref, ys_ref, hid_scr):
    p = pl.program_id(0)
    tm = MOE_TM

    @pl.when(p >= nt_ref[0])
    def _():
        ys_ref[...] = jnp.zeros_like(ys_ref)

    @pl.when(p < nt_ref[0])
    def _():
        x = jnp.concatenate([xs_ref[pl.ds(j, tm, stride=SUBLANES), :] for j in range(D_MODEL // LANES)],
                            axis=1)
        xb = x.astype(BF16)

        logits = jnp.dot(xb, wr_ref[...], preferred_element_type=F32) + br_ref[...]
        lane = lax.broadcasted_iota(jnp.int32, (tm, LANES), 1)
        lane_a = N_GROUPS + ea_ref[p]
        lane_b = N_GROUPS + eb_ref[p]
        grp = ea_ref[p] // EXPERTS_PER_GROUP
        is_g = lane < N_GROUPS
        mg = jnp.max(jnp.where(is_g, logits, NEG_INF), axis=-1, keepdims=True)
        eg = jnp.where(is_g, jnp.exp(logits - mg), 0.0)
        g_gate = (jnp.sum(jnp.where(lane == grp, eg, 0.0), axis=-1, keepdims=True)
                  / jnp.sum(eg, axis=-1, keepdims=True))
        l_a = jnp.sum(jnp.where(lane == lane_a, logits, 0.0), axis=-1, keepdims=True)
        l_b = jnp.sum(jnp.where(lane == lane_b, logits, 0.0), axis=-1, keepdims=True)
        mx = jnp.maximum(l_a, l_b)
        p_a = jnp.exp(l_a - mx)
        p_b = jnp.exp(l_b - mx)
        scale = g_gate / (p_a + p_b)
        w_a = p_a * scale
        w_b = p_b * scale

        for e, (wg_ref, wu_ref) in enumerate(((wga_ref, wua_ref), (wgb_ref, wub_ref))):
            gate = jnp.dot(xb, wg_ref[...], preferred_element_type=F32)
            up = jnp.dot(xb, wu_ref[...], preferred_element_type=F32)
            hid_scr[e] = (gate * (1.0 / (1.0 + jnp.exp(-gate))) * up).astype(BF16)
        y = (w_a * jnp.dot(hid_scr[0], wda_ref[...], preferred_element_type=F32)
             + w_b * jnp.dot(hid_scr[1], wdb_ref[...], preferred_element_type=F32))
        out = _layer_norm(DEEPNORM_ALPHA * x + y, g_ref[...], b_ref[...])
        for j in range(D_MODEL // LANES):
            ys_ref[pl.ds(j, tm, stride=SUBLANES), :] = out[:, j * LANES:(j + 1) * LANES]


def _experts(layer, tile_ea, tile_eb, n_tiles, xs, w_router, b_router, w_gate, w_up, w_down, g, b):
    tm = MOE_TM
    rows = tm * SUBLANES
    n_grid = xs.shape[0] // rows
    last = lambda p, nt: jnp.maximum(jnp.minimum(p, nt[0] - 1), 0)
    wspec_a = lambda shape: pl.BlockSpec((None, None) + shape, lambda p, ea, eb, nt: (layer, ea[p], 0, 0))
    wspec_b = lambda shape: pl.BlockSpec((None, None) + shape, lambda p, ea, eb, nt: (layer, eb[p], 0, 0))
    const = lambda shape: pl.BlockSpec(shape, lambda p, ea, eb, nt: (0, 0))
    up_shape = (D_MODEL, D_EXPERT)
    dn_shape = (D_EXPERT, D_MODEL)
    return pl.pallas_call(
        _expert_kernel,
        grid_spec=pltpu.PrefetchScalarGridSpec(
            num_scalar_prefetch=3,
            grid=(n_grid,),
            in_specs=[pl.BlockSpec((rows, LANES), lambda p, ea, eb, nt: (last(p, nt), 0)),
                      const((D_MODEL, LANES)), const((1, LANES)),
                      wspec_a(up_shape), wspec_a(up_shape), wspec_a(dn_shape),
                      wspec_b(up_shape), wspec_b(up_shape), wspec_b(dn_shape),
                      const((1, D_MODEL)), const((1, D_MODEL))],
            out_specs=pl.BlockSpec((rows, LANES), lambda p, ea, eb, nt: (p, 0)),
            scratch_shapes=[pltpu.VMEM((2, tm, D_EXPERT), BF16)]),
        out_shape=jax.ShapeDtypeStruct(xs.shape, F32),
        compiler_params=_cparams(("arbitrary",)),
        name="experts",
    )(tile_ea, tile_eb, n_tiles, xs, w_router.astype(BF16), b_router,
      w_gate, w_up, w_down, w_gate, w_up, w_down, g.reshape(1, D_MODEL), b.reshape(1, D_MODEL))


def _gather_kernel(dest_ref, ys_ref, *refs, seg_blocks):
    out_refs = refs[:len(seg_blocks)]
    rec_scr, sems = refs[len(seg_blocks):]
    i = pl.program_id(0)
    n_steps = pl.num_programs(0)
    tm = out_refs[0].shape[0]
    rows = tm * SUBLANES
    slot = i % 2

    def fetch(s):
        _start_rows(lambda r: _tile_copy(ys_ref, dest_ref[s * tm + r], rec_scr.at[s % 2], r, sems.at[s % 2]), tm)

    @pl.when(i == 0)
    def _():
        fetch(i)

    @pl.when(i + 1 < n_steps)
    def _():
        fetch(i + 1)

    pltpu.make_async_copy(ys_ref.at[pl.ds(0, rows), :], rec_scr.at[slot], sems.at[slot]).wait()

    start = 0
    for out_ref, nb in zip(out_refs, seg_blocks):
        @pl.when(jnp.logical_and(i >= start, i < start + nb))
        def _(out_ref=out_ref):
            for j in range(D_MODEL // LANES):
                out_ref[:, j * LANES:(j + 1) * LANES] = rec_scr[slot, pl.ds(j, tm, stride=SUBLANES), :]
        start += nb


def _gather_rows(dest, ys, seg_rows):
    tm = ROW_TM
    seg_blocks = tuple(r // tm for r in seg_rows)
    return pl.pallas_call(
        functools.partial(_gather_kernel, seg_blocks=seg_blocks),
        grid_spec=pltpu.PrefetchScalarGridSpec(
            num_scalar_prefetch=1,
            grid=(sum(seg_blocks),),
            in_specs=[pl.BlockSpec(memory_space=pl.ANY)],
            out_specs=_segment_specs(seg_blocks, tm, D_MODEL),
            scratch_shapes=[pltpu.VMEM((2, tm * SUBLANES, LANES), F32), pltpu.SemaphoreType.DMA((2,))]),
        out_shape=[jax.ShapeDtypeStruct((r, D_MODEL), F32) for r in seg_rows],
        compiler_params=_cparams(("arbitrary",)),
        name="gather_rows",
    )(dest, ys)


_PAIR_A = np.array([0, 0, 0, 1, 1, 2], np.int32)
_PAIR_B = np.array([1, 2, 3, 2, 3, 3], np.int32)


def _moe_layer(layer, x, info, counts, w_router, b_router, w_gate, w_up, w_down, g, b, out_rows):
    n = x.shape[0]
    tm = MOE_TM
    n_sorted = n + N_CLASSES * tm
    cls = info[:, 0, :].reshape(n).astype(jnp.int32)
    rank = (info[:, 1, :] * RANK_BASE + info[:, 2, :]).reshape(n).astype(jnp.int32)
    counts = counts[0, :N_CLASSES]
    padded = (counts + tm - 1) // tm * tm
    classes = jnp.arange(N_CLASSES, dtype=jnp.int32)
    ends = jnp.sum(jnp.where(classes[None, :] <= classes[:, None], padded[None, :], 0), axis=1)
    offs = ends - padded
    total = ends[N_CLASSES - 1]
    dest = rank + jnp.sum(jnp.where(cls[:, None] == classes[None, :], offs[None, :], 0), axis=1)
    unused = total + classes * tm
    pad_start = jnp.concatenate([jnp.where(padded > 0, ends - tm, -1),
                                 jnp.where(unused < n_sorted, unused, -1)]).astype(jnp.int32)
    tile_start = jnp.arange(n_sorted // tm, dtype=jnp.int32) * tm
    tile_start = jnp.minimum(tile_start, total - tm)
    tile_cls = jnp.sum((ends[None, :] <= tile_start[:, None]).astype(jnp.int32), axis=1)
    pair = tile_cls % N_PAIRS
    pair_a = jnp.sum(jnp.where(pair[:, None] == np.arange(N_PAIRS)[None, :], _PAIR_A[None, :], 0), axis=1)
    pair_b = jnp.sum(jnp.where(pair[:, None] == np.arange(N_PAIRS)[None, :], _PAIR_B[None, :], 0), axis=1)
    grp = tile_cls // N_PAIRS
    tile_ea = (grp * EXPERTS_PER_GROUP + pair_a).astype(jnp.int32)
    tile_eb = (grp * EXPERTS_PER_GROUP + pair_b).astype(jnp.int32)
    n_tiles = (total // tm).astype(jnp.int32).reshape(1)
    xs = _dispatch(dest.astype(jnp.int32), pad_start, x, n_sorted)
    ys = _experts(layer, tile_ea, tile_eb, n_tiles, xs, w_router, b_router, w_gate, w_up, w_down, g, b)
    return _gather_rows(dest.astype(jnp.int32), ys, out_rows)


_A_ORDER = np.array([0, 4, 1, 5, 2, 6, 3, 7])


def _prep_ab(w_in, w_out):
    qa = w_in[:, :QA_W].reshape(D_MODEL, A_HEADS, HEAD_DIM)[:, _A_ORDER].reshape(D_MODEL, QA_W) * ATTN_SCALE
    kva = w_in[:, QA_W:A_IN]
    qb = w_in[:, A_IN:A_IN + B_W] * ATTN_SCALE
    kvb = w_in[:, A_IN + B_W:]
    w = jnp.concatenate([qa, kva, qb, kvb], axis=1).astype(BF16)
    wo_a = w_out[:QA_W].reshape(A_HEADS, HEAD_DIM, D_MODEL)[_A_ORDER].reshape(QA_W, D_MODEL).astype(BF16)
    wo_b = w_out[QA_W:].astype(BF16)
    return w, wo_a, wo_b


def _prep_c(w_in, w_out):
    w = jnp.concatenate([w_in[:, :C_W] * ATTN_SCALE, w_in[:, C_W:]], axis=1).astype(BF16)
    return w, w_out.astype(BF16)


def _trunk(xs, seg_starts, seg_ends, rel_bias, w_in_ab, a_sink, w_out_ab, w_in_c, c_rpb, w_out_c,
           ln1_g, ln1_b, ln2_g, ln2_b, router_g_w, router_g_b, router_e_w, router_e_b,
           w_gate, w_up, w_down):
    io_rows = tuple(a.shape[0] for a in xs)
    n = sum(io_rows)
    bias_a = _bias_a(rel_bias)
    bias_b = _bias_b(rel_bias)
    w_gate = w_gate.astype(BF16)
    w_up = w_up.astype(BF16)
    w_down = w_down.astype(BF16)
    for l in range(DEPTH):
        i = l // 2
        pad = LANES - N_GROUPS - N_EXPERTS
        w_router = jnp.pad(jnp.concatenate([router_g_w[l], router_e_w[l]], axis=1), ((0, 0), (0, pad)))
        b_router = jnp.pad(jnp.concatenate([router_g_b[l], router_e_b[l]]), (0, pad)).reshape(1, LANES)
        if l % 2 == 0:
            w, wo_a, wo_b = _prep_ab(w_in_ab[i], w_out_ab[i])
            a_qkv, b_qkv = _inproj(xs, w, ((0, A_IN), (A_IN, A_IN + B_IN)), (BF16, F32))
            o_a = _attn_a(a_qkv, bias_a, a_sink[i].astype(F32) * LOG2E, seg_starts, seg_ends)
            o_b = _attn_b(b_qkv, bias_b, seg_starts, seg_ends)
            parts, weights = [o_a, o_b], [wo_a, wo_b]
        else:
            w, wo = _prep_c(w_in_c[i], w_out_c[i])
            (c_qkv,) = _inproj(xs, w, ((0, 3 * C_W),), (BF16,))
            parts, weights = [_attn_c(c_qkv, _bias_c(c_rpb[i]), seg_starts, seg_ends)], [wo]
        x, info, counts = _outproj_ln(parts, weights, xs, ln1_g[l], ln1_b[l], w_router, b_router)
        xs = _moe_layer(l, x, info, counts, w_router, b_router, w_gate, w_up, w_down, ln2_g[l], ln2_b[l],
                        io_rows if l == DEPTH - 1 else (n,))
    return xs


def kernel(x_prompt, x_sample, rel_bias, w_in_ab, a_sink, w_out_ab, w_in_c, c_rpb, w_out_c,
           ln1_g, ln1_b, ln2_g, ln2_b, router_g_w, router_g_b, router_e_w, router_e_b,
           w_gate, w_up, w_down):
    lens = [x_prompt.shape[1]] * x_prompt.shape[0] + [x_sample.shape[1]] * x_sample.shape[0]
    seg_ends = tuple(int(v) for v in np.cumsum(lens))
    seg_starts = tuple(e - n for e, n in zip(seg_ends, lens))
    for n in lens:
        assert n % ATT_TB == 0 and n // GRID_W >= NA_ROWS
    xs = [x_prompt.reshape(-1, D_MODEL), x_sample.reshape(-1, D_MODEL)]
    y_p, y_s = _trunk(xs, seg_starts, seg_ends, rel_bias, w_in_ab, a_sink, w_out_ab, w_in_c, c_rpb, w_out_c,
                      ln1_g, ln1_b, ln2_g, ln2_b, router_g_w, router_g_b, router_e_w, router_e_b,
                      w_gate, w_up, w_down)
    return (y_p.reshape(x_prompt.shape), y_s.reshape(x_sample.shape))
```

```python
import functools
import math

import numpy as np
import jax
import jax.numpy as jnp
from jax import lax
from jax.experimental import pallas as pl
from jax.experimental.pallas import tpu as pltpu

F32 = jnp.float32
BF16 = jnp.bfloat16

D_MODEL = 1024
DEPTH = 4
HEAD_DIM = 64
LANES = 128
A_HEADS = 8
A_KV_HEADS = 2
A_WINDOW = 128
B_HEADS = 8
B_BRANCHES = ((128, 1), (512, 4), (2048, 16))
B_HALF = 64
C_HEADS = 16
GRID_W = 64
NA_ROWS = 8
NA_COLS = 16
REL_BUCKETS = 32
REL_MAX_DIST = 1024
N_GROUPS = 4
EXPERTS_PER_GROUP = 4
N_EXPERTS = 16
D_EXPERT = 512
N_PAIRS = 6
N_CLASSES = N_GROUPS * N_PAIRS
DEEPNORM_ALPHA = (2.0 * DEPTH) ** 0.25
LN_EPS = 1e-5
LOG2E = math.log2(math.e)
ATTN_SCALE = HEAD_DIM ** -0.5 * LOG2E
NEG_INF = -1e30

QA_W = A_HEADS * HEAD_DIM
KVA_W = A_KV_HEADS * HEAD_DIM
A_IN = QA_W + 2 * KVA_W
B_W = B_HEADS * HEAD_DIM
B_IN = 3 * B_W
C_W = C_HEADS * HEAD_DIM

ATT_TB = 1024
MM_TM = 512
MOE_TM = 256
ROW_TM = 512
VMEM_LIMIT = 56 * 1024 * 1024


def _cparams(sem):
    return pltpu.CompilerParams(dimension_semantics=sem, vmem_limit_bytes=VMEM_LIMIT)


def _segment_flags(tok0, size, seg_starts, seg_ends):
    is_first = functools.reduce(jnp.logical_or, [tok0 == s for s in seg_starts])
    is_last = functools.reduce(jnp.logical_or, [tok0 + size == e for e in seg_ends])
    return is_first, is_last


def _t5_bucket_np(rel):
    half_b = REL_BUCKETS // 2
    max_exact = half_b // 2
    n = np.abs(rel)
    large = max_exact + (np.log(np.maximum(n, max_exact).astype(np.float32) / max_exact)
                         / math.log(REL_MAX_DIST / max_exact) * (half_b - max_exact)).astype(np.int32)
    large = np.minimum(large, half_b - 1)
    return np.where(rel > 0, half_b, 0) + np.where(n < max_exact, n, large)


def _banded_bias(table, half, dil):
    rel = np.arange(3 * half)[None, :] - half - np.arange(half)[:, None]
    bucket = jnp.asarray(_t5_bucket_np(rel * dil).astype(np.int32))
    hit = bucket[None] == jnp.arange(REL_BUCKETS, dtype=jnp.int32)[:, None, None]
    bias = jnp.sum(jnp.where(hit[:, None], table.astype(F32)[:, :, None, None], 0.0), axis=0)
    return jnp.where(jnp.asarray(np.abs(rel) <= half)[None], bias * LOG2E, NEG_INF)


def _bias_a(rel_bias):
    w = A_WINDOW
    ch = A_CHUNK
    b = _banded_bias(rel_bias[:, :A_HEADS], w, 1).reshape(A_KV_HEADS, 4, w // ch, ch, 3 * w)
    b = b.transpose(0, 2, 1, 3, 4).reshape(A_KV_HEADS, w // ch, 4 * ch, 3 * w)
    col = np.arange(3 * w)
    first = jnp.where(jnp.asarray(col < w), NEG_INF, b)
    last = jnp.where(jnp.asarray(col >= 2 * w), NEG_INF, b)
    return jnp.stack([b, first, last])


def _bias_b(rel_bias):
    per = [_banded_bias(rel_bias[:, A_HEADS:], B_HALF, d) for _, d in B_BRANCHES]
    b = jnp.stack(per, axis=1)
    return b.reshape(B_HEADS // 2, 2, len(B_BRANCHES), B_HALF, 3 * B_HALF).transpose(0, 2, 1, 3, 4) \
            .reshape(B_HEADS // 2, len(B_BRANCHES), 2 * B_HALF, 3 * B_HALF)


def _bias_c(rpb):
    gw = GRID_W
    n_dr = 2 * NA_ROWS - 1
    side = gw - NA_COLS
    p = jnp.concatenate([jnp.repeat(rpb[..., :1], side, axis=-1), rpb.astype(F32),
                         jnp.repeat(rpb[..., -1:], side + 1, axis=-1)], axis=-1)
    z = jnp.broadcast_to(p[:, :, None, :], (C_HEADS, n_dr, gw, 2 * gw)).reshape(C_HEADS, n_dr, 2 * gw * gw)
    t = z[:, :, gw - 1:gw - 1 + gw * (2 * gw - 1)].reshape(C_HEADS, n_dr, gw, 2 * gw - 1)[..., :gw]
    cq = np.arange(gw)[:, None]
    w = np.arange(gw)[None, :]
    c0 = np.clip(cq - NA_COLS // 2, 0, gw - NA_COLS)
    t = jnp.where(jnp.asarray((w >= c0) & (w < c0 + NA_COLS)), t * LOG2E, NEG_INF)
    t = t.reshape(C_HEADS // 2, 2, n_dr, gw, gw).transpose(0, 1, 3, 2, 4)
    bias = jnp.stack([t[:, :, :, NA_ROWS - 1 - s:2 * NA_ROWS - 1 - s, :].reshape(C_HEADS // 2, 2, gw, NA_ROWS * gw)
                      for s in range(NA_ROWS)], axis=1)
    return bias.reshape(C_HEADS // 2, NA_ROWS, 2 * gw, NA_ROWS * gw)


def _segment_blocks(segs, tm):
    return tuple(a.shape[0] // tm for a in segs)


def _segment_specs(seg_blocks, tm, width):
    specs, start = [], 0
    for nb in seg_blocks:
        specs.append(pl.BlockSpec((tm, width), lambda i, *_, s=start, nb=nb: (jnp.clip(i - s, 0, nb - 1), 0)))
        start += nb
    return specs


def _segment_rows(i, refs, seg_blocks):
    x, start = refs[0][...], seg_blocks[0]
    for ref, nb in zip(refs[1:], seg_blocks[1:]):
        x = jnp.where(i >= start, ref[...], x)
        start += nb
    return x


def _inproj_kernel(*refs, splits, seg_blocks):
    n_seg = len(seg_blocks)
    w_ref = refs[n_seg]
    o_refs = refs[n_seg + 1:]
    x = _segment_rows(pl.program_id(0), refs[:n_seg], seg_blocks).astype(BF16)
    for o_ref, (lo, hi) in zip(o_refs, splits):
        o_ref[...] = jnp.dot(x, w_ref[:, lo:hi], preferred_element_type=F32).astype(o_ref.dtype)


def _inproj(xs, w, splits, dtypes):
    seg_blocks = _segment_blocks(xs, MM_TM)
    n = sum(seg_blocks) * MM_TM
    return pl.pallas_call(
        functools.partial(_inproj_kernel, splits=splits, seg_blocks=seg_blocks),
        grid=(n // MM_TM,),
        in_specs=_segment_specs(seg_blocks, MM_TM, D_MODEL) + [pl.BlockSpec(w.shape, lambda i: (0, 0))],
        out_specs=[pl.BlockSpec((MM_TM, hi - lo), lambda i: (i, 0)) for lo, hi in splits],
        out_shape=[jax.ShapeDtypeStruct((n, hi - lo), dt) for (lo, hi), dt in zip(splits, dtypes)],
        compiler_params=_cparams(("parallel",)),
        name="inproj",
    )(*xs, w)


def _layer_norm(z, g, b):
    mu = jnp.mean(z, axis=-1, keepdims=True)
    zc = z - mu
    var = jnp.mean(zc * zc, axis=-1, keepdims=True)
    return zc * lax.rsqrt(var + LN_EPS) * g + b


def _outproj_ln_kernel(*refs, n_parts, seg_blocks):
    n_seg = len(seg_blocks)
    o_refs = refs[:n_parts]
    w_refs = refs[n_parts:2 * n_parts]
    x_refs = refs[2 * n_parts:2 * n_parts + n_seg]
    g_ref, b_ref, wh_ref, wl_ref, br_ref, out_ref, info_ref, cnt_ref, run_scr, tri_scr = refs[2 * n_parts + n_seg:]
    i = pl.program_id(0)
    h = DEEPNORM_ALPHA * _segment_rows(i, x_refs, seg_blocks)
    for o_ref, w_ref in zip(o_refs, w_refs):
        h = h + jnp.dot(o_ref[...], w_ref[...], preferred_element_type=F32)
    out = _layer_norm(h, g_ref[...], b_ref[...])
    out_ref[...] = out
    _route(i, out, wh_ref, wl_ref, br_ref, info_ref, cnt_ref, run_scr, tri_scr)


def _outproj_ln(parts, weights, xs, g, b, w_router, b_router):
    tm = MM_TM
    seg_blocks = _segment_blocks(xs, tm)
    n = sum(seg_blocks) * tm
    n_parts = len(parts)
    wh = w_router.astype(BF16)
    wl = (w_router - wh.astype(F32)).astype(BF16)
    const = lambda shape: pl.BlockSpec(shape, lambda i: (0, 0))
    return pl.pallas_call(
        functools.partial(_outproj_ln_kernel, n_parts=n_parts, seg_blocks=seg_blocks),
        grid=(n // tm,),
        in_specs=([pl.BlockSpec((tm, p.shape[1]), lambda i: (i, 0)) for p in parts]
                  + [const(w.shape) for w in weights]
                  + _segment_specs(seg_blocks, tm, D_MODEL)
                  + [const((1, D_MODEL)), const((1, D_MODEL)),
                     const((D_MODEL, LANES)), const((D_MODEL, LANES)), const((1, LANES))]),
        out_specs=[pl.BlockSpec((tm, D_MODEL), lambda i: (i, 0)),
                   pl.BlockSpec((None, SUBLANES, tm), lambda i: (i, 0, 0)),
                   const((CLASS_ROWS, LANES))],
        out_shape=[jax.ShapeDtypeStruct((n, D_MODEL), F32),
                   jax.ShapeDtypeStruct((n // tm, SUBLANES, tm), F32),
                   jax.ShapeDtypeStruct((CLASS_ROWS, LANES), F32)],
        scratch_shapes=[pltpu.VMEM((CLASS_ROWS, LANES), F32), pltpu.VMEM((tm, tm), BF16)],
        compiler_params=_cparams(("arbitrary",)),
        name="outproj_ln",
    )(*parts, *weights, *xs, g.reshape(1, D_MODEL), b.reshape(1, D_MODEL), wh, wl, b_router)


ATT_DEPTH = 3


def _staged(n, weights, values):
    for i in range(min(ATT_DEPTH, n)):
        weights(i)
    for i in range(n):
        if i + ATT_DEPTH < n:
            weights(i + ATT_DEPTH)
        values(i)


def _attn_a_kernel(q_ref, kvm_ref, kvp_ref, kvn_ref, bias_ref, sink_ref, o_ref, kv_scr, p_scr,
                   *, seg_starts, seg_ends):
    w = A_WINDOW
    n_sub = ATT_TB // w
    tok0 = pl.program_id(0) * ATT_TB
    is_first, is_last = _segment_flags(tok0, ATT_TB, seg_starts, seg_ends)
    kv_scr[0:w, :] = kvp_ref[...]
    kv_scr[w:w + ATT_TB, :] = kvm_ref[...]
    kv_scr[w + ATT_TB:, :] = kvn_ref[...]
    low = lax.broadcasted_iota(jnp.int32, (1, LANES), 1) < HEAD_DIM
    ch = A_CHUNK
    per = w // ch

    sinks = [jnp.concatenate([jnp.full((ch, 1), sink_ref[c + 4 * g], F32) for c in range(4)], axis=0)
             for g in range(A_KV_HEADS)]

    def scores(t, g):
        j = t // per
        q = q_ref[t * ch:(t + 1) * ch, :]
        qg = jnp.concatenate([jnp.where(low, q[:, c * LANES:(c + 1) * LANES], 0) if g == 0
                              else jnp.where(low, 0, q[:, c * LANES:(c + 1) * LANES]) for c in range(4)], axis=0)
        k2 = kv_scr[j * w:(j + 3) * w, :LANES]
        return lax.dot_general(qg, k2, (((1,), (1,)), ((), ())), preferred_element_type=F32)

    tiles = [(t, g) for t in range(ATT_TB // ch) for g in range(A_KV_HEADS)]
    rows = 4 * ch
    rdens = {}
    outs = {}

    def weights(i):
        t, g = tiles[i]
        j = t // per
        if j == 0:
            variant = jnp.where(is_first, 1, 0)
        elif j == n_sub - 1:
            variant = jnp.where(is_last, 2, 0)
        else:
            variant = 0
        s = scores(t, g) + bias_ref[variant, g, t % per]
        m = jnp.maximum(jnp.max(s, axis=-1, keepdims=True), sinks[g])
        e = jnp.exp2(s - m)
        rdens[i] = 1.0 / (jnp.sum(e, axis=-1, keepdims=True) + jnp.exp2(sinks[g] - m))
        p_scr[i * rows:(i + 1) * rows, :] = e.astype(BF16)

    def values(i):
        t, g = tiles[i]
        j = t // per
        v2 = kv_scr[j * w:(j + 3) * w, LANES:]
        outs[g] = jnp.dot(p_scr[i * rows:(i + 1) * rows, :], v2, preferred_element_type=F32) * rdens[i]
        if g == A_KV_HEADS - 1:
            for c in range(4):
                oc = jnp.where(low, outs[0][c * ch:(c + 1) * ch], outs[1][c * ch:(c + 1) * ch])
                o_ref[t * ch:(t + 1) * ch, c * LANES:(c + 1) * LANES] = oc.astype(o_ref.dtype)

    _staged(len(tiles), weights, values)


A_CHUNK = 64


def _attn_a(a_qkv, bias, sink, seg_starts, seg_ends):
    n = a_qkv.shape[0]
    w = A_WINDOW
    sub = ATT_TB // w
    nhb = n // w
    kv_col = QA_W // (2 * LANES)
    return pl.pallas_call(
        functools.partial(_attn_a_kernel, seg_starts=seg_starts, seg_ends=seg_ends),
        grid=(n // ATT_TB,),
        in_specs=[pl.BlockSpec((ATT_TB, QA_W), lambda i: (i, 0)),
                  pl.BlockSpec((ATT_TB, 2 * LANES), lambda i: (i, kv_col)),
                  pl.BlockSpec((w, 2 * LANES), lambda i: (jnp.maximum(i * sub - 1, 0), kv_col)),
                  pl.BlockSpec((w, 2 * LANES), lambda i: (jnp.minimum((i + 1) * sub, nhb - 1), kv_col)),
                  pl.BlockSpec(bias.shape, lambda i: (0,) * bias.ndim),
                  pl.BlockSpec(memory_space=pltpu.SMEM)],
        out_specs=pl.BlockSpec((ATT_TB, QA_W), lambda i: (i, 0)),
        out_shape=jax.ShapeDtypeStruct((n, QA_W), BF16),
        scratch_shapes=[pltpu.VMEM((ATT_TB + 2 * w, 2 * LANES), BF16),
                        pltpu.VMEM((A_HEADS * ATT_TB, 3 * w), BF16)],
        compiler_params=_cparams(("parallel",)),
        name="attn_a",
    )(a_qkv, a_qkv, a_qkv, a_qkv, bias, sink)


def _attn_b_kernel(q_ref, kp_ref, km_ref, kn_ref, vp_ref, vm_ref, vn_ref, bias_ref, o_ref,
                   k_scr, v_scr, o_scr, m_scr, l_scr, p_scr, *, seg_starts, seg_ends):
    tb = ATT_TB
    h = B_HALF
    tok0 = pl.program_id(1) * tb
    is_first, is_last = _segment_flags(tok0, tb, seg_starts, seg_ends)
    k_scr[0:tb, :] = kp_ref[...]
    k_scr[tb:2 * tb, :] = km_ref[...]
    k_scr[2 * tb:, :] = kn_ref[...]
    v_scr[0:tb, :] = vp_ref[...]
    v_scr[tb:2 * tb, :] = vm_ref[...]
    v_scr[2 * tb:, :] = vn_ref[...]
    lane = lax.broadcasted_iota(jnp.int32, (1, LANES), 1)
    low = lane < HEAD_DIM
    col = lax.broadcasted_iota(jnp.int32, (1, 3 * h), 1)

    pen_first = jnp.where(jnp.logical_and(col < h, is_first), NEG_INF, 0.0)
    pen_last = jnp.where(jnp.logical_and(col >= 2 * h, is_last), NEG_INF, 0.0)

    def slices(d, r, b):
        row0 = r + h * d * b
        if d == 1:
            return pl.ds(row0, h), pl.ds(tb + row0 - h, 3 * h)
        return pl.ds(row0, h, stride=d), pl.ds(tb + row0 - h * d, 3 * h, stride=d)

    def scores(br, d, r, b):
        qs, ks = slices(d, r, b)
        q = q_ref[qs, :].astype(BF16)
        k = k_scr[ks, :].astype(BF16)
        qq = jnp.concatenate([jnp.where(low, q, 0), jnp.where(low, 0, q)], axis=0)
        return lax.dot_general(qq, k, (((1,), (1,)), ((), ())), preferred_element_type=F32)

    tiles = [(br, d, r, b) for br, (_, d) in enumerate(B_BRANCHES)
             for r in range(d) for b in range(tb // (h * d))]

    def weights(i):
        br, d, r, b = tiles[i]
        qs, _ = slices(d, r, b)
        s = scores(br, d, r, b) + bias_ref[br]
        if b == 0:
            s = s + pen_first
        if b == tb // (h * d) - 1:
            s = s + pen_last
        m = jnp.max(s, axis=-1, keepdims=True)
        e = jnp.exp2(s - m)
        l = jnp.sum(e, axis=-1, keepdims=True)
        p_scr[i * 2 * h:(i + 1) * 2 * h, :] = e.astype(BF16)
        m_scr[br, qs, :] = jnp.where(low, m[:h], m[h:])
        l_scr[br, qs, :] = jnp.where(low, l[:h], l[h:])

    def values(i):
        br, d, r, b = tiles[i]
        qs, ks = slices(d, r, b)
        v = v_scr[ks, :].astype(BF16)
        pv = jnp.dot(p_scr[i * 2 * h:(i + 1) * 2 * h, :], v, preferred_element_type=F32)
        o_scr[br, qs, :] = jnp.where(low, pv[:h], pv[h:])

    _staged(len(tiles), weights, values)

    m_all = jnp.maximum(jnp.maximum(m_scr[0], m_scr[1]), m_scr[2])
    num = jnp.zeros((tb, LANES), F32)
    den = jnp.zeros((tb, LANES), F32)
    for br in range(len(B_BRANCHES)):
        a = jnp.exp2(m_scr[br] - m_all)
        num = num + a * o_scr[br]
        den = den + a * l_scr[br]
    o_ref[...] = (num / den).astype(o_ref.dtype)


def _attn_b(b_qkv, bias, seg_starts, seg_ends):
    n = b_qkv.shape[0]
    tb = ATT_TB
    nblk = n // tb
    npair = B_HEADS // 2
    prev = lambda i: jnp.maximum(i - 1, 0)
    nxt = lambda i: jnp.minimum(i + 1, nblk - 1)
    blk = lambda rowf, off: pl.BlockSpec((tb, LANES), lambda c, i: (rowf(i), off + c))
    same = lambda i: i
    stat = pltpu.VMEM((len(B_BRANCHES), tb, LANES), F32)
    return pl.pallas_call(
        functools.partial(_attn_b_kernel, seg_starts=seg_starts, seg_ends=seg_ends),
        grid=(npair, nblk),
        in_specs=[blk(same, 0),
                  blk(prev, npair), blk(same, npair), blk(nxt, npair),
                  blk(prev, 2 * npair), blk(same, 2 * npair), blk(nxt, 2 * npair),
                  pl.BlockSpec((None,) + bias.shape[1:], lambda c, i: (c, 0, 0, 0))],
        out_specs=pl.BlockSpec((tb, LANES), lambda c, i: (i, c)),
        out_shape=jax.ShapeDtypeStruct((n, B_W), BF16),
        scratch_shapes=[pltpu.VMEM((3 * tb, LANES), F32), pltpu.VMEM((3 * tb, LANES), F32),
                        stat, stat, stat,
                        pltpu.VMEM((len(B_BRANCHES) * 2 * tb, 3 * B_HALF), BF16)],
        compiler_params=_cparams(("parallel", "parallel")),
        name="attn_b",
    )(b_qkv, b_qkv, b_qkv, b_qkv, b_qkv, b_qkv, b_qkv, bias)


C_HALO = (NA_ROWS // 2) * GRID_W

def _attn_c_kernel(q_ref, kp_ref, km_ref, kn_ref, vp_ref, vm_ref, vn_ref, bias_ref, o_ref,
                   k_scr, v_scr, p_scr, *, seg_starts, seg_ends):
    tb = ATT_TB
    gw = GRID_W
    nkeys = NA_ROWS * gw
    tok0 = pl.program_id(1) * tb
    k_scr[0:C_HALO, :] = kp_ref[...]
    k_scr[C_HALO:C_HALO + tb, :] = km_ref[...]
    k_scr[C_HALO + tb:, :] = kn_ref[...]
    v_scr[0:C_HALO, :] = vp_ref[...]
    v_scr[C_HALO:C_HALO + tb, :] = vm_ref[...]
    v_scr[C_HALO + tb:, :] = vn_ref[...]
    seg_row0 = jnp.int32(0)
    seg_rows = jnp.int32(0)
    for s, e in zip(seg_starts, seg_ends):
        inside = jnp.logical_and(tok0 >= s, tok0 < e)
        seg_row0 = jnp.where(inside, s // gw, seg_row0)
        seg_rows = jnp.where(inside, (e - s) // gw, seg_rows)
    lane = lax.broadcasted_iota(jnp.int32, (1, LANES), 1)
    low = lane < HEAD_DIM

    def window(rr):
        rs = tok0 // gw + rr - seg_row0
        start = jnp.clip(rs - NA_ROWS // 2, 0, seg_rows - NA_ROWS)
        shift = rs - start
        return shift, pl.ds(pl.multiple_of((rr + NA_ROWS // 2 - shift) * gw, gw), nkeys)

    def scores(rr):
        _, ks = window(rr)
        q = q_ref[rr * gw:(rr + 1) * gw, :]
        qq = jnp.concatenate([jnp.where(low, q, 0), jnp.where(low, 0, q)], axis=0)
        return lax.dot_general(qq, k_scr[ks, :], (((1,), (1,)), ((), ())), preferred_element_type=F32)

    n_rows = tb // gw
    rdens = {}

    def weights(rr):
        shift, _ = window(rr)
        s = scores(rr) + bias_ref[shift]
        m = jnp.max(s, axis=-1, keepdims=True)
        e = jnp.exp2(s - m)
        rdens[rr] = 1.0 / jnp.sum(e, axis=-1, keepdims=True)
        p_scr[rr * 2 * gw:(rr + 1) * 2 * gw, :] = e.astype(BF16)

    def values(rr):
        _, ks = window(rr)
        pv = jnp.dot(p_scr[rr * 2 * gw:(rr + 1) * 2 * gw, :], v_scr[ks, :], preferred_element_type=F32) * rdens[rr]
        o_ref[rr * gw:(rr + 1) * gw, :] = jnp.where(low, pv[:gw], pv[gw:]).astype(o_ref.dtype)

    _staged(n_rows, weights, values)


def _attn_c(c_qkv, bias, seg_starts, seg_ends):
    n = c_qkv.shape[0]
    tb = ATT_TB
    npair = C_HEADS // 2
    sub = tb // C_HALO
    nhb = n // C_HALO
    main = lambda off: pl.BlockSpec((tb, LANES), lambda c, i: (i, off + c))
    prev = lambda off: pl.BlockSpec((C_HALO, LANES), lambda c, i: (jnp.maximum(i * sub - 1, 0), off + c))
    nxt = lambda off: pl.BlockSpec((C_HALO, LANES),
                                   lambda c, i: (jnp.minimum((i + 1) * sub, nhb - 1), off + c))
    return pl.pallas_call(
        functools.partial(_attn_c_kernel, seg_starts=seg_starts, seg_ends=seg_ends),
        grid=(npair, n // tb),
        in_specs=[main(0),
                  prev(npair), main(npair), nxt(npair),
                  prev(2 * npair), main(2 * npair), nxt(2 * npair),
                  pl.BlockSpec((None,) + bias.shape[1:], lambda c, i: (c, 0, 0, 0))],
        out_specs=pl.BlockSpec((tb, LANES), lambda c, i: (i, c)),
        out_shape=jax.ShapeDtypeStruct((n, C_W), BF16),
        scratch_shapes=[pltpu.VMEM((tb + 2 * C_HALO, LANES), BF16),
                        pltpu.VMEM((tb + 2 * C_HALO, LANES), BF16),
                        pltpu.VMEM((2 * tb, NA_ROWS * GRID_W), BF16)],
        compiler_params=_cparams(("parallel", "parallel")),
        name="attn_c",
    )(c_qkv, c_qkv, c_qkv, c_qkv, c_qkv, c_qkv, c_qkv, bias)


def _route(i, x, wh_ref, wl_ref, b_ref, info_ref, cnt_ref, run_scr, tri_scr):
    tm = x.shape[0]

    @pl.when(i == 0)
    def _():
        run_scr[...] = jnp.zeros_like(run_scr)

    xh = x.astype(BF16)
    xl = (x - xh.astype(F32)).astype(BF16)
    logits = (jnp.dot(xh, wh_ref[...], preferred_element_type=F32)
              + jnp.dot(xl, wh_ref[...], preferred_element_type=F32)
              + jnp.dot(xh, wl_ref[...], preferred_element_type=F32)) + b_ref[...]
    lt = logits.T
    first = lambda hit, n: jnp.min(jnp.where(hit, lax.broadcasted_iota(jnp.int32, (n, tm), 0).astype(F32),
                                             float(n)), axis=0, keepdims=True)
    lg = lt[0:N_GROUPS]
    g_sel = first(lg == jnp.max(lg, axis=0, keepdims=True), N_GROUPS)
    le = jnp.zeros((EXPERTS_PER_GROUP, tm), F32)
    for g in range(N_GROUPS):
        lo = N_GROUPS + g * EXPERTS_PER_GROUP
        le = jnp.where(g_sel == g, lt[lo:lo + EXPERTS_PER_GROUP], le)
    row = lax.broadcasted_iota(jnp.int32, (EXPERTS_PER_GROUP, tm), 0).astype(F32)
    i1 = first(le == jnp.max(le, axis=0, keepdims=True), EXPERTS_PER_GROUP)
    rest = jnp.where(row == i1, NEG_INF, le)
    i2 = first(jnp.logical_and(rest == jnp.max(rest, axis=0, keepdims=True), row != i1), EXPERTS_PER_GROUP)
    a = jnp.minimum(i1, i2)
    b = jnp.maximum(i1, i2)
    cls = g_sel * N_PAIRS + a * 3.0 - jnp.where(a == 2.0, 1.0, 0.0) + (b - a - 1.0)

    @pl.when(i == 0)
    def _():
        tri_scr[...] = (lax.broadcasted_iota(jnp.int32, (tm, tm), 0)
                        < lax.broadcasted_iota(jnp.int32, (tm, tm), 1)).astype(BF16)

    onehot = lax.broadcasted_iota(jnp.int32, (CLASS_ROWS, tm), 0).astype(F32) == cls
    before = jnp.dot(onehot.astype(BF16), tri_scr[...], preferred_element_type=F32) + run_scr[:, 0:1]
    rank = jnp.sum(jnp.where(onehot, before, 0.0), axis=0, keepdims=True)
    run_scr[...] = run_scr[...] + jnp.sum(onehot.astype(F32), axis=1, keepdims=True)
    srow = lax.broadcasted_iota(jnp.int32, (SUBLANES, tm), 0)
    info_ref[...] = jnp.where(srow == 0, cls, jnp.where(srow == 1, rank, 0.0))
    cnt_ref[...] = run_scr[...]


CLASS_ROWS = 32
SUBLANES = 8


def _tile_copy(src, src_tok, dst, dst_tok, sem):
    return pltpu.make_async_copy(src.at[pl.ds(pl.multiple_of(src_tok * SUBLANES, SUBLANES), SUBLANES), :],
                                 dst.at[pl.ds(pl.multiple_of(dst_tok * SUBLANES, SUBLANES), SUBLANES), :], sem)


ROW_UNROLL = 8


def _start_rows(copy, n):
    def body(g, carry):
        for u in range(ROW_UNROLL):
            copy(g * ROW_UNROLL + u).start(priority=u % 2)
        return carry

    lax.fori_loop(0, n // ROW_UNROLL, body, 0)


def _dispatch_kernel(dest_ref, pad_ref, x_ref, xs_ref, rec_scr, zero_scr, sems, zsem):
    i = pl.program_id(0)
    last = pl.num_programs(0) - 1
    tm = x_ref.shape[0]
    rows = tm * SUBLANES
    slot = i % 2
    tile_rows = MOE_TM * SUBLANES

    def zero_copy(c):
        start = pl.multiple_of(pad_ref[c] * SUBLANES, tile_rows)
        return pltpu.make_async_copy(zero_scr, xs_ref.at[pl.ds(start, tile_rows), :], zsem)

    @pl.when(i == 0)
    def _():
        zero_scr[...] = jnp.zeros_like(zero_scr)
        for c in range(pad_ref.shape[0]):
            @pl.when(pad_ref[c] >= 0)
            def _():
                zero_copy(c).start()
        for c in range(pad_ref.shape[0]):
            @pl.when(pad_ref[c] >= 0)
            def _():
                zero_copy(c).wait()

    for j in range(D_MODEL // LANES):
        rec_scr[slot, pl.ds(j, tm, stride=SUBLANES), :] = x_ref[:, j * LANES:(j + 1) * LANES]

    _start_rows(lambda r: _tile_copy(rec_scr.at[slot], r, xs_ref, dest_ref[i * tm + r], sems.at[slot]), tm)

    def wait_step(s):
        pltpu.make_async_copy(rec_scr.at[s], xs_ref.at[pl.ds(0, rows), :], sems.at[s]).wait()

    @pl.when(i > 0)
    def _():
        wait_step(1 - slot)

    @pl.when(i == last)
    def _():
        wait_step(slot)


def _dispatch(dest, pad_start, x, n_sorted):
    n = x.shape[0]
    tm = ROW_TM
    return pl.pallas_call(
        _dispatch_kernel,
        grid_spec=pltpu.PrefetchScalarGridSpec(
            num_scalar_prefetch=2,
            grid=(n // tm,),
            in_specs=[pl.BlockSpec((tm, D_MODEL), lambda i, d, p: (i, 0))],
            out_specs=pl.BlockSpec(memory_space=pl.ANY),
            scratch_shapes=[pltpu.VMEM((2, tm * SUBLANES, LANES), F32),
                            pltpu.VMEM((MOE_TM * SUBLANES, LANES), F32),
                            pltpu.SemaphoreType.DMA((2,)), pltpu.SemaphoreType.DMA]),
        out_shape=jax.ShapeDtypeStruct((n_sorted * SUBLANES, LANES), F32),
        compiler_params=_cparams(("arbitrary",)),
        name="dispatch",
    )(dest, pad_start, x)


def _expert_kernel(ea_ref, eb_ref, nt_ref, xs_ref, wr_ref, br_ref,
                   wga_ref, wua_ref, wda_ref, wgb_ref, wub_ref, wdb_ref, g_ref, b_ref, ys_ref, hid_scr):
    p = pl.program_id(0)
    tm = MOE_TM

    @pl.when(p >= nt_ref[0])
    def _():
        ys_ref[...] = jnp.zeros_like(ys_ref)

    @pl.when(p < nt_ref[0])
    def _():
        x = jnp.concatenate([xs_ref[pl.ds(j, tm, stride=SUBLANES), :] for j in range(D_MODEL // LANES)],
                            axis=1)
        xb = x.astype(BF16)

        logits = jnp.dot(xb, wr_ref[...], preferred_element_type=F32) + br_ref[...]
        lane = lax.broadcasted_iota(jnp.int32, (tm, LANES), 1)
        lane_a = N_GROUPS + ea_ref[p]
        lane_b = N_GROUPS + eb_ref[p]
        grp = ea_ref[p] // EXPERTS_PER_GROUP
        is_g = lane < N_GROUPS
        mg = jnp.max(jnp.where(is_g, logits, NEG_INF), axis=-1, keepdims=True)
        eg = jnp.where(is_g, jnp.exp(logits - mg), 0.0)
        g_gate = (jnp.sum(jnp.where(lane == grp, eg, 0.0), axis=-1, keepdims=True)
                  / jnp.sum(eg, axis=-1, keepdims=True))
        l_a = jnp.sum(jnp.where(lane == lane_a, logits, 0.0), axis=-1, keepdims=True)
        l_b = jnp.sum(jnp.where(lane == lane_b, logits, 0.0), axis=-1, keepdims=True)
        mx = jnp.maximum(l_a, l_b)
        p_a = jnp.exp(l_a - mx)
        p_b = jnp.exp(l_b - mx)
        scale = g_gate / (p_a + p_b)
        w_a = p_a * scale
        w_b = p_b * scale

        for e, (wg_ref, wu_ref) in enumerate(((wga_ref, wua_ref), (wgb_ref, wub_ref))):
            gate = jnp.dot(xb, wg_ref[...], preferred_element_type=F32)
            up = jnp.dot(xb, wu_ref[...], preferred_element_type=F32)
            hid_scr[e] = (gate * (1.0 / (1.0 + jnp.exp(-gate))) * up).astype(BF16)
        y = (w_a * jnp.dot(hid_scr[0], wda_ref[...], preferred_element_type=F32)
             + w_b * jnp.dot(hid_scr[1], wdb_ref[...], preferred_element_type=F32))
        out = _layer_norm(DEEPNORM_ALPHA * x + y, g_ref[...], b_ref[...])
        for j in range(D_MODEL // LANES):
            ys_ref[pl.ds(j, tm, stride=SUBLANES), :] = out[:, j * LANES:(j + 1) * LANES]


def _experts(layer, tile_ea, tile_eb, n_tiles, xs, w_router, b_router, w_gate, w_up, w_down, g, b):
    tm = MOE_TM
    rows = tm * SUBLANES
    n_grid = xs.shape[0] // rows
    last = lambda p, nt: jnp.maximum(jnp.minimum(p, nt[0] - 1), 0)
    wspec_a = lambda shape: pl.BlockSpec((None, None) + shape, lambda p, ea, eb, nt: (layer, ea[p], 0, 0))
    wspec_b = lambda shape: pl.BlockSpec((None, None) + shape, lambda p, ea, eb, nt: (layer, eb[p], 0, 0))
    const = lambda shape: pl.BlockSpec(shape, lambda p, ea, eb, nt: (0, 0))
    up_shape = (D_MODEL, D_EXPERT)
    dn_shape = (D_EXPERT, D_MODEL)
    return pl.pallas_call(
        _expert_kernel,
        grid_spec=pltpu.PrefetchScalarGridSpec(
            num_scalar_prefetch=3,
            grid=(n_grid,),
            in_specs=[pl.BlockSpec((rows, LANES), lambda p, ea, eb, nt: (last(p, nt), 0)),
                      const((D_MODEL, LANES)), const((1, LANES)),
                      wspec_a(up_shape), wspec_a(up_shape), wspec_a(dn_shape),
                      wspec_b(up_shape), wspec_b(up_shape), wspec_b(dn_shape),
                      const((1, D_MODEL)), const((1, D_MODEL))],
            out_specs=pl.BlockSpec((rows, LANES), lambda p, ea, eb, nt: (p, 0)),
            scratch_shapes=[pltpu.VMEM((2, tm, D_EXPERT), BF16)]),
        out_shape=jax.ShapeDtypeStruct(xs.shape, F32),
        compiler_params=_cparams(("arbitrary",)),
        name="experts",
    )(tile_ea, tile_eb, n_tiles, xs, w_router.astype(BF16), b_router,
      w_gate, w_up, w_down, w_gate, w_up, w_down, g.reshape(1, D_MODEL), b.reshape(1, D_MODEL))


def _gather_kernel(dest_ref, ys_ref, *refs, seg_blocks):
    out_refs = refs[:len(seg_blocks)]
    rec_scr, sems = refs[len(seg_blocks):]
    i = pl.program_id(0)
    n_steps = pl.num_programs(0)
    tm = out_refs[0].shape[0]
    rows = tm * SUBLANES
    slot = i % 2

    def fetch(s):
        _start_rows(lambda r: _tile_copy(ys_ref, dest_ref[s * tm + r], rec_scr.at[s % 2], r, sems.at[s % 2]), tm)

    @pl.when(i == 0)
    def _():
        fetch(i)

    @pl.when(i + 1 < n_steps)
    def _():
        fetch(i + 1)

    pltpu.make_async_copy(ys_ref.at[pl.ds(0, rows), :], rec_scr.at[slot], sems.at[slot]).wait()

    start = 0
    for out_ref, nb in zip(out_refs, seg_blocks):
        @pl.when(jnp.logical_and(i >= start, i < start + nb))
        def _(out_ref=out_ref):
            for j in range(D_MODEL // LANES):
                out_ref[:, j * LANES:(j + 1) * LANES] = rec_scr[slot, pl.ds(j, tm, stride=SUBLANES), :]
        start += nb


def _gather_rows(dest, ys, seg_rows):
    tm = ROW_TM
    seg_blocks = tuple(r // tm for r in seg_rows)
    return pl.pallas_call(
        functools.partial(_gather_kernel, seg_blocks=seg_blocks),
        grid_spec=pltpu.PrefetchScalarGridSpec(
            num_scalar_prefetch=1,
            grid=(sum(seg_blocks),),
            in_specs=[pl.BlockSpec(memory_space=pl.ANY)],
            out_specs=_segment_specs(seg_blocks, tm, D_MODEL),
            scratch_shapes=[pltpu.VMEM((2, tm * SUBLANES, LANES), F32), pltpu.SemaphoreType.DMA((2,))]),
        out_shape=[jax.ShapeDtypeStruct((r, D_MODEL), F32) for r in seg_rows],
        compiler_params=_cparams(("arbitrary",)),
        name="gather_rows",
    )(dest, ys)


_PAIR_A = np.array([0, 0, 0, 1, 1, 2], np.int32)
_PAIR_B = np.array([1, 2, 3, 2, 3, 3], np.int32)


def _moe_layer(layer, x, info, counts, w_router, b_router, w_gate, w_up, w_down, g, b, out_rows):
    n = x.shape[0]
    tm = MOE_TM
    n_sorted = n + N_CLASSES * tm
    cls = info[:, 0, :].reshape(n).astype(jnp.int32)
    rank = info[:, 1, :].reshape(n).astype(jnp.int32)
    counts = counts[:N_CLASSES, 0].astype(jnp.int32)
    padded = (counts + tm - 1) // tm * tm
    classes = jnp.arange(N_CLASSES, dtype=jnp.int32)
    ends = jnp.sum(jnp.where(classes[None, :] <= classes[:, None], padded[None, :], 0), axis=1)
    offs = ends - padded
    total = ends[N_CLASSES - 1]
    dest = rank + jnp.sum(jnp.where(cls[:, None] == classes[None, :], offs[None, :], 0), axis=1)
    unused = total + classes * tm
    pad_start = jnp.concatenate([jnp.where(padded > 0, ends - tm, -1),
                                 jnp.where(unused < n_sorted, unused, -1)]).astype(jnp.int32)
    tile_start = jnp.arange(n_sorted // tm, dtype=jnp.int32) * tm
    tile_start = jnp.minimum(tile_start, total - tm)
    tile_cls = jnp.sum((ends[None, :] <= tile_start[:, None]).astype(jnp.int32), axis=1)
    pair = tile_cls % N_PAIRS
    pair_a = jnp.sum(jnp.where(pair[:, None] == np.arange(N_PAIRS)[None, :], _PAIR_A[None, :], 0), axis=1)
    pair_b = jnp.sum(jnp.where(pair[:, None] == np.arange(N_PAIRS)[None, :], _PAIR_B[None, :], 0), axis=1)
    grp = tile_cls // N_PAIRS
    tile_ea = (grp * EXPERTS_PER_GROUP + pair_a).astype(jnp.int32)
    tile_eb = (grp * EXPERTS_PER_GROUP + pair_b).astype(jnp.int32)
    n_tiles = (total // tm).astype(jnp.int32).reshape(1)
    xs = _dispatch(dest.astype(jnp.int32), pad_start, x, n_sorted)
    ys = _experts(layer, tile_ea, tile_eb, n_tiles, xs, w_router, b_router, w_gate, w_up, w_down, g, b)
    return _gather_rows(dest.astype(jnp.int32), ys, out_rows)


_A_ORDER = np.array([0, 4, 1, 5, 2, 6, 3, 7])


def _prep_ab(w_in, w_out):
    qa = w_in[:, :QA_W].reshape(D_MODEL, A_HEADS, HEAD_DIM)[:, _A_ORDER].reshape(D_MODEL, QA_W) * ATTN_SCALE
    kva = w_in[:, QA_W:A_IN]
    qb = w_in[:, A_IN:A_IN + B_W] * ATTN_SCALE
    kvb = w_in[:, A_IN + B_W:]
    w = jnp.concatenate([qa, kva, qb, kvb], axis=1).astype(BF16)
    wo_a = w_out[:QA_W].reshape(A_HEADS, HEAD_DIM, D_MODEL)[_A_ORDER].reshape(QA_W, D_MODEL).astype(BF16)
    wo_b = w_out[QA_W:].astype(BF16)
    return w, wo_a, wo_b


def _prep_c(w_in, w_out):
    w = jnp.concatenate([w_in[:, :C_W] * ATTN_SCALE, w_in[:, C_W:]], axis=1).astype(BF16)
    return w, w_out.astype(BF16)


def _trunk(xs, seg_starts, seg_ends, rel_bias, w_in_ab, a_sink, w_out_ab, w_in_c, c_rpb, w_out_c,
           ln1_g, ln1_b, ln2_g, ln2_b, router_g_w, router_g_b, router_e_w, router_e_b,
           w_gate, w_up, w_down):
    io_rows = tuple(a.shape[0] for a in xs)
    n = sum(io_rows)
    bias_a = _bias_a(rel_bias)
    bias_b = _bias_b(rel_bias)
    w_gate = w_gate.astype(BF16)
    w_up = w_up.astype(BF16)
    w_down = w_down.astype(BF16)
    for l in range(DEPTH):
        i = l // 2
        pad = LANES - N_GROUPS - N_EXPERTS
        w_router = jnp.pad(jnp.concatenate([router_g_w[l], router_e_w[l]], axis=1), ((0, 0), (0, pad)))
        b_router = jnp.pad(jnp.concatenate([router_g_b[l], router_e_b[l]]), (0, pad)).reshape(1, LANES)
        if l % 2 == 0:
            w, wo_a, wo_b = _prep_ab(w_in_ab[i], w_out_ab[i])
            a_qkv, b_qkv = _inproj(xs, w, ((0, A_IN), (A_IN, A_IN + B_IN)), (BF16, F32))
            o_a = _attn_a(a_qkv, bias_a, a_sink[i].astype(F32) * LOG2E, seg_starts, seg_ends)
            o_b = _attn_b(b_qkv, bias_b, seg_starts, seg_ends)
            parts, weights = [o_a, o_b], [wo_a, wo_b]
        else:
            w, wo = _prep_c(w_in_c[i], w_out_c[i])
            (c_qkv,) = _inproj(xs, w, ((0, 3 * C_W),), (BF16,))
            parts, weights = [_attn_c(c_qkv, _bias_c(c_rpb[i]), seg_starts, seg_ends)], [wo]
        x, info, counts = _outproj_ln(parts, weights, xs, ln1_g[l], ln1_b[l], w_router, b_router)
        xs = _moe_layer(l, x, info, counts, w_router, b_router, w_gate, w_up, w_down, ln2_g[l], ln2_b[l],
                        io_rows if l == DEPTH - 1 else (n,))
    return xs


def kernel(x_prompt, x_sample, rel_bias, w_in_ab, a_sink, w_out_ab, w_in_c, c_rpb, w_out_c,
           ln1_g, ln1_b, ln2_g, ln2_b, router_g_w, router_g_b, router_e_w, router_e_b,
           w_gate, w_up, w_down):
    lens = [x_prompt.shape[1]] * x_prompt.shape[0] + [x_sample.shape[1]] * x_sample.shape[0]
    seg_ends = tuple(int(v) for v in np.cumsum(lens))
    seg_starts = tuple(e - n for e, n in zip(seg_ends, lens))
    for n in lens:
        assert n % ATT_TB == 0 and n // GRID_W >= NA_ROWS
    xs = [x_prompt.reshape(-1, D_MODEL), x_sample.reshape(-1, D_MODEL)]
    y_p, y_s = _trunk(xs, seg_starts, seg_ends, rel_bias, w_in_ab, a_sink, w_out_ab, w_in_c, c_rpb, w_out_c,
                      ln1_g, ln1_b, ln2_g, ln2_b, router_g_w, router_g_b, router_e_w, router_e_b,
                      w_gate, w_up, w_down)
    return (y_p.reshape(x_prompt.shape), y_s.reshape(x_sample.shape))
```

```python
import functools
import math

import numpy as np
import jax
import jax.numpy as jnp
from jax import lax
from jax.experimental import pallas as pl
from jax.experimental.pallas import tpu as pltpu

F32 = jnp.float32
BF16 = jnp.bfloat16

D_MODEL = 1024
DEPTH = 4
HEAD_DIM = 64
LANES = 128
A_HEADS = 8
A_KV_HEADS = 2
A_WINDOW = 128
B_HEADS = 8
B_BRANCHES = ((128, 1), (512, 4), (2048, 16))
B_HALF = 64
C_HEADS = 16
GRID_W = 64
NA_ROWS = 8
NA_COLS = 16
REL_BUCKETS = 32
REL_MAX_DIST = 1024
N_GROUPS = 4
EXPERTS_PER_GROUP = 4
N_EXPERTS = 16
D_EXPERT = 512
N_PAIRS = 6
N_CLASSES = N_GROUPS * N_PAIRS
DEEPNORM_ALPHA = (2.0 * DEPTH) ** 0.25
LN_EPS = 1e-5
LOG2E = math.log2(math.e)
ATTN_SCALE = HEAD_DIM ** -0.5 * LOG2E
NEG_INF = -1e30

QA_W = A_HEADS * HEAD_DIM
KVA_W = A_KV_HEADS * HEAD_DIM
A_IN = QA_W + 2 * KVA_W
B_W = B_HEADS * HEAD_DIM
B_IN = 3 * B_W
C_W = C_HEADS * HEAD_DIM

ATT_TB = 1024
MM_TM = 512
MOE_TM = 256
ROW_TM = 512
VMEM_LIMIT = 56 * 1024 * 1024


def _cparams(sem):
    return pltpu.CompilerParams(dimension_semantics=sem, vmem_limit_bytes=VMEM_LIMIT)


def _segment_flags(tok0, size, seg_starts, seg_ends):
    is_first = functools.reduce(jnp.logical_or, [tok0 == s for s in seg_starts])
    is_last = functools.reduce(jnp.logical_or, [tok0 + size == e for e in seg_ends])
    return is_first, is_last


def _t5_bucket_np(rel):
    half_b = REL_BUCKETS // 2
    max_exact = half_b // 2
    n = np.abs(rel)
    large = max_exact + (np.log(np.maximum(n, max_exact).astype(np.float32) / max_exact)
                         / math.log(REL_MAX_DIST / max_exact) * (half_b - max_exact)).astype(np.int32)
    large = np.minimum(large, half_b - 1)
    return np.where(rel > 0, half_b, 0) + np.where(n < max_exact, n, large)


def _banded_bias(table, half, dil):
    rel = np.arange(3 * half)[None, :] - half - np.arange(half)[:, None]
    bucket = jnp.asarray(_t5_bucket_np(rel * dil).astype(np.int32))
    hit = bucket[None] == jnp.arange(REL_BUCKETS, dtype=jnp.int32)[:, None, None]
    bias = jnp.sum(jnp.where(hit[:, None], table.astype(F32)[:, :, None, None], 0.0), axis=0)
    return jnp.where(jnp.asarray(np.abs(rel) <= half)[None], bias * LOG2E, NEG_INF)


def _bias_a(rel_bias):
    w = A_WINDOW
    ch = A_CHUNK
    b = _banded_bias(rel_bias[:, :A_HEADS], w, 1).reshape(A_KV_HEADS, 4, w // ch, ch, 3 * w)
    b = b.transpose(0, 2, 1, 3, 4).reshape(A_KV_HEADS, w // ch, 4 * ch, 3 * w)
    col = np.arange(3 * w)
    first = jnp.where(jnp.asarray(col < w), NEG_INF, b)
    last = jnp.where(jnp.asarray(col >= 2 * w), NEG_INF, b)
    return jnp.stack([b, first, last])


def _bias_b(rel_bias):
    per = [_banded_bias(rel_bias[:, A_HEADS:], B_HALF, d) for _, d in B_BRANCHES]
    b = jnp.stack(per, axis=1)
    return b.reshape(B_HEADS // 2, 2, len(B_BRANCHES), B_HALF, 3 * B_HALF).transpose(0, 2, 1, 3, 4) \
            .reshape(B_HEADS // 2, len(B_BRANCHES), 2 * B_HALF, 3 * B_HALF)


def _bias_c(rpb):
    gw = GRID_W
    n_dr = 2 * NA_ROWS - 1
    side = gw - NA_COLS
    p = jnp.concatenate([jnp.repeat(rpb[..., :1], side, axis=-1), rpb.astype(F32),
                         jnp.repeat(rpb[..., -1:], side + 1, axis=-1)], axis=-1)
    z = jnp.broadcast_to(p[:, :, None, :], (C_HEADS, n_dr, gw, 2 * gw)).reshape(C_HEADS, n_dr, 2 * gw * gw)
    t = z[:, :, gw - 1:gw - 1 + gw * (2 * gw - 1)].reshape(C_HEADS, n_dr, gw, 2 * gw - 1)[..., :gw]
    cq = np.arange(gw)[:, None]
    w = np.arange(gw)[None, :]
    c0 = np.clip(cq - NA_COLS // 2, 0, gw - NA_COLS)
    t = jnp.where(jnp.asarray((w >= c0) & (w < c0 + NA_COLS)), t * LOG2E, NEG_INF)
    t = t.reshape(C_HEADS // 2, 2, n_dr, gw, gw).transpose(0, 1, 3, 2, 4)
    bias = jnp.stack([t[:, :, :, NA_ROWS - 1 - s:2 * NA_ROWS - 1 - s, :].reshape(C_HEADS // 2, 2, gw, NA_ROWS * gw)
                      for s in range(NA_ROWS)], axis=1)
    return bias.reshape(C_HEADS // 2, NA_ROWS, 2 * gw, NA_ROWS * gw)


def _segment_blocks(segs, tm):
    return tuple(a.shape[0] // tm for a in segs)


def _segment_specs(seg_blocks, tm, width):
    specs, start = [], 0
    for nb in seg_blocks:
        specs.append(pl.BlockSpec((tm, width), lambda i, *_, s=start, nb=nb: (jnp.clip(i - s, 0, nb - 1), 0)))
        start += nb
    return specs


def _segment_rows(i, refs, seg_blocks):
    x, start = refs[0][...], seg_blocks[0]
    for ref, nb in zip(refs[1:], seg_blocks[1:]):
        x = jnp.where(i >= start, ref[...], x)
        start += nb
    return x


def _inproj_kernel(*refs, splits, seg_blocks):
    n_seg = len(seg_blocks)
    w_ref = refs[n_seg]
    o_refs = refs[n_seg + 1:]
    x = _segment_rows(pl.program_id(0), refs[:n_seg], seg_blocks).astype(BF16)
    for o_ref, (lo, hi) in zip(o_refs, splits):
        o_ref[...] = jnp.dot(x, w_ref[:, lo:hi], preferred_element_type=F32).astype(o_ref.dtype)


def _inproj(xs, w, splits, dtypes):
    seg_blocks = _segment_blocks(xs, MM_TM)
    n = sum(seg_blocks) * MM_TM
    return pl.pallas_call(
        functools.partial(_inproj_kernel, splits=splits, seg_blocks=seg_blocks),
        grid=(n // MM_TM,),
        in_specs=_segment_specs(seg_blocks, MM_TM, D_MODEL) + [pl.BlockSpec(w.shape, lambda i: (0, 0))],
        out_specs=[pl.BlockSpec((MM_TM, hi - lo), lambda i: (i, 0)) for lo, hi in splits],
        out_shape=[jax.ShapeDtypeStruct((n, hi - lo), dt) for (lo, hi), dt in zip(splits, dtypes)],
        compiler_params=_cparams(("parallel",)),
        name="inproj",
    )(*xs, w)


def _layer_norm(z, g, b):
    mu = jnp.mean(z, axis=-1, keepdims=True)
    zc = z - mu
    var = jnp.mean(zc * zc, axis=-1, keepdims=True)
    return zc * lax.rsqrt(var + LN_EPS) * g + b


def _outproj_ln_kernel(*refs, n_parts, seg_blocks):
    n_seg = len(seg_blocks)
    o_refs = refs[:n_parts]
    w_refs = refs[n_parts:2 * n_parts]
    x_refs = refs[2 * n_parts:2 * n_parts + n_seg]
    g_ref, b_ref, whl_ref, br_ref, out_ref, info_ref, cnt_ref, run_scr, tri_scr = refs[2 * n_parts + n_seg:]
    i = pl.program_id(0)
    h = DEEPNORM_ALPHA * _segment_rows(i, x_refs, seg_blocks)
    for o_ref, w_ref in zip(o_refs, w_refs):
        h = h + jnp.dot(o_ref[...], w_ref[...], preferred_element_type=F32)
    out = _layer_norm(h, g_ref[...], b_ref[...])
    out_ref[...] = out
    _route(i, out, whl_ref, br_ref, info_ref, cnt_ref, run_scr, tri_scr)


def _outproj_ln(parts, weights, xs, g, b, w_router, b_router):
    tm = MM_TM
    seg_blocks = _segment_blocks(xs, tm)
    n = sum(seg_blocks) * tm
    n_parts = len(parts)
    wh = w_router.astype(BF16)
    whl = jnp.concatenate([wh, (w_router - wh.astype(F32)).astype(BF16)], axis=1)
    const = lambda shape: pl.BlockSpec(shape, lambda i: (0, 0))
    return pl.pallas_call(
        functools.partial(_outproj_ln_kernel, n_parts=n_parts, seg_blocks=seg_blocks),
        grid=(n // tm,),
        in_specs=([pl.BlockSpec((tm, p.shape[1]), lambda i: (i, 0)) for p in parts]
                  + [const(w.shape) for w in weights]
                  + _segment_specs(seg_blocks, tm, D_MODEL)
                  + [const((1, D_MODEL)), const((1, D_MODEL)),
                     const((D_MODEL, 2 * LANES)), const((1, LANES))]),
        out_specs=[pl.BlockSpec((tm, D_MODEL), lambda i: (i, 0)),
                   pl.BlockSpec((None, SUBLANES, tm), lambda i: (i, 0, 0)),
                   const((CLASS_ROWS, LANES))],
        out_shape=[jax.ShapeDtypeStruct((n, D_MODEL), F32),
                   jax.ShapeDtypeStruct((n // tm, SUBLANES, tm), F32),
                   jax.ShapeDtypeStruct((CLASS_ROWS, LANES), F32)],
        scratch_shapes=[pltpu.VMEM((CLASS_ROWS, LANES), F32), pltpu.VMEM((tm, tm), BF16)],
        compiler_params=_cparams(("arbitrary",)),
        name="outproj_ln",
    )(*parts, *weights, *xs, g.reshape(1, D_MODEL), b.reshape(1, D_MODEL), whl, b_router)


ATT_DEPTH = 3


def _staged(n, weights, values):
    for i in range(min(ATT_DEPTH, n)):
        weights(i)
    for i in range(n):
        if i + ATT_DEPTH < n:
            weights(i + ATT_DEPTH)
        values(i)


def _attn_a_kernel(q_ref, kvm_ref, kvp_ref, kvn_ref, bias_ref, sink_ref, o_ref, kv_scr, p_scr,
                   *, seg_starts, seg_ends):
    w = A_WINDOW
    tb = A_TB
    n_sub = tb // w
    tok0 = pl.program_id(0) * tb
    is_first, is_last = _segment_flags(tok0, tb, seg_starts, seg_ends)
    kv_scr[0:w, :] = kvp_ref[...]
    kv_scr[w:w + tb, :] = kvm_ref[...]
    kv_scr[w + tb:, :] = kvn_ref[...]
    low = lax.broadcasted_iota(jnp.int32, (1, LANES), 1) < HEAD_DIM
    ch = A_CHUNK
    per = w // ch

    sinks = [jnp.concatenate([jnp.full((ch, 1), sink_ref[c + 4 * g], F32) for c in range(4)], axis=0)
             for g in range(A_KV_HEADS)]

    def scores(t, g):
        j = t // per
        q = q_ref[t * ch:(t + 1) * ch, :]
        qg = jnp.concatenate([jnp.where(low, q[:, c * LANES:(c + 1) * LANES], 0) if g == 0
                              else jnp.where(low, 0, q[:, c * LANES:(c + 1) * LANES]) for c in range(4)], axis=0)
        k2 = kv_scr[j * w:(j + 3) * w, :LANES]
        return lax.dot_general(qg, k2, (((1,), (1,)), ((), ())), preferred_element_type=F32)

    tiles = [(t, g) for t in range(tb // ch) for g in range(A_KV_HEADS)]
    rows = 4 * ch
    rdens = {}
    outs = {}

    def weights(i):
        t, g = tiles[i]
        j = t // per
        if j == 0:
            variant = jnp.where(is_first, 1, 0)
        elif j == n_sub - 1:
            variant = jnp.where(is_last, 2, 0)
        else:
            variant = 0
        s = scores(t, g) + bias_ref[variant, g, t % per]
        m = jnp.maximum(jnp.max(s, axis=-1, keepdims=True), sinks[g])
        e = jnp.exp2(s - m)
        rdens[i] = 1.0 / (jnp.sum(e, axis=-1, keepdims=True) + jnp.exp2(sinks[g] - m))
        p_scr[i * rows:(i + 1) * rows, :] = e.astype(BF16)

    def values(i):
        t, g = tiles[i]
        j = t // per
        v2 = kv_scr[j * w:(j + 3) * w, LANES:]
        outs[g] = jnp.dot(p_scr[i * rows:(i + 1) * rows, :], v2, preferred_element_type=F32) * rdens[i]
        if g == A_KV_HEADS - 1:
            for c in range(4):
                oc = jnp.where(low, outs[0][c * ch:(c + 1) * ch], outs[1][c * ch:(c + 1) * ch])
                o_ref[t * ch:(t + 1) * ch, c * LANES:(c + 1) * LANES] = oc.astype(o_ref.dtype)

    _staged(len(tiles), weights, values)


A_CHUNK = 64
A_TB = 2 * ATT_TB


def _attn_a(a_qkv, bias, sink, seg_starts, seg_ends):
    n = a_qkv.shape[0]
    w = A_WINDOW
    tb = A_TB
    sub = tb // w
    nhb = n // w
    kv_col = QA_W // (2 * LANES)
    return pl.pallas_call(
        functools.partial(_attn_a_kernel, seg_starts=seg_starts, seg_ends=seg_ends),
        grid=(n // tb,),
        in_specs=[pl.BlockSpec((tb, QA_W), lambda i: (i, 0)),
                  pl.BlockSpec((tb, 2 * LANES), lambda i: (i, kv_col)),
                  pl.BlockSpec((w, 2 * LANES), lambda i: (jnp.maximum(i * sub - 1, 0), kv_col)),
                  pl.BlockSpec((w, 2 * LANES), lambda i: (jnp.minimum((i + 1) * sub, nhb - 1), kv_col)),
                  pl.BlockSpec(bias.shape, lambda i: (0,) * bias.ndim),
                  pl.BlockSpec(memory_space=pltpu.SMEM)],
        out_specs=pl.BlockSpec((tb, QA_W), lambda i: (i, 0)),
        out_shape=jax.ShapeDtypeStruct((n, QA_W), BF16),
        scratch_shapes=[pltpu.VMEM((tb + 2 * w, 2 * LANES), BF16),
                        pltpu.VMEM((A_HEADS * tb, 3 * w), BF16)],
        compiler_params=_cparams(("parallel",)),
        name="attn_a",
    )(a_qkv, a_qkv, a_qkv, a_qkv, bias, sink)


def _attn_b_kernel(q_ref, kp_ref, km_ref, kn_ref, vp_ref, vm_ref, vn_ref, bias_ref, o_ref,
                   k_scr, v_scr, o_scr, m_scr, l_scr, p_scr, *, seg_starts, seg_ends):
    tb = B_TB
    h = B_HALF
    tok0 = pl.program_id(1) * tb
    is_first, is_last = _segment_flags(tok0, tb, seg_starts, seg_ends)
    k_scr[0:tb, :] = kp_ref[...]
    k_scr[tb:2 * tb, :] = km_ref[...]
    k_scr[2 * tb:, :] = kn_ref[...]
    v_scr[0:tb, :] = vp_ref[...]
    v_scr[tb:2 * tb, :] = vm_ref[...]
    v_scr[2 * tb:, :] = vn_ref[...]
    lane = lax.broadcasted_iota(jnp.int32, (1, LANES), 1)
    low = lane < HEAD_DIM
    col = lax.broadcasted_iota(jnp.int32, (1, 3 * h), 1)

    pen_first = jnp.where(jnp.logical_and(col < h, is_first), NEG_INF, 0.0)
    pen_last = jnp.where(jnp.logical_and(col >= 2 * h, is_last), NEG_INF, 0.0)

    def slices(d, r, b):
        row0 = r + h * d * b
        if d == 1:
            return pl.ds(row0, h), pl.ds(tb + row0 - h, 3 * h)
        return pl.ds(row0, h, stride=d), pl.ds(tb + row0 - h * d, 3 * h, stride=d)

    def scores(br, d, r, b):
        qs, ks = slices(d, r, b)
        q = q_ref[qs, :].astype(BF16)
        k = k_scr[ks, :].astype(BF16)
        qq = jnp.concatenate([jnp.where(low, q, 0), jnp.where(low, 0, q)], axis=0)
        return lax.dot_general(qq, k, (((1,), (1,)), ((), ())), preferred_element_type=F32)

    tiles = [(br, d, r, b) for br, (_, d) in enumerate(B_BRANCHES)
             for r in range(d) for b in range(tb // (h * d))]

    def weights(i):
        br, d, r, b = tiles[i]
        qs, _ = slices(d, r, b)
        s = scores(br, d, r, b) + bias_ref[br]
        if b == 0:
            s = s + pen_first
        if b == tb // (h * d) - 1:
            s = s + pen_last
        m = jnp.max(s, axis=-1, keepdims=True)
        e = jnp.exp2(s - m)
        l = jnp.sum(e, axis=-1, keepdims=True)
        p_scr[i * 2 * h:(i + 1) * 2 * h, :] = e.astype(BF16)
        m_scr[br, qs, :] = jnp.where(low, m[:h], m[h:])
        l_scr[br, qs, :] = jnp.where(low, l[:h], l[h:])

    def values(i):
        br, d, r, b = tiles[i]
        qs, ks = slices(d, r, b)
        v = v_scr[ks, :].astype(BF16)
        pv = jnp.dot(p_scr[i * 2 * h:(i + 1) * 2 * h, :], v, preferred_element_type=F32)
        o_scr[br, qs, :] = jnp.where(low, pv[:h], pv[h:])

    _staged(len(tiles), weights, values)

    m_all = jnp.maximum(jnp.maximum(m_scr[0], m_scr[1]), m_scr[2])
    num = jnp.zeros((tb, LANES), F32)
    den = jnp.zeros((tb, LANES), F32)
    for br in range(len(B_BRANCHES)):
        a = jnp.exp2(m_scr[br] - m_all)
        num = num + a * o_scr[br]
        den = den + a * l_scr[br]
    o_ref[...] = (num / den).astype(o_ref.dtype)


B_TB = ATT_TB


def _attn_b(b_qkv, bias, seg_starts, seg_ends):
    n = b_qkv.shape[0]
    tb = B_TB
    nblk = n // tb
    npair = B_HEADS // 2
    prev = lambda i: jnp.maximum(i - 1, 0)
    nxt = lambda i: jnp.minimum(i + 1, nblk - 1)
    blk = lambda rowf, off: pl.BlockSpec((tb, LANES), lambda c, i: (rowf(i), off + c))
    same = lambda i: i
    stat = pltpu.VMEM((len(B_BRANCHES), tb, LANES), F32)
    return pl.pallas_call(
        functools.partial(_attn_b_kernel, seg_starts=seg_starts, seg_ends=seg_ends),
        grid=(npair, nblk),
        in_specs=[blk(same, 0),
                  blk(prev, npair), blk(same, npair), blk(nxt, npair),
                  blk(prev, 2 * npair), blk(same, 2 * npair), blk(nxt, 2 * npair),
                  pl.BlockSpec((None,) + bias.shape[1:], lambda c, i: (c, 0, 0, 0))],
        out_specs=pl.BlockSpec((tb, LANES), lambda c, i: (i, c)),
        out_shape=jax.ShapeDtypeStruct((n, B_W), BF16),
        scratch_shapes=[pltpu.VMEM((3 * tb, LANES), F32), pltpu.VMEM((3 * tb, LANES), F32),
                        stat, stat, stat,
                        pltpu.VMEM((len(B_BRANCHES) * 2 * tb, 3 * B_HALF), BF16)],
        compiler_params=_cparams(("parallel", "parallel")),
        name="attn_b",
    )(b_qkv, b_qkv, b_qkv, b_qkv, b_qkv, b_qkv, b_qkv, bias)


C_HALO = (NA_ROWS // 2) * GRID_W

def _attn_c_kernel(q_ref, kp_ref, km_ref, kn_ref, vp_ref, vm_ref, vn_ref, bias_ref, o_ref,
                   k_scr, v_scr, p_scr, *, seg_starts, seg_ends):
    tb = ATT_TB
    gw = GRID_W
    nkeys = NA_ROWS * gw
    tok0 = pl.program_id(1) * tb
    k_scr[0:C_HALO, :] = kp_ref[...]
    k_scr[C_HALO:C_HALO + tb, :] = km_ref[...]
    k_scr[C_HALO + tb:, :] = kn_ref[...]
    v_scr[0:C_HALO, :] = vp_ref[...]
    v_scr[C_HALO:C_HALO + tb, :] = vm_ref[...]
    v_scr[C_HALO + tb:, :] = vn_ref[...]
    seg_row0 = jnp.int32(0)
    seg_rows = jnp.int32(0)
    for s, e in zip(seg_starts, seg_ends):
        inside = jnp.logical_and(tok0 >= s, tok0 < e)
        seg_row0 = jnp.where(inside, s // gw, seg_row0)
        seg_rows = jnp.where(inside, (e - s) // gw, seg_rows)
    lane = lax.broadcasted_iota(jnp.int32, (1, LANES), 1)
    low = lane < HEAD_DIM

    def window(rr):
        rs = tok0 // gw + rr - seg_row0
        start = jnp.clip(rs - NA_ROWS // 2, 0, seg_rows - NA_ROWS)
        shift = rs - start
        return shift, pl.ds(pl.multiple_of((rr + NA_ROWS // 2 - shift) * gw, gw), nkeys)

    def scores(rr):
        _, ks = window(rr)
        q = q_ref[rr * gw:(rr + 1) * gw, :]
        qq = jnp.concatenate([jnp.where(low, q, 0), jnp.where(low, 0, q)], axis=0)
        return lax.dot_general(qq, k_scr[ks, :], (((1,), (1,)), ((), ())), preferred_element_type=F32)

    n_rows = tb // gw
    rdens = {}

    def weights(rr):
        shift, _ = window(rr)
        s = scores(rr) + bias_ref[shift]
        m = jnp.max(s, axis=-1, keepdims=True)
        e = jnp.exp2(s - m)
        rdens[rr] = 1.0 / jnp.sum(e, axis=-1, keepdims=True)
        p_scr[rr * 2 * gw:(rr + 1) * 2 * gw, :] = e.astype(BF16)

    def values(rr):
        _, ks = window(rr)
        pv = jnp.dot(p_scr[rr * 2 * gw:(rr + 1) * 2 * gw, :], v_scr[ks, :], preferred_element_type=F32) * rdens[rr]
        o_ref[rr * gw:(rr + 1) * gw, :] = jnp.where(low, pv[:gw], pv[gw:]).astype(o_ref.dtype)

    _staged(n_rows, weights, values)


def _attn_c(c_qkv, bias, seg_starts, seg_ends):
    n = c_qkv.shape[0]
    tb = ATT_TB
    npair = C_HEADS // 2
    sub = tb // C_HALO
    nhb = n // C_HALO
    main = lambda off: pl.BlockSpec((tb, LANES), lambda c, i: (i, off + c))
    prev = lambda off: pl.BlockSpec((C_HALO, LANES), lambda c, i: (jnp.maximum(i * sub - 1, 0), off + c))
    nxt = lambda off: pl.BlockSpec((C_HALO, LANES),
                                   lambda c, i: (jnp.minimum((i + 1) * sub, nhb - 1), off + c))
    return pl.pallas_call(
        functools.partial(_attn_c_kernel, seg_starts=seg_starts, seg_ends=seg_ends),
        grid=(npair, n // tb),
        in_specs=[main(0),
                  prev(npair), main(npair), nxt(npair),
                  prev(2 * npair), main(2 * npair), nxt(2 * npair),
                  pl.BlockSpec((None,) + bias.shape[1:], lambda c, i: (c, 0, 0, 0))],
        out_specs=pl.BlockSpec((tb, LANES), lambda c, i: (i, c)),
        out_shape=jax.ShapeDtypeStruct((n, C_W), BF16),
        scratch_shapes=[pltpu.VMEM((tb + 2 * C_HALO, LANES), BF16),
                        pltpu.VMEM((tb + 2 * C_HALO, LANES), BF16),
                        pltpu.VMEM((2 * tb, NA_ROWS * GRID_W), BF16)],
        compiler_params=_cparams(("parallel", "parallel")),
        name="attn_c",
    )(c_qkv, c_qkv, c_qkv, c_qkv, c_qkv, c_qkv, c_qkv, bias)


def _route(i, x, whl_ref, b_ref, info_ref, cnt_ref, run_scr, tri_scr):
    tm = x.shape[0]

    @pl.when(i == 0)
    def _():
        run_scr[...] = jnp.zeros_like(run_scr)

    xh = x.astype(BF16)
    xl = (x - xh.astype(F32)).astype(BF16)
    hh_hl = jnp.dot(xh, whl_ref[...], preferred_element_type=F32)
    logits = (hh_hl[:, :LANES] + jnp.dot(xl, whl_ref[:, :LANES], preferred_element_type=F32)
              + hh_hl[:, LANES:]) + b_ref[...]
    lt = logits.T
    first = lambda hit, n: jnp.min(jnp.where(hit, lax.broadcasted_iota(jnp.int32, (n, tm), 0).astype(F32),
                                             float(n)), axis=0, keepdims=True)
    lg = lt[0:N_GROUPS]
    g_sel = first(lg == jnp.max(lg, axis=0, keepdims=True), N_GROUPS)
    le = jnp.zeros((EXPERTS_PER_GROUP, tm), F32)
    for g in range(N_GROUPS):
        lo = N_GROUPS + g * EXPERTS_PER_GROUP
        le = jnp.where(g_sel == g, lt[lo:lo + EXPERTS_PER_GROUP], le)
    row = lax.broadcasted_iota(jnp.int32, (EXPERTS_PER_GROUP, tm), 0).astype(F32)
    i1 = first(le == jnp.max(le, axis=0, keepdims=True), EXPERTS_PER_GROUP)
    rest = jnp.where(row == i1, NEG_INF, le)
    i2 = first(jnp.logical_and(rest == jnp.max(rest, axis=0, keepdims=True), row != i1), EXPERTS_PER_GROUP)
    a = jnp.minimum(i1, i2)
    b = jnp.maximum(i1, i2)
    cls = g_sel * N_PAIRS + a * 3.0 - jnp.where(a == 2.0, 1.0, 0.0) + (b - a - 1.0)

    @pl.when(i == 0)
    def _():
        tri_scr[...] = (lax.broadcasted_iota(jnp.int32, (tm, tm), 0)
                        < lax.broadcasted_iota(jnp.int32, (tm, tm), 1)).astype(BF16)

    onehot = lax.broadcasted_iota(jnp.int32, (CLASS_ROWS, tm), 0).astype(F32) == cls
    before = jnp.dot(onehot.astype(BF16), tri_scr[...], preferred_element_type=F32) + run_scr[:, 0:1]
    rank = jnp.sum(jnp.where(onehot, before, 0.0), axis=0, keepdims=True)
    run_scr[...] = run_scr[...] + jnp.sum(onehot.astype(F32), axis=1, keepdims=True)
    srow = lax.broadcasted_iota(jnp.int32, (SUBLANES, tm), 0)
    info_ref[...] = jnp.where(srow == 0, cls, jnp.where(srow == 1, rank, 0.0))
    cnt_ref[...] = run_scr[...]


CLASS_ROWS = 32
SUBLANES = 8


def _tile_copy(src, src_tok, dst, dst_tok, sem):
    return pltpu.make_async_copy(src.at[pl.ds(pl.multiple_of(src_tok * SUBLANES, SUBLANES), SUBLANES), :],
                                 dst.at[pl.ds(pl.multiple_of(dst_tok * SUBLANES, SUBLANES), SUBLANES), :], sem)


ROW_UNROLL = 8


def _start_rows(copy, n):
    def body(g, carry):
        for u in range(ROW_UNROLL):
            copy(g * ROW_UNROLL + u).start(priority=u % 2)
        return carry

    lax.fori_loop(0, n // ROW_UNROLL, body, 0)


def _dispatch_kernel(dest_ref, pad_ref, x_ref, xs_ref, rec_scr, zero_scr, sems, zsem):
    i = pl.program_id(0)
    last = pl.num_programs(0) - 1
    tm = x_ref.shape[0]
    rows = tm * SUBLANES
    slot = i % 2
    tile_rows = MOE_TM * SUBLANES

    def zero_copy(c):
        start = pl.multiple_of(pad_ref[c] * SUBLANES, tile_rows)
        return pltpu.make_async_copy(zero_scr, xs_ref.at[pl.ds(start, tile_rows), :], zsem)

    @pl.when(i == 0)
    def _():
        zero_scr[...] = jnp.zeros_like(zero_scr)
        for c in range(pad_ref.shape[0]):
            @pl.when(pad_ref[c] >= 0)
            def _():
                zero_copy(c).start()
        for c in range(pad_ref.shape[0]):
            @pl.when(pad_ref[c] >= 0)
            def _():
                zero_copy(c).wait()

    for j in range(D_MODEL // LANES):
        rec_scr[slot, pl.ds(j, tm, stride=SUBLANES), :] = x_ref[:, j * LANES:(j + 1) * LANES]

    _start_rows(lambda r: _tile_copy(rec_scr.at[slot], r, xs_ref, dest_ref[i * tm + r], sems.at[slot]), tm)

    def wait_step(s):
        pltpu.make_async_copy(rec_scr.at[s], xs_ref.at[pl.ds(0, rows), :], sems.at[s]).wait()

    @pl.when(i > 0)
    def _():
        wait_step(1 - slot)

    @pl.when(i == last)
    def _():
        wait_step(slot)


def _dispatch(dest, pad_start, x, n_sorted):
    n = x.shape[0]
    tm = ROW_TM
    return pl.pallas_call(
        _dispatch_kernel,
        grid_spec=pltpu.PrefetchScalarGridSpec(
            num_scalar_prefetch=2,
            grid=(n // tm,),
            in_specs=[pl.BlockSpec((tm, D_MODEL), lambda i, d, p: (i, 0))],
            out_specs=pl.BlockSpec(memory_space=pl.ANY),
            scratch_shapes=[pltpu.VMEM((2, tm * SUBLANES, LANES), F32),
                            pltpu.VMEM((MOE_TM * SUBLANES, LANES), F32),
                            pltpu.SemaphoreType.DMA((2,)), pltpu.SemaphoreType.DMA]),
        out_shape=jax.ShapeDtypeStruct((n_sorted * SUBLANES, LANES), F32),
        compiler_params=_cparams(("arbitrary",)),
        name="dispatch",
    )(dest, pad_start, x)


def _expert_kernel(ea_ref, eb_ref, nt_ref, xs_ref, wr_ref, br_ref,
                   wga_ref, wua_ref, wda_ref, wgb_ref, wub_ref, wdb_ref, g_ref, b_ref, ys_ref, hid_scr):
    p = pl.program_id(0)
    tm = MOE_TM

    @pl.when(p >= nt_ref[0])
    def _():
        ys_ref[...] = jnp.zeros_like(ys_ref)

    @pl.when(p < nt_ref[0])
    def _():
        x = jnp.concatenate([xs_ref[pl.ds(j, tm, stride=SUBLANES), :] for j in range(D_MODEL // LANES)],
                            axis=1)
        xb = x.astype(BF16)

        logits = jnp.dot(xb, wr_ref[...], preferred_element_type=F32) + br_ref[...]
        lane = lax.broadcasted_iota(jnp.int32, (tm, LANES), 1)
        lane_a = N_GROUPS + ea_ref[p]
        lane_b = N_GROUPS + eb_ref[p]
        grp = ea_ref[p] // EXPERTS_PER_GROUP
        is_g = lane < N_GROUPS
        mg = jnp.max(jnp.where(is_g, logits, NEG_INF), axis=-1, keepdims=True)
        eg = jnp.where(is_g, jnp.exp(logits - mg), 0.0)
        g_gate = (jnp.sum(jnp.where(lane == grp, eg, 0.0), axis=-1, keepdims=True)
                  / jnp.sum(eg, axis=-1, keepdims=True))
        l_a = jnp.sum(jnp.where(lane == lane_a, logits, 0.0), axis=-1, keepdims=True)
        l_b = jnp.sum(jnp.where(lane == lane_b, logits, 0.0), axis=-1, keepdims=True)
        mx = jnp.maximum(l_a, l_b)
        p_a = jnp.exp(l_a - mx)
        p_b = jnp.exp(l_b - mx)
        scale = g_gate / (p_a + p_b)
        w_a = p_a * scale
        w_b = p_b * scale

        for e, (wg_ref, wu_ref) in enumerate(((wga_ref, wua_ref), (wgb_ref, wub_ref))):
            gate = jnp.dot(xb, wg_ref[...], preferred_element_type=F32)
            up = jnp.dot(xb, wu_ref[...], preferred_element_type=F32)
            hid_scr[e] = (gate * (1.0 / (1.0 + jnp.exp(-gate))) * up).astype(BF16)
        y = (w_a * jnp.dot(hid_scr[0], wda_ref[...], preferred_element_type=F32)
             + w_b * jnp.dot(hid_scr[1], wdb_ref[...], preferred_element_type=F32))
        out = _layer_norm(DEEPNORM_ALPHA * x + y, g_ref[...], b_ref[...])
        for j in range(D_MODEL // LANES):
            ys_ref[pl.ds(j, tm, stride=SUBLANES), :] = out[:, j * LANES:(j + 1) * LANES]


def _experts(layer, tile_ea, tile_eb, n_tiles, xs, w_router, b_router, w_gate, w_up, w_down, g, b):
    tm = MOE_TM
    rows = tm * SUBLANES
    n_grid = xs.shape[0] // rows
    last = lambda p, nt: jnp.maximum(jnp.minimum(p, nt[0] - 1), 0)
    wspec_a = lambda shape: pl.BlockSpec((None, None) + shape, lambda p, ea, eb, nt: (layer, ea[p], 0, 0))
    wspec_b = lambda shape: pl.BlockSpec((None, None) + shape, lambda p, ea, eb, nt: (layer, eb[p], 0, 0))
    const = lambda shape: pl.BlockSpec(shape, lambda p, ea, eb, nt: (0, 0))
    up_shape = (D_MODEL, D_EXPERT)
    dn_shape = (D_EXPERT, D_MODEL)
    return pl.pallas_call(
        _expert_kernel,
        grid_spec=pltpu.PrefetchScalarGridSpec(
            num_scalar_prefetch=3,
            grid=(n_grid,),
            in_specs=[pl.BlockSpec((rows, LANES), lambda p, ea, eb, nt: (last(p, nt), 0)),
                      const((D_MODEL, LANES)), const((1, LANES)),
                      wspec_a(up_shape), wspec_a(up_shape), wspec_a(dn_shape),
                      wspec_b(up_shape), wspec_b(up_shape), wspec_b(dn_shape),
                      const((1, D_MODEL)), const((1, D_MODEL))],
            out_specs=pl.BlockSpec((rows, LANES), lambda p, ea, eb, nt: (p, 0)),
            scratch_shapes=[pltpu.VMEM((2, tm, D_EXPERT), BF16)]),
        out_shape=jax.ShapeDtypeStruct(xs.shape, F32),
        compiler_params=_cparams(("arbitrary",)),
        name="experts",
    )(tile_ea, tile_eb, n_tiles, xs, w_router.astype(BF16), b_router,
      w_gate, w_up, w_down, w_gate, w_up, w_down, g.reshape(1, D_MODEL), b.reshape(1, D_MODEL))


def _gather_kernel(dest_ref, ys_ref, *refs, seg_blocks):
    out_refs = refs[:len(seg_blocks)]
    rec_scr, sems = refs[len(seg_blocks):]
    i = pl.program_id(0)
    n_steps = pl.num_programs(0)
    tm = out_refs[0].shape[0]
    rows = tm * SUBLANES
    slot = i % 2

    def fetch(s):
        _start_rows(lambda r: _tile_copy(ys_ref, dest_ref[s * tm + r], rec_scr.at[s % 2], r, sems.at[s % 2]), tm)

    @pl.when(i == 0)
    def _():
        fetch(i)

    @pl.when(i + 1 < n_steps)
    def _():
        fetch(i + 1)

    pltpu.make_async_copy(ys_ref.at[pl.ds(0, rows), :], rec_scr.at[slot], sems.at[slot]).wait()

    start = 0
    for out_ref, nb in zip(out_refs, seg_blocks):
        @pl.when(jnp.logical_and(i >= start, i < start + nb))
        def _(out_ref=out_ref):
            for j in range(D_MODEL // LANES):
                out_ref[:, j * LANES:(j + 1) * LANES] = rec_scr[slot, pl.ds(j, tm, stride=SUBLANES), :]
        start += nb


def _gather_rows(dest, ys, seg_rows):
    tm = ROW_TM
    seg_blocks = tuple(r // tm for r in seg_rows)
    return pl.pallas_call(
        functools.partial(_gather_kernel, seg_blocks=seg_blocks),
        grid_spec=pltpu.PrefetchScalarGridSpec(
            num_scalar_prefetch=1,
            grid=(sum(seg_blocks),),
            in_specs=[pl.BlockSpec(memory_space=pl.ANY)],
            out_specs=_segment_specs(seg_blocks, tm, D_MODEL),
            scratch_shapes=[pltpu.VMEM((2, tm * SUBLANES, LANES), F32), pltpu.SemaphoreType.DMA((2,))]),
        out_shape=[jax.ShapeDtypeStruct((r, D_MODEL), F32) for r in seg_rows],
        compiler_params=_cparams(("arbitrary",)),
        name="gather_rows",
    )(dest, ys)


_PAIR_A = np.array([0, 0, 0, 1, 1, 2], np.int32)
_PAIR_B = np.array([1, 2, 3, 2, 3, 3], np.int32)


def _moe_layer(layer, x, info, counts, w_router, b_router, w_gate, w_up, w_down, g, b, out_rows):
    n = x.shape[0]
    tm = MOE_TM
    n_sorted = n + N_CLASSES * tm
    cls = info[:, 0, :].reshape(n).astype(jnp.int32)
    rank = info[:, 1, :].reshape(n).astype(jnp.int32)
    counts = counts[:N_CLASSES, 0].astype(jnp.int32)
    padded = (counts + tm - 1) // tm * tm
    classes = jnp.arange(N_CLASSES, dtype=jnp.int32)
    ends = jnp.sum(jnp.where(classes[None, :] <= classes[:, None], padded[None, :], 0), axis=1)
    offs = ends - padded
    total = ends[N_CLASSES - 1]
    dest = rank + jnp.sum(jnp.where(cls[:, None] == classes[None, :], offs[None, :], 0), axis=1)
    unused = total + classes * tm
    pad_start = jnp.concatenate([jnp.where(padded > 0, ends - tm, -1),
                                 jnp.where(unused < n_sorted, unused, -1)]).astype(jnp.int32)
    tile_start = jnp.arange(n_sorted // tm, dtype=jnp.int32) * tm
    tile_start = jnp.minimum(tile_start, total - tm)
    tile_cls = jnp.sum((ends[None, :] <= tile_start[:, None]).astype(jnp.int32), axis=1)
    pair = tile_cls % N_PAIRS
    pair_a = jnp.sum(jnp.where(pair[:, None] == np.arange(N_PAIRS)[None, :], _PAIR_A[None, :], 0), axis=1)
    pair_b = jnp.sum(jnp.where(pair[:, None] == np.arange(N_PAIRS)[None, :], _PAIR_B[None, :], 0), axis=1)
    grp = tile_cls // N_PAIRS
    tile_ea = (grp * EXPERTS_PER_GROUP + pair_a).astype(jnp.int32)
    tile_eb = (grp * EXPERTS_PER_GROUP + pair_b).astype(jnp.int32)
    n_tiles = (total // tm).astype(jnp.int32).reshape(1)
    xs = _dispatch(dest.astype(jnp.int32), pad_start, x, n_sorted)
    ys = _experts(layer, tile_ea, tile_eb, n_tiles, xs, w_router, b_router, w_gate, w_up, w_down, g, b)
    return _gather_rows(dest.astype(jnp.int32), ys, out_rows)


_A_ORDER = np.array([0, 4, 1, 5, 2, 6, 3, 7])


def _prep_ab(w_in, w_out):
    qa = w_in[:, :QA_W].reshape(D_MODEL, A_HEADS, HEAD_DIM)[:, _A_ORDER].reshape(D_MODEL, QA_W) * ATTN_SCALE
    kva = w_in[:, QA_W:A_IN]
    qb = w_in[:, A_IN:A_IN + B_W] * ATTN_SCALE
    kvb = w_in[:, A_IN + B_W:]
    w = jnp.concatenate([qa, kva, qb, kvb], axis=1).astype(BF16)
    wo_a = w_out[:QA_W].reshape(A_HEADS, HEAD_DIM, D_MODEL)[_A_ORDER].reshape(QA_W, D_MODEL).astype(BF16)
    wo_b = w_out[QA_W:].astype(BF16)
    return w, wo_a, wo_b


def _prep_c(w_in, w_out):
    w = jnp.concatenate([w_in[:, :C_W] * ATTN_SCALE, w_in[:, C_W:]], axis=1).astype(BF16)
    return w, w_out.astype(BF16)


def _trunk(xs, seg_starts, seg_ends, rel_bias, w_in_ab, a_sink, w_out_ab, w_in_c, c_rpb, w_out_c,
           ln1_g, ln1_b, ln2_g, ln2_b, router_g_w, router_g_b, router_e_w, router_e_b,
           w_gate, w_up, w_down):
    io_rows = tuple(a.shape[0] for a in xs)
    n = sum(io_rows)
    bias_a = _bias_a(rel_bias)
    bias_b = _bias_b(rel_bias)
    w_gate = w_gate.astype(BF16)
    w_up = w_up.astype(BF16)
    w_down = w_down.astype(BF16)
    for l in range(DEPTH):
        i = l // 2
        pad = LANES - N_GROUPS - N_EXPERTS
        w_router = jnp.pad(jnp.concatenate([router_g_w[l], router_e_w[l]], axis=1), ((0, 0), (0, pad)))
        b_router = jnp.pad(jnp.concatenate([router_g_b[l], router_e_b[l]]), (0, pad)).reshape(1, LANES)
        if l % 2 == 0:
            w, wo_a, wo_b = _prep_ab(w_in_ab[i], w_out_ab[i])
            a_qkv, b_qkv = _inproj(xs, w, ((0, A_IN), (A_IN, A_IN + B_IN)), (BF16, F32))
            o_a = _attn_a(a_qkv, bias_a, a_sink[i].astype(F32) * LOG2E, seg_starts, seg_ends)
            o_b = _attn_b(b_qkv, bias_b, seg_starts, seg_ends)
            parts, weights = [o_a, o_b], [wo_a, wo_b]
        else:
            w, wo = _prep_c(w_in_c[i], w_out_c[i])
            (c_qkv,) = _inproj(xs, w, ((0, 3 * C_W),), (BF16,))
            parts, weights = [_attn_c(c_qkv, _bias_c(c_rpb[i]), seg_starts, seg_ends)], [wo]
        x, info, counts = _outproj_ln(parts, weights, xs, ln1_g[l], ln1_b[l], w_router, b_router)
        xs = _moe_layer(l, x, info, counts, w_router, b_router, w_gate, w_up, w_down, ln2_g[l], ln2_b[l],
                        io_rows if l == DEPTH - 1 else (n,))
    return xs


def kernel(x_prompt, x_sample, rel_bias, w_in_ab, a_sink, w_out_ab, w_in_c, c_rpb, w_out_c,
           ln1_g, ln1_b, ln2_g, ln2_b, router_g_w, router_g_b, router_e_w, router_e_b,
           w_gate, w_up, w_down):
    lens = [x_prompt.shape[1]] * x_prompt.shape[0] + [x_sample.shape[1]] * x_sample.shape[0]
    seg_ends = tuple(int(v) for v in np.cumsum(lens))
    seg_starts = tuple(e - n for e, n in zip(seg_ends, lens))
    for n in lens:
        assert n % ATT_TB == 0 and n % A_TB == 0 and n // GRID_W >= NA_ROWS
    xs = [x_prompt.reshape(-1, D_MODEL), x_sample.reshape(-1, D_MODEL)]
    y_p, y_s = _trunk(xs, seg_starts, seg_ends, rel_bias, w_in_ab, a_sink, w_out_ab, w_in_c, c_rpb, w_out_c,
                      ln1_g, ln1_b, ln2_g, ln2_b, router_g_w, router_g_b, router_e_w, router_e_b,
                      w_gate, w_up, w_down)
    return (y_p.reshape(x_prompt.shape), y_s.reshape(x_sample.shape))
```

```python
import functools
import math

import numpy as np
import jax
import jax.numpy as jnp
from jax import lax
from jax.experimental import pallas as pl
from jax.experimental.pallas import tpu as pltpu

F32 = jnp.float32
BF16 = jnp.bfloat16

D_MODEL = 1024
DEPTH = 4
HEAD_DIM = 64
LANES = 128
A_HEADS = 8
A_KV_HEADS = 2
A_WINDOW = 128
B_HEADS = 8
B_BRANCHES = ((128, 1), (512, 4), (2048, 16))
B_HALF = 64
C_HEADS = 16
GRID_W = 64
NA_ROWS = 8
NA_COLS = 16
REL_BUCKETS = 32
REL_MAX_DIST = 1024
N_GROUPS = 4
EXPERTS_PER_GROUP = 4
N_EXPERTS = 16
D_EXPERT = 512
N_PAIRS = 6
N_CLASSES = N_GROUPS * N_PAIRS
DEEPNORM_ALPHA = (2.0 * DEPTH) ** 0.25
LN_EPS = 1e-5
LOG2E = math.log2(math.e)
ATTN_SCALE = HEAD_DIM ** -0.5 * LOG2E
NEG_INF = -1e30

QA_W = A_HEADS * HEAD_DIM
KVA_W = A_KV_HEADS * HEAD_DIM
A_IN = QA_W + 2 * KVA_W
B_W = B_HEADS * HEAD_DIM
B_IN = 3 * B_W
C_W = C_HEADS * HEAD_DIM

ATT_TB = 1024
MM_TM = 512
MOE_TM = 256
ROW_TM = 512
VMEM_LIMIT = 56 * 1024 * 1024


def _cparams(sem):
    return pltpu.CompilerParams(dimension_semantics=sem, vmem_limit_bytes=VMEM_LIMIT)


def _segment_flags(tok0, size, seg_starts, seg_ends):
    is_first = functools.reduce(jnp.logical_or, [tok0 == s for s in seg_starts])
    is_last = functools.reduce(jnp.logical_or, [tok0 + size == e for e in seg_ends])
    return is_first, is_last


def _t5_bucket_np(rel):
    half_b = REL_BUCKETS // 2
    max_exact = half_b // 2
    n = np.abs(rel)
    large = max_exact + (np.log(np.maximum(n, max_exact).astype(np.float32) / max_exact)
                         / math.log(REL_MAX_DIST / max_exact) * (half_b - max_exact)).astype(np.int32)
    large = np.minimum(large, half_b - 1)
    return np.where(rel > 0, half_b, 0) + np.where(n < max_exact, n, large)


def _banded_bias(table, half, dil):
    rel = np.arange(3 * half)[None, :] - half - np.arange(half)[:, None]
    bucket = jnp.asarray(_t5_bucket_np(rel * dil).astype(np.int32))
    hit = bucket[None] == jnp.arange(REL_BUCKETS, dtype=jnp.int32)[:, None, None]
    bias = jnp.sum(jnp.where(hit[:, None], table.astype(F32)[:, :, None, None], 0.0), axis=0)
    return jnp.where(jnp.asarray(np.abs(rel) <= half)[None], bias * LOG2E, NEG_INF)


def _bias_a(rel_bias):
    w = A_WINDOW
    ch = A_CHUNK
    b = _banded_bias(rel_bias[:, :A_HEADS], w, 1).reshape(A_KV_HEADS, 4, w // ch, ch, 3 * w)
    b = b.transpose(0, 2, 1, 3, 4).reshape(A_KV_HEADS, w // ch, 4 * ch, 3 * w)
    col = np.arange(3 * w)
    first = jnp.where(jnp.asarray(col < w), NEG_INF, b)
    last = jnp.where(jnp.asarray(col >= 2 * w), NEG_INF, b)
    return jnp.stack([b, first, last])


def _bias_b(rel_bias):
    per = [_banded_bias(rel_bias[:, A_HEADS:], B_HALF, d) for _, d in B_BRANCHES]
    b = jnp.stack(per, axis=1)
    return b.reshape(B_HEADS // 2, 2, len(B_BRANCHES), B_HALF, 3 * B_HALF).transpose(0, 2, 1, 3, 4) \
            .reshape(B_HEADS // 2, len(B_BRANCHES), 2 * B_HALF, 3 * B_HALF)


def _bias_c(rpb):
    gw = GRID_W
    n_dr = 2 * NA_ROWS - 1
    side = gw - NA_COLS
    p = jnp.concatenate([jnp.repeat(rpb[..., :1], side, axis=-1), rpb.astype(F32),
                         jnp.repeat(rpb[..., -1:], side + 1, axis=-1)], axis=-1)
    z = jnp.broadcast_to(p[:, :, None, :], (C_HEADS, n_dr, gw, 2 * gw)).reshape(C_HEADS, n_dr, 2 * gw * gw)
    t = z[:, :, gw - 1:gw - 1 + gw * (2 * gw - 1)].reshape(C_HEADS, n_dr, gw, 2 * gw - 1)[..., :gw]
    cq = np.arange(gw)[:, None]
    w = np.arange(gw)[None, :]
    c0 = np.clip(cq - NA_COLS // 2, 0, gw - NA_COLS)
    t = jnp.where(jnp.asarray((w >= c0) & (w < c0 + NA_COLS)), t * LOG2E, NEG_INF)
    t = t.reshape(C_HEADS // 2, 2, n_dr, gw, gw).transpose(0, 1, 3, 2, 4)
    bias = jnp.stack([t[:, :, :, NA_ROWS - 1 - s:2 * NA_ROWS - 1 - s, :].reshape(C_HEADS // 2, 2, gw, NA_ROWS * gw)
                      for s in range(NA_ROWS)], axis=1)
    return bias.reshape(C_HEADS // 2, NA_ROWS, 2 * gw, NA_ROWS * gw)


def _segment_blocks(segs, tm):
    return tuple(a.shape[0] // tm for a in segs)


def _segment_specs(seg_blocks, tm, width):
    specs, start = [], 0
    for nb in seg_blocks:
        specs.append(pl.BlockSpec((tm, width), lambda i, *_, s=start, nb=nb: (jnp.clip(i - s, 0, nb - 1), 0)))
        start += nb
    return specs


def _segment_rows(i, refs, seg_blocks):
    x, start = refs[0][...], seg_blocks[0]
    for ref, nb in zip(refs[1:], seg_blocks[1:]):
        x = jnp.where(i >= start, ref[...], x)
        start += nb
    return x


def _inproj_kernel(*refs, splits, seg_blocks):
    n_seg = len(seg_blocks)
    w_ref = refs[n_seg]
    o_refs = refs[n_seg + 1:]
    x = _segment_rows(pl.program_id(0), refs[:n_seg], seg_blocks).astype(BF16)
    for o_ref, (lo, hi) in zip(o_refs, splits):
        o_ref[...] = jnp.dot(x, w_ref[:, lo:hi], preferred_element_type=F32).astype(o_ref.dtype)


def _inproj(xs, w, splits, dtypes):
    seg_blocks = _segment_blocks(xs, MM_TM)
    n = sum(seg_blocks) * MM_TM
    return pl.pallas_call(
        functools.partial(_inproj_kernel, splits=splits, seg_blocks=seg_blocks),
        grid=(n // MM_TM,),
        in_specs=_segment_specs(seg_blocks, MM_TM, D_MODEL) + [pl.BlockSpec(w.shape, lambda i: (0, 0))],
        out_specs=[pl.BlockSpec((MM_TM, hi - lo), lambda i: (i, 0)) for lo, hi in splits],
        out_shape=[jax.ShapeDtypeStruct((n, hi - lo), dt) for (lo, hi), dt in zip(splits, dtypes)],
        compiler_params=_cparams(("parallel",)),
        name="inproj",
    )(*xs, w)


def _layer_norm(z, g, b):
    mu = jnp.mean(z, axis=-1, keepdims=True)
    zc = z - mu
    var = jnp.mean(zc * zc, axis=-1, keepdims=True)
    return zc * lax.rsqrt(var + LN_EPS) * g + b


def _outproj_ln_kernel(*refs, n_parts, seg_blocks):
    n_seg = len(seg_blocks)
    o_refs = refs[:n_parts]
    w_refs = refs[n_parts:2 * n_parts]
    x_refs = refs[2 * n_parts:2 * n_parts + n_seg]
    g_ref, b_ref, whl_ref, br_ref, out_ref, info_ref, cnt_ref, run_scr, tri_scr = refs[2 * n_parts + n_seg:]
    i = pl.program_id(0)
    h = DEEPNORM_ALPHA * _segment_rows(i, x_refs, seg_blocks)
    for o_ref, w_ref in zip(o_refs, w_refs):
        h = h + jnp.dot(o_ref[...], w_ref[...], preferred_element_type=F32)
    out = _layer_norm(h, g_ref[...], b_ref[...])
    out_ref[...] = out
    _route(i, out, whl_ref, br_ref, info_ref, cnt_ref, run_scr, tri_scr)


def _outproj_ln(parts, weights, xs, g, b, w_router, b_router):
    tm = MM_TM
    seg_blocks = _segment_blocks(xs, tm)
    n = sum(seg_blocks) * tm
    n_parts = len(parts)
    wh = w_router.astype(BF16)
    whl = jnp.concatenate([wh, (w_router - wh.astype(F32)).astype(BF16)], axis=1)
    const = lambda shape: pl.BlockSpec(shape, lambda i: (0, 0))
    return pl.pallas_call(
        functools.partial(_outproj_ln_kernel, n_parts=n_parts, seg_blocks=seg_blocks),
        grid=(n // tm,),
        in_specs=([pl.BlockSpec((tm, p.shape[1]), lambda i: (i, 0)) for p in parts]
                  + [const(w.shape) for w in weights]
                  + _segment_specs(seg_blocks, tm, D_MODEL)
                  + [const((1, D_MODEL)), const((1, D_MODEL)),
                     const((D_MODEL, 2 * LANES)), const((1, LANES))]),
        out_specs=[pl.BlockSpec((tm, D_MODEL), lambda i: (i, 0)),
                   pl.BlockSpec((None, SUBLANES, tm), lambda i: (i, 0, 0)),
                   const((CLASS_ROWS, LANES))],
        out_shape=[jax.ShapeDtypeStruct((n, D_MODEL), F32),
                   jax.ShapeDtypeStruct((n // tm, SUBLANES, tm), F32),
                   jax.ShapeDtypeStruct((CLASS_ROWS, LANES), F32)],
        scratch_shapes=[pltpu.VMEM((CLASS_ROWS, LANES), F32), pltpu.VMEM((tm, tm), BF16)],
        compiler_params=_cparams(("arbitrary",)),
        name="outproj_ln",
    )(*parts, *weights, *xs, g.reshape(1, D_MODEL), b.reshape(1, D_MODEL), whl, b_router)


ATT_DEPTH = 3
B_DEPTH = 10


def _staged(n, weights, values, depth=ATT_DEPTH):
    for i in range(min(depth, n)):
        weights(i)
    for i in range(n):
        if i + depth < n:
            weights(i + depth)
        values(i)


def _attn_a_kernel(q_ref, kvm_ref, kvp_ref, kvn_ref, bias_ref, sink_ref, o_ref, kv_scr, p_scr,
                   *, seg_starts, seg_ends):
    w = A_WINDOW
    tb = A_TB
    n_sub = tb // w
    tok0 = pl.program_id(0) * tb
    is_first, is_last = _segment_flags(tok0, tb, seg_starts, seg_ends)
    kv_scr[0:w, :] = kvp_ref[...]
    kv_scr[w:w + tb, :] = kvm_ref[...]
    kv_scr[w + tb:, :] = kvn_ref[...]
    low = lax.broadcasted_iota(jnp.int32, (1, LANES), 1) < HEAD_DIM
    ch = A_CHUNK
    per = w // ch

    sinks = [jnp.concatenate([jnp.full((ch, 1), sink_ref[c + 4 * g], F32) for c in range(4)], axis=0)
             for g in range(A_KV_HEADS)]

    def scores(t, g):
        j = t // per
        q = q_ref[t * ch:(t + 1) * ch, :]
        qg = jnp.concatenate([jnp.where(low, q[:, c * LANES:(c + 1) * LANES], 0) if g == 0
                              else jnp.where(low, 0, q[:, c * LANES:(c + 1) * LANES]) for c in range(4)], axis=0)
        k2 = kv_scr[j * w:(j + 3) * w, :LANES]
        return lax.dot_general(qg, k2, (((1,), (1,)), ((), ())), preferred_element_type=F32)

    tiles = [(t, g) for t in range(tb // ch) for g in range(A_KV_HEADS)]
    rows = 4 * ch
    rdens = {}
    outs = {}

    def weights(i):
        t, g = tiles[i]
        j = t // per
        if j == 0:
            variant = jnp.where(is_first, 1, 0)
        elif j == n_sub - 1:
            variant = jnp.where(is_last, 2, 0)
        else:
            variant = 0
        s = scores(t, g) + bias_ref[variant, g, t % per]
        m = jnp.maximum(jnp.max(s, axis=-1, keepdims=True), sinks[g])
        e = jnp.exp2(s - m)
        rdens[i] = 1.0 / (jnp.sum(e, axis=-1, keepdims=True) + jnp.exp2(sinks[g] - m))
        p_scr[i * rows:(i + 1) * rows, :] = e.astype(BF16)

    def values(i):
        t, g = tiles[i]
        j = t // per
        v2 = kv_scr[j * w:(j + 3) * w, LANES:]
        outs[g] = jnp.dot(p_scr[i * rows:(i + 1) * rows, :], v2, preferred_element_type=F32) * rdens[i]
        if g == A_KV_HEADS - 1:
            for c in range(4):
                oc = jnp.where(low, outs[0][c * ch:(c + 1) * ch], outs[1][c * ch:(c + 1) * ch])
                o_ref[t * ch:(t + 1) * ch, c * LANES:(c + 1) * LANES] = oc.astype(o_ref.dtype)

    _staged(len(tiles), weights, values)


A_CHUNK = 64
A_TB = 2 * ATT_TB


def _attn_a(a_qkv, bias, sink, seg_starts, seg_ends):
    n = a_qkv.shape[0]
    w = A_WINDOW
    tb = A_TB
    sub = tb // w
    nhb = n // w
    kv_col = QA_W // (2 * LANES)
    return pl.pallas_call(
        functools.partial(_attn_a_kernel, seg_starts=seg_starts, seg_ends=seg_ends),
        grid=(n // tb,),
        in_specs=[pl.BlockSpec((tb, QA_W), lambda i: (i, 0)),
                  pl.BlockSpec((tb, 2 * LANES), lambda i: (i, kv_col)),
                  pl.BlockSpec((w, 2 * LANES), lambda i: (jnp.maximum(i * sub - 1, 0), kv_col)),
                  pl.BlockSpec((w, 2 * LANES), lambda i: (jnp.minimum((i + 1) * sub, nhb - 1), kv_col)),
                  pl.BlockSpec(bias.shape, lambda i: (0,) * bias.ndim),
                  pl.BlockSpec(memory_space=pltpu.SMEM)],
        out_specs=pl.BlockSpec((tb, QA_W), lambda i: (i, 0)),
        out_shape=jax.ShapeDtypeStruct((n, QA_W), BF16),
        scratch_shapes=[pltpu.VMEM((tb + 2 * w, 2 * LANES), BF16),
                        pltpu.VMEM((A_HEADS * tb, 3 * w), BF16)],
        compiler_params=_cparams(("parallel",)),
        name="attn_a",
    )(a_qkv, a_qkv, a_qkv, a_qkv, bias, sink)


def _attn_b_kernel(q_ref, kp_ref, km_ref, kn_ref, vp_ref, vm_ref, vn_ref, bias_ref, o_ref,
                   k_scr, v_scr, o_scr, m_scr, l_scr, p_scr, *, seg_starts, seg_ends):
    tb = B_TB
    h = B_HALF
    tok0 = pl.program_id(1) * tb
    is_first, is_last = _segment_flags(tok0, tb, seg_starts, seg_ends)
    k_scr[0:tb, :] = kp_ref[...]
    k_scr[tb:2 * tb, :] = km_ref[...]
    k_scr[2 * tb:, :] = kn_ref[...]
    v_scr[0:tb, :] = vp_ref[...]
    v_scr[tb:2 * tb, :] = vm_ref[...]
    v_scr[2 * tb:, :] = vn_ref[...]
    lane = lax.broadcasted_iota(jnp.int32, (1, LANES), 1)
    low = lane < HEAD_DIM
    col = lax.broadcasted_iota(jnp.int32, (1, 3 * h), 1)

    pen_first = jnp.where(jnp.logical_and(col < h, is_first), NEG_INF, 0.0)
    pen_last = jnp.where(jnp.logical_and(col >= 2 * h, is_last), NEG_INF, 0.0)

    def slices(d, r, b):
        row0 = r + h * d * b
        if d == 1:
            return pl.ds(row0, h), pl.ds(tb + row0 - h, 3 * h)
        return pl.ds(row0, h, stride=d), pl.ds(tb + row0 - h * d, 3 * h, stride=d)

    def scores(br, d, r, b):
        qs, ks = slices(d, r, b)
        q = q_ref[qs, :].astype(BF16)
        k = k_scr[ks, :].astype(BF16)
        qq = jnp.concatenate([jnp.where(low, q, 0), jnp.where(low, 0, q)], axis=0)
        return lax.dot_general(qq, k, (((1,), (1,)), ((), ())), preferred_element_type=F32)

    tiles = [(br, d, r, b) for br, (_, d) in enumerate(B_BRANCHES)
             for r in range(d) for b in range(tb // (h * d))]

    def weights(i):
        br, d, r, b = tiles[i]
        qs, _ = slices(d, r, b)
        s = scores(br, d, r, b) + bias_ref[br]
        if b == 0:
            s = s + pen_first
        if b == tb // (h * d) - 1:
            s = s + pen_last
        m = jnp.max(s, axis=-1, keepdims=True)
        e = jnp.exp2(s - m)
        l = jnp.sum(e, axis=-1, keepdims=True)
        p_scr[i * 2 * h:(i + 1) * 2 * h, :] = e.astype(BF16)
        m_scr[br, qs, :] = jnp.where(low, m[:h], m[h:])
        l_scr[br, qs, :] = jnp.where(low, l[:h], l[h:])

    def values(i):
        br, d, r, b = tiles[i]
        qs, ks = slices(d, r, b)
        v = v_scr[ks, :].astype(BF16)
        pv = jnp.dot(p_scr[i * 2 * h:(i + 1) * 2 * h, :], v, preferred_element_type=F32)
        o_scr[br, qs, :] = jnp.where(low, pv[:h], pv[h:])

    _staged(len(tiles), weights, values, B_DEPTH)

    m_all = jnp.maximum(jnp.maximum(m_scr[0], m_scr[1]), m_scr[2])
    num = jnp.zeros((tb, LANES), F32)
    den = jnp.zeros((tb, LANES), F32)
    for br in range(len(B_BRANCHES)):
        a = jnp.exp2(m_scr[br] - m_all)
        num = num + a * o_scr[br]
        den = den + a * l_scr[br]
    o_ref[...] = (num / den).astype(o_ref.dtype)


B_TB = ATT_TB


def _attn_b(b_qkv, bias, seg_starts, seg_ends):
    n = b_qkv.shape[0]
    tb = B_TB
    nblk = n // tb
    npair = B_HEADS // 2
    prev = lambda i: jnp.maximum(i - 1, 0)
    nxt = lambda i: jnp.minimum(i + 1, nblk - 1)
    blk = lambda rowf, off: pl.BlockSpec((tb, LANES), lambda c, i: (rowf(i), off + c))
    same = lambda i: i
    stat = pltpu.VMEM((len(B_BRANCHES), tb, LANES), F32)
    return pl.pallas_call(
        functools.partial(_attn_b_kernel, seg_starts=seg_starts, seg_ends=seg_ends),
        grid=(npair, nblk),
        in_specs=[blk(same, 0),
                  blk(prev, npair), blk(same, npair), blk(nxt, npair),
                  blk(prev, 2 * npair), blk(same, 2 * npair), blk(nxt, 2 * npair),
                  pl.BlockSpec((None,) + bias.shape[1:], lambda c, i: (c, 0, 0, 0))],
        out_specs=pl.BlockSpec((tb, LANES), lambda c, i: (i, c)),
        out_shape=jax.ShapeDtypeStruct((n, B_W), BF16),
        scratch_shapes=[pltpu.VMEM((3 * tb, LANES), F32), pltpu.VMEM((3 * tb, LANES), F32),
                        stat, stat, stat,
                        pltpu.VMEM((len(B_BRANCHES) * 2 * tb, 3 * B_HALF), BF16)],
        compiler_params=_cparams(("parallel", "parallel")),
        name="attn_b",
    )(b_qkv, b_qkv, b_qkv, b_qkv, b_qkv, b_qkv, b_qkv, bias)


C_HALO = (NA_ROWS // 2) * GRID_W

def _attn_c_kernel(q_ref, kp_ref, km_ref, kn_ref, vp_ref, vm_ref, vn_ref, bias_ref, o_ref,
                   k_scr, v_scr, p_scr, *, seg_starts, seg_ends):
    tb = ATT_TB
    gw = GRID_W
    nkeys = NA_ROWS * gw
    tok0 = pl.program_id(1) * tb
    k_scr[0:C_HALO, :] = kp_ref[...]
    k_scr[C_HALO:C_HALO + tb, :] = km_ref[...]
    k_scr[C_HALO + tb:, :] = kn_ref[...]
    v_scr[0:C_HALO, :] = vp_ref[...]
    v_scr[C_HALO:C_HALO + tb, :] = vm_ref[...]
    v_scr[C_HALO + tb:, :] = vn_ref[...]
    seg_row0 = jnp.int32(0)
    seg_rows = jnp.int32(0)
    for s, e in zip(seg_starts, seg_ends):
        inside = jnp.logical_and(tok0 >= s, tok0 < e)
        seg_row0 = jnp.where(inside, s // gw, seg_row0)
        seg_rows = jnp.where(inside, (e - s) // gw, seg_rows)
    lane = lax.broadcasted_iota(jnp.int32, (1, LANES), 1)
    low = lane < HEAD_DIM

    def window(rr):
        rs = tok0 // gw + rr - seg_row0
        start = jnp.clip(rs - NA_ROWS // 2, 0, seg_rows - NA_ROWS)
        shift = rs - start
        return shift, pl.ds(pl.multiple_of((rr + NA_ROWS // 2 - shift) * gw, gw), nkeys)

    def scores(rr):
        _, ks = window(rr)
        q = q_ref[rr * gw:(rr + 1) * gw, :]
        qq = jnp.concatenate([jnp.where(low, q, 0), jnp.where(low, 0, q)], axis=0)
        return lax.dot_general(qq, k_scr[ks, :], (((1,), (1,)), ((), ())), preferred_element_type=F32)

    n_rows = tb // gw
    rdens = {}

    def weights(rr):
        shift, _ = window(rr)
        s = scores(rr) + bias_ref[shift]
        m = jnp.max(s, axis=-1, keepdims=True)
        e = jnp.exp2(s - m)
        rdens[rr] = 1.0 / jnp.sum(e, axis=-1, keepdims=True)
        p_scr[rr * 2 * gw:(rr + 1) * 2 * gw, :] = e.astype(BF16)

    def values(rr):
        _, ks = window(rr)
        pv = jnp.dot(p_scr[rr * 2 * gw:(rr + 1) * 2 * gw, :], v_scr[ks, :], preferred_element_type=F32) * rdens[rr]
        o_ref[rr * gw:(rr + 1) * gw, :] = jnp.where(low, pv[:gw], pv[gw:]).astype(o_ref.dtype)

    _staged(n_rows, weights, values)


def _attn_c(c_qkv, bias, seg_starts, seg_ends):
    n = c_qkv.shape[0]
    tb = ATT_TB
    npair = C_HEADS // 2
    sub = tb // C_HALO
    nhb = n // C_HALO
    main = lambda off: pl.BlockSpec((tb, LANES), lambda c, i: (i, off + c))
    prev = lambda off: pl.BlockSpec((C_HALO, LANES), lambda c, i: (jnp.maximum(i * sub - 1, 0), off + c))
    nxt = lambda off: pl.BlockSpec((C_HALO, LANES),
                                   lambda c, i: (jnp.minimum((i + 1) * sub, nhb - 1), off + c))
    return pl.pallas_call(
        functools.partial(_attn_c_kernel, seg_starts=seg_starts, seg_ends=seg_ends),
        grid=(npair, n // tb),
        in_specs=[main(0),
                  prev(npair), main(npair), nxt(npair),
                  prev(2 * npair), main(2 * npair), nxt(2 * npair),
                  pl.BlockSpec((None,) + bias.shape[1:], lambda c, i: (c, 0, 0, 0))],
        out_specs=pl.BlockSpec((tb, LANES), lambda c, i: (i, c)),
        out_shape=jax.ShapeDtypeStruct((n, C_W), BF16),
        scratch_shapes=[pltpu.VMEM((tb + 2 * C_HALO, LANES), BF16),
                        pltpu.VMEM((tb + 2 * C_HALO, LANES), BF16),
                        pltpu.VMEM((2 * tb, NA_ROWS * GRID_W), BF16)],
        compiler_params=_cparams(("parallel", "parallel")),
        name="attn_c",
    )(c_qkv, c_qkv, c_qkv, c_qkv, c_qkv, c_qkv, c_qkv, bias)


def _route(i, x, whl_ref, b_ref, info_ref, cnt_ref, run_scr, tri_scr):
    tm = x.shape[0]

    @pl.when(i == 0)
    def _():
        run_scr[...] = jnp.zeros_like(run_scr)

    xh = x.astype(BF16)
    xl = (x - xh.astype(F32)).astype(BF16)
    hh_hl = jnp.dot(xh, whl_ref[...], preferred_element_type=F32)
    logits = (hh_hl[:, :LANES] + jnp.dot(xl, whl_ref[:, :LANES], preferred_element_type=F32)
              + hh_hl[:, LANES:]) + b_ref[...]
    lt = logits.T
    first = lambda hit, n: jnp.min(jnp.where(hit, lax.broadcasted_iota(jnp.int32, (n, tm), 0).astype(F32),
                                             float(n)), axis=0, keepdims=True)
    lg = lt[0:N_GROUPS]
    g_sel = first(lg == jnp.max(lg, axis=0, keepdims=True), N_GROUPS)
    le = jnp.zeros((EXPERTS_PER_GROUP, tm), F32)
    for g in range(N_GROUPS):
        lo = N_GROUPS + g * EXPERTS_PER_GROUP
        le = jnp.where(g_sel == g, lt[lo:lo + EXPERTS_PER_GROUP], le)
    row = lax.broadcasted_iota(jnp.int32, (EXPERTS_PER_GROUP, tm), 0).astype(F32)
    i1 = first(le == jnp.max(le, axis=0, keepdims=True), EXPERTS_PER_GROUP)
    rest = jnp.where(row == i1, NEG_INF, le)
    i2 = first(jnp.logical_and(rest == jnp.max(rest, axis=0, keepdims=True), row != i1), EXPERTS_PER_GROUP)
    a = jnp.minimum(i1, i2)
    b = jnp.maximum(i1, i2)
    cls = g_sel * N_PAIRS + a * 3.0 - jnp.where(a == 2.0, 1.0, 0.0) + (b - a - 1.0)

    @pl.when(i == 0)
    def _():
        tri_scr[...] = (lax.broadcasted_iota(jnp.int32, (tm, tm), 0)
                        < lax.broadcasted_iota(jnp.int32, (tm, tm), 1)).astype(BF16)

    onehot = lax.broadcasted_iota(jnp.int32, (CLASS_ROWS, tm), 0).astype(F32) == cls
    before = jnp.dot(onehot.astype(BF16), tri_scr[...], preferred_element_type=F32) + run_scr[:, 0:1]
    rank = jnp.sum(jnp.where(onehot, before, 0.0), axis=0, keepdims=True)
    run_scr[...] = run_scr[...] + jnp.sum(onehot.astype(F32), axis=1, keepdims=True)
    srow = lax.broadcasted_iota(jnp.int32, (SUBLANES, tm), 0)
    info_ref[...] = jnp.where(srow == 0, cls, jnp.where(srow == 1, rank, 0.0))
    cnt_ref[...] = run_scr[...]


CLASS_ROWS = 32
SUBLANES = 8


def _tile_copy(src, src_row, dst, dst_row, sem):
    return pltpu.make_async_copy(src.at[pl.ds(pl.multiple_of(src_row, SUBLANES), SUBLANES), :],
                                 dst.at[pl.ds(pl.multiple_of(dst_row, SUBLANES), SUBLANES), :], sem)


ROW_UNROLL = 8


def _start_rows(copy, n):
    def body(g, carry):
        for u in range(ROW_UNROLL):
            copy(g * ROW_UNROLL + u).start(priority=u % 2)
        return carry

    lax.fori_loop(0, n // ROW_UNROLL, body, 0)


def _dispatch_kernel(dest_ref, pad_ref, x_ref, xs_ref, rec_scr, zero_scr, sems, zsem):
    i = pl.program_id(0)
    last = pl.num_programs(0) - 1
    tm = x_ref.shape[0]
    rows = tm * SUBLANES
    slot = i % 2
    tile_rows = MOE_TM * SUBLANES

    def zero_copy(c):
        start = pl.multiple_of(pad_ref[c] * SUBLANES, tile_rows)
        return pltpu.make_async_copy(zero_scr, xs_ref.at[pl.ds(start, tile_rows), :], zsem)

    @pl.when(i == 0)
    def _():
        zero_scr[...] = jnp.zeros_like(zero_scr)
        for c in range(pad_ref.shape[0]):
            @pl.when(pad_ref[c] >= 0)
            def _():
                zero_copy(c).start()
        for c in range(pad_ref.shape[0]):
            @pl.when(pad_ref[c] >= 0)
            def _():
                zero_copy(c).wait()

    base = i * tm
    for s in range(2):
        @pl.when(slot == s)
        def _(s=s):
            for j in range(D_MODEL // LANES):
                rec_scr[s, pl.ds(j, tm, stride=SUBLANES), :] = x_ref[:, j * LANES:(j + 1) * LANES]
            _start_rows(lambda r: _tile_copy(rec_scr.at[s], r * SUBLANES, xs_ref, dest_ref[base + r], sems.at[s]),
                        tm)

    def wait_step(s):
        pltpu.make_async_copy(rec_scr.at[s], xs_ref.at[pl.ds(0, rows), :], sems.at[s]).wait()

    @pl.when(i > 0)
    def _():
        wait_step(1 - slot)

    @pl.when(i == last)
    def _():
        wait_step(slot)


def _dispatch(dest, pad_start, x, n_sorted):
    n = x.shape[0]
    tm = ROW_TM
    return pl.pallas_call(
        _dispatch_kernel,
        grid_spec=pltpu.PrefetchScalarGridSpec(
            num_scalar_prefetch=2,
            grid=(n // tm,),
            in_specs=[pl.BlockSpec((tm, D_MODEL), lambda i, d, p: (i, 0))],
            out_specs=pl.BlockSpec(memory_space=pl.ANY),
            scratch_shapes=[pltpu.VMEM((2, tm * SUBLANES, LANES), F32),
                            pltpu.VMEM((MOE_TM * SUBLANES, LANES), F32),
                            pltpu.SemaphoreType.DMA((2,)), pltpu.SemaphoreType.DMA]),
        out_shape=jax.ShapeDtypeStruct((n_sorted * SUBLANES, LANES), F32),
        compiler_params=_cparams(("arbitrary",)),
        name="dispatch",
    )(dest, pad_start, x)


def _expert_kernel(ea_ref, eb_ref, nt_ref, xs_ref, wr_ref, br_ref,
                   wga_ref, wua_ref, wda_ref, wgb_ref, wub_ref, wdb_ref, g_ref, b_ref, ys_ref, hid_scr):
    p = pl.program_id(0)
    tm = MOE_TM

    @pl.when(p >= nt_ref[0])
    def _():
        ys_ref[...] = jnp.zeros_like(ys_ref)

    @pl.when(p < nt_ref[0])
    def _():
        x = jnp.concatenate([xs_ref[pl.ds(j, tm, stride=SUBLANES), :] for j in range(D_MODEL // LANES)],
                            axis=1)
        xb = x.astype(BF16)

        logits = jnp.dot(xb, wr_ref[...], preferred_element_type=F32) + br_ref[...]
        lane = lax.broadcasted_iota(jnp.int32, (tm, LANES), 1)
        lane_a = N_GROUPS + ea_ref[p]
        lane_b = N_GROUPS + eb_ref[p]
        grp = ea_ref[p] // EXPERTS_PER_GROUP
        is_g = lane < N_GROUPS
        mg = jnp.max(jnp.where(is_g, logits, NEG_INF), axis=-1, keepdims=True)
        eg = jnp.where(is_g, jnp.exp(logits - mg), 0.0)
        g_gate = (jnp.sum(jnp.where(lane == grp, eg, 0.0), axis=-1, keepdims=True)
                  / jnp.sum(eg, axis=-1, keepdims=True))
        l_a = jnp.sum(jnp.where(lane == lane_a, logits, 0.0), axis=-1, keepdims=True)
        l_b = jnp.sum(jnp.where(lane == lane_b, logits, 0.0), axis=-1, keepdims=True)
        mx = jnp.maximum(l_a, l_b)
        p_a = jnp.exp(l_a - mx)
        p_b = jnp.exp(l_b - mx)
        scale = g_gate / (p_a + p_b)
        w_a = p_a * scale
        w_b = p_b * scale

        for e, (wg_ref, wu_ref) in enumerate(((wga_ref, wua_ref), (wgb_ref, wub_ref))):
            gate = jnp.dot(xb, wg_ref[...], preferred_element_type=F32)
            up = jnp.dot(xb, wu_ref[...], preferred_element_type=F32)
            hid_scr[e] = (gate * (1.0 / (1.0 + jnp.exp(-gate))) * up).astype(BF16)
        y = (w_a * jnp.dot(hid_scr[0], wda_ref[...], preferred_element_type=F32)
             + w_b * jnp.dot(hid_scr[1], wdb_ref[...], preferred_element_type=F32))
        out = _layer_norm(DEEPNORM_ALPHA * x + y, g_ref[...], b_ref[...])
        for j in range(D_MODEL // LANES):
            ys_ref[pl.ds(j, tm, stride=SUBLANES), :] = out[:, j * LANES:(j + 1) * LANES]


def _experts(layer, tile_ea, tile_eb, n_tiles, xs, w_router, b_router, w_gate, w_up, w_down, g, b):
    tm = MOE_TM
    rows = tm * SUBLANES
    n_grid = xs.shape[0] // rows
    last = lambda p, nt: jnp.maximum(jnp.minimum(p, nt[0] - 1), 0)
    wspec_a = lambda shape: pl.BlockSpec((None, None) + shape, lambda p, ea, eb, nt: (layer, ea[p], 0, 0))
    wspec_b = lambda shape: pl.BlockSpec((None, None) + shape, lambda p, ea, eb, nt: (layer, eb[p], 0, 0))
    const = lambda shape: pl.BlockSpec(shape, lambda p, ea, eb, nt: (0, 0))
    up_shape = (D_MODEL, D_EXPERT)
    dn_shape = (D_EXPERT, D_MODEL)
    return pl.pallas_call(
        _expert_kernel,
        grid_spec=pltpu.PrefetchScalarGridSpec(
            num_scalar_prefetch=3,
            grid=(n_grid,),
            in_specs=[pl.BlockSpec((rows, LANES), lambda p, ea, eb, nt: (last(p, nt), 0)),
                      const((D_MODEL, LANES)), const((1, LANES)),
                      wspec_a(up_shape), wspec_a(up_shape), wspec_a(dn_shape),
                      wspec_b(up_shape), wspec_b(up_shape), wspec_b(dn_shape),
                      const((1, D_MODEL)), const((1, D_MODEL))],
            out_specs=pl.BlockSpec((rows, LANES), lambda p, ea, eb, nt: (p, 0)),
            scratch_shapes=[pltpu.VMEM((2, tm, D_EXPERT), BF16)]),
        out_shape=jax.ShapeDtypeStruct(xs.shape, F32),
        compiler_params=_cparams(("arbitrary",)),
        name="experts",
    )(tile_ea, tile_eb, n_tiles, xs, w_router.astype(BF16), b_router,
      w_gate, w_up, w_down, w_gate, w_up, w_down, g.reshape(1, D_MODEL), b.reshape(1, D_MODEL))


def _gather_kernel(dest_ref, ys_ref, *refs, seg_blocks):
    out_refs = refs[:len(seg_blocks)]
    rec_scr, sems = refs[len(seg_blocks):]
    i = pl.program_id(0)
    n_steps = pl.num_programs(0)
    tm = out_refs[0].shape[0]
    rows = tm * SUBLANES
    slot = i % 2

    def fetch(s, buf):
        base = s * tm
        _start_rows(lambda r: _tile_copy(ys_ref, dest_ref[base + r], rec_scr.at[buf], r * SUBLANES, sems.at[buf]),
                    tm)

    @pl.when(i == 0)
    def _():
        fetch(0, 0)

    for buf in range(2):
        @pl.when(jnp.logical_and(i + 1 < n_steps, (i + 1) % 2 == buf))
        def _(buf=buf):
            fetch(i + 1, buf)

    pltpu.make_async_copy(ys_ref.at[pl.ds(0, rows), :], rec_scr.at[slot], sems.at[slot]).wait()

    start = 0
    for out_ref, nb in zip(out_refs, seg_blocks):
        @pl.when(jnp.logical_and(i >= start, i < start + nb))
        def _(out_ref=out_ref):
            for j in range(D_MODEL // LANES):
                out_ref[:, j * LANES:(j + 1) * LANES] = rec_scr[slot, pl.ds(j, tm, stride=SUBLANES), :]
        start += nb


def _gather_rows(dest, ys, seg_rows):
    tm = ROW_TM
    seg_blocks = tuple(r // tm for r in seg_rows)
    return pl.pallas_call(
        functools.partial(_gather_kernel, seg_blocks=seg_blocks),
        grid_spec=pltpu.PrefetchScalarGridSpec(
            num_scalar_prefetch=1,
            grid=(sum(seg_blocks),),
            in_specs=[pl.BlockSpec(memory_space=pl.ANY)],
            out_specs=_segment_specs(seg_blocks, tm, D_MODEL),
            scratch_shapes=[pltpu.VMEM((2, tm * SUBLANES, LANES), F32), pltpu.SemaphoreType.DMA((2,))]),
        out_shape=[jax.ShapeDtypeStruct((r, D_MODEL), F32) for r in seg_rows],
        compiler_params=_cparams(("arbitrary",)),
        name="gather_rows",
    )(dest, ys)


_PAIR_A = np.array([0, 0, 0, 1, 1, 2], np.int32)
_PAIR_B = np.array([1, 2, 3, 2, 3, 3], np.int32)


def _moe_layer(layer, x, info, counts, w_router, b_router, w_gate, w_up, w_down, g, b, out_rows):
    n = x.shape[0]
    tm = MOE_TM
    n_sorted = n + N_CLASSES * tm
    cls = info[:, 0, :].reshape(n).astype(jnp.int32)
    rank = info[:, 1, :].reshape(n).astype(jnp.int32)
    counts = counts[:N_CLASSES, 0].astype(jnp.int32)
    padded = (counts + tm - 1) // tm * tm
    classes = jnp.arange(N_CLASSES, dtype=jnp.int32)
    ends = jnp.sum(jnp.where(classes[None, :] <= classes[:, None], padded[None, :], 0), axis=1)
    offs = ends - padded
    total = ends[N_CLASSES - 1]
    dest = rank + jnp.sum(jnp.where(cls[:, None] == classes[None, :], offs[None, :], 0), axis=1)
    unused = total + classes * tm
    pad_start = jnp.concatenate([jnp.where(padded > 0, ends - tm, -1),
                                 jnp.where(unused < n_sorted, unused, -1)]).astype(jnp.int32)
    tile_start = jnp.arange(n_sorted // tm, dtype=jnp.int32) * tm
    tile_start = jnp.minimum(tile_start, total - tm)
    tile_cls = jnp.sum((ends[None, :] <= tile_start[:, None]).astype(jnp.int32), axis=1)
    pair = tile_cls % N_PAIRS
    pair_a = jnp.sum(jnp.where(pair[:, None] == np.arange(N_PAIRS)[None, :], _PAIR_A[None, :], 0), axis=1)
    pair_b = jnp.sum(jnp.where(pair[:, None] == np.arange(N_PAIRS)[None, :], _PAIR_B[None, :], 0), axis=1)
    grp = tile_cls // N_PAIRS
    tile_ea = (grp * EXPERTS_PER_GROUP + pair_a).astype(jnp.int32)
    tile_eb = (grp * EXPERTS_PER_GROUP + pair_b).astype(jnp.int32)
    n_tiles = (total // tm).astype(jnp.int32).reshape(1)
    dest_row = (dest * SUBLANES).astype(jnp.int32)
    xs = _dispatch(dest_row, pad_start, x, n_sorted)
    ys = _experts(layer, tile_ea, tile_eb, n_tiles, xs, w_router, b_router, w_gate, w_up, w_down, g, b)
    return _gather_rows(dest_row, ys, out_rows)


_A_ORDER = np.array([0, 4, 1, 5, 2, 6, 3, 7])


def _prep_ab(w_in, w_out):
    qa = w_in[:, :QA_W].reshape(D_MODEL, A_HEADS, HEAD_DIM)[:, _A_ORDER].reshape(D_MODEL, QA_W) * ATTN_SCALE
    kva = w_in[:, QA_W:A_IN]
    qb = w_in[:, A_IN:A_IN + B_W] * ATTN_SCALE
    kvb = w_in[:, A_IN + B_W:]
    w = jnp.concatenate([qa, kva, qb, kvb], axis=1).astype(BF16)
    wo_a = w_out[:QA_W].reshape(A_HEADS, HEAD_DIM, D_MODEL)[_A_ORDER].reshape(QA_W, D_MODEL).astype(BF16)
    wo_b = w_out[QA_W:].astype(BF16)
    return w, wo_a, wo_b


def _prep_c(w_in, w_out):
    w = jnp.concatenate([w_in[:, :C_W] * ATTN_SCALE, w_in[:, C_W:]], axis=1).astype(BF16)
    return w, w_out.astype(BF16)


def _trunk(xs, seg_starts, seg_ends, rel_bias, w_in_ab, a_sink, w_out_ab, w_in_c, c_rpb, w_out_c,
           ln1_g, ln1_b, ln2_g, ln2_b, router_g_w, router_g_b, router_e_w, router_e_b,
           w_gate, w_up, w_down):
    io_rows = tuple(a.shape[0] for a in xs)
    n = sum(io_rows)
    bias_a = _bias_a(rel_bias)
    bias_b = _bias_b(rel_bias)
    w_gate = w_gate.astype(BF16)
    w_up = w_up.astype(BF16)
    w_down = w_down.astype(BF16)
    for l in range(DEPTH):
        i = l // 2
        pad = LANES - N_GROUPS - N_EXPERTS
        w_router = jnp.pad(jnp.concatenate([router_g_w[l], router_e_w[l]], axis=1), ((0, 0), (0, pad)))
        b_router = jnp.pad(jnp.concatenate([router_g_b[l], router_e_b[l]]), (0, pad)).reshape(1, LANES)
        if l % 2 == 0:
            w, wo_a, wo_b = _prep_ab(w_in_ab[i], w_out_ab[i])
            a_qkv, b_qkv = _inproj(xs, w, ((0, A_IN), (A_IN, A_IN + B_IN)), (BF16, F32))
            o_a = _attn_a(a_qkv, bias_a, a_sink[i].astype(F32) * LOG2E, seg_starts, seg_ends)
            o_b = _attn_b(b_qkv, bias_b, seg_starts, seg_ends)
            parts, weights = [o_a, o_b], [wo_a, wo_b]
        else:
            w, wo = _prep_c(w_in_c[i], w_out_c[i])
            (c_qkv,) = _inproj(xs, w, ((0, 3 * C_W),), (BF16,))
            parts, weights = [_attn_c(c_qkv, _bias_c(c_rpb[i]), seg_starts, seg_ends)], [wo]
        x, info, counts = _outproj_ln(parts, weights, xs, ln1_g[l], ln1_b[l], w_router, b_router)
        xs = _moe_layer(l, x, info, counts, w_router, b_router, w_gate, w_up, w_down, ln2_g[l], ln2_b[l],
                        io_rows if l == DEPTH - 1 else (n,))
    return xs


def kernel(x_prompt, x_sample, rel_bias, w_in_ab, a_sink, w_out_ab, w_in_c, c_rpb, w_out_c,
           ln1_g, ln1_b, ln2_g, ln2_b, router_g_w, router_g_b, router_e_w, router_e_b,
           w_gate, w_up, w_down):
    lens = [x_prompt.shape[1]] * x_prompt.shape[0] + [x_sample.shape[1]] * x_sample.shape[0]
    seg_ends = tuple(int(v) for v in np.cumsum(lens))
    seg_starts = tuple(e - n for e, n in zip(seg_ends, lens))
    for n in lens:
        assert n % ATT_TB == 0 and n % A_TB == 0 and n // GRID_W >= NA_ROWS
    xs = [x_prompt.reshape(-1, D_MODEL), x_sample.reshape(-1, D_MODEL)]
    y_p, y_s = _trunk(xs, seg_starts, seg_ends, rel_bias, w_in_ab, a_sink, w_out_ab, w_in_c, c_rpb, w_out_c,
                      ln1_g, ln1_b, ln2_g, ln2_b, router_g_w, router_g_b, router_e_w, router_e_b,
                      w_gate, w_up, w_down)
    return (y_p.reshape(x_prompt.shape), y_s.reshape(x_sample.shape))
```

```python
import functools
import math

import numpy as np
import jax
import jax.numpy as jnp
from jax import lax
from jax.experimental import pallas as pl
from jax.experimental.pallas import tpu as pltpu

F32 = jnp.float32
BF16 = jnp.bfloat16

D_MODEL = 1024
DEPTH = 4
HEAD_DIM = 64
LANES = 128
A_HEADS = 8
A_KV_HEADS = 2
A_WINDOW = 128
B_HEADS = 8
B_BRANCHES = ((128, 1), (512, 4), (2048, 16))
B_HALF = 64
C_HEADS = 16
GRID_W = 64
NA_ROWS = 8
NA_COLS = 16
REL_BUCKETS = 32
REL_MAX_DIST = 1024
N_GROUPS = 4
EXPERTS_PER_GROUP = 4
N_EXPERTS = 16
D_EXPERT = 512
N_PAIRS = 6
N_CLASSES = N_GROUPS * N_PAIRS
DEEPNORM_ALPHA = (2.0 * DEPTH) ** 0.25
LN_EPS = 1e-5
LOG2E = math.log2(math.e)
ATTN_SCALE = HEAD_DIM ** -0.5 * LOG2E
NEG_INF = -1e30

QA_W = A_HEADS * HEAD_DIM
KVA_W = A_KV_HEADS * HEAD_DIM
A_IN = QA_W + 2 * KVA_W
B_W = B_HEADS * HEAD_DIM
B_IN = 3 * B_W
C_W = C_HEADS * HEAD_DIM

ATT_TB = 1024
MM_TM = 512
MOE_TM = 256
ROW_TM = 512
VMEM_LIMIT = 56 * 1024 * 1024


def _cparams(sem):
    return pltpu.CompilerParams(dimension_semantics=sem, vmem_limit_bytes=VMEM_LIMIT)


def _segment_flags(tok0, size, seg_starts, seg_ends):
    is_first = functools.reduce(jnp.logical_or, [tok0 == s for s in seg_starts])
    is_last = functools.reduce(jnp.logical_or, [tok0 + size == e for e in seg_ends])
    return is_first, is_last


def _t5_bucket_np(rel):
    half_b = REL_BUCKETS // 2
    max_exact = half_b // 2
    n = np.abs(rel)
    large = max_exact + (np.log(np.maximum(n, max_exact).astype(np.float32) / max_exact)
                         / math.log(REL_MAX_DIST / max_exact) * (half_b - max_exact)).astype(np.int32)
    large = np.minimum(large, half_b - 1)
    return np.where(rel > 0, half_b, 0) + np.where(n < max_exact, n, large)


def _banded_bias(table, half, dil):
    rel = np.arange(3 * half)[None, :] - half - np.arange(half)[:, None]
    bucket = jnp.asarray(_t5_bucket_np(rel * dil).astype(np.int32))
    hit = bucket[None] == jnp.arange(REL_BUCKETS, dtype=jnp.int32)[:, None, None]
    bias = jnp.sum(jnp.where(hit[:, None], table.astype(F32)[:, :, None, None], 0.0), axis=0)
    return jnp.where(jnp.asarray(np.abs(rel) <= half)[None], bias * LOG2E, NEG_INF)


def _bias_a(rel_bias):
    w = A_WINDOW
    ch = A_CHUNK
    b = _banded_bias(rel_bias[:, :A_HEADS], w, 1).reshape(A_HEADS, w // ch, ch, 3 * w)
    b = b.transpose(1, 0, 2, 3).reshape(w // ch, A_HEADS * ch, 3 * w)
    col = np.arange(3 * w)
    first = jnp.where(jnp.asarray(col < w), NEG_INF, b)
    last = jnp.where(jnp.asarray(col >= 2 * w), NEG_INF, b)
    return jnp.stack([b, first, last])


def _bias_b(rel_bias):
    per = [_banded_bias(rel_bias[:, A_HEADS:], B_HALF, d) for _, d in B_BRANCHES]
    b = jnp.stack(per, axis=1)
    return b.reshape(B_HEADS // 2, 2, len(B_BRANCHES), B_HALF, 3 * B_HALF).transpose(0, 2, 1, 3, 4) \
            .reshape(B_HEADS // 2, len(B_BRANCHES), 2 * B_HALF, 3 * B_HALF)


def _bias_c(rpb):
    gw = GRID_W
    n_dr = 2 * NA_ROWS - 1
    side = gw - NA_COLS
    p = jnp.concatenate([jnp.repeat(rpb[..., :1], side, axis=-1), rpb.astype(F32),
                         jnp.repeat(rpb[..., -1:], side + 1, axis=-1)], axis=-1)
    z = jnp.broadcast_to(p[:, :, None, :], (C_HEADS, n_dr, gw, 2 * gw)).reshape(C_HEADS, n_dr, 2 * gw * gw)
    t = z[:, :, gw - 1:gw - 1 + gw * (2 * gw - 1)].reshape(C_HEADS, n_dr, gw, 2 * gw - 1)[..., :gw]
    cq = np.arange(gw)[:, None]
    w = np.arange(gw)[None, :]
    c0 = np.clip(cq - NA_COLS // 2, 0, gw - NA_COLS)
    t = jnp.where(jnp.asarray((w >= c0) & (w < c0 + NA_COLS)), t * LOG2E, NEG_INF)
    t = t.reshape(C_HEADS // 2, 2, n_dr, gw, gw).transpose(0, 1, 3, 2, 4)
    bias = jnp.stack([t[:, :, :, NA_ROWS - 1 - s:2 * NA_ROWS - 1 - s, :].reshape(C_HEADS // 2, 2, gw, NA_ROWS * gw)
                      for s in range(NA_ROWS)], axis=1)
    return bias.reshape(C_HEADS // 2, NA_ROWS, 2 * gw, NA_ROWS * gw)


def _segment_blocks(segs, tm):
    return tuple(a.shape[0] // tm for a in segs)


def _segment_specs(seg_blocks, tm, width):
    specs, start = [], 0
    for nb in seg_blocks:
        specs.append(pl.BlockSpec((tm, width), lambda i, *_, s=start, nb=nb: (jnp.clip(i - s, 0, nb - 1), 0)))
        start += nb
    return specs


def _segment_rows(i, refs, seg_blocks):
    x, start = refs[0][...], seg_blocks[0]
    for ref, nb in zip(refs[1:], seg_blocks[1:]):
        x = jnp.where(i >= start, ref[...], x)
        start += nb
    return x


def _inproj_kernel(*refs, splits, seg_blocks):
    n_seg = len(seg_blocks)
    w_ref = refs[n_seg]
    o_refs = refs[n_seg + 1:]
    x = _segment_rows(pl.program_id(0), refs[:n_seg], seg_blocks).astype(BF16)
    for o_ref, (lo, hi) in zip(o_refs, splits):
        o_ref[...] = jnp.dot(x, w_ref[:, lo:hi], preferred_element_type=F32).astype(o_ref.dtype)


def _inproj(xs, w, splits, dtypes):
    seg_blocks = _segment_blocks(xs, MM_TM)
    n = sum(seg_blocks) * MM_TM
    return pl.pallas_call(
        functools.partial(_inproj_kernel, splits=splits, seg_blocks=seg_blocks),
        grid=(n // MM_TM,),
        in_specs=_segment_specs(seg_blocks, MM_TM, D_MODEL) + [pl.BlockSpec(w.shape, lambda i: (0, 0))],
        out_specs=[pl.BlockSpec((MM_TM, hi - lo), lambda i: (i, 0)) for lo, hi in splits],
        out_shape=[jax.ShapeDtypeStruct((n, hi - lo), dt) for (lo, hi), dt in zip(splits, dtypes)],
        compiler_params=_cparams(("parallel",)),
        name="inproj",
    )(*xs, w)


def _layer_norm(z, g, b):
    mu = jnp.mean(z, axis=-1, keepdims=True)
    zc = z - mu
    var = jnp.mean(zc * zc, axis=-1, keepdims=True)
    return zc * lax.rsqrt(var + LN_EPS) * g + b


def _outproj_ln_kernel(*refs, n_parts, seg_blocks):
    n_seg = len(seg_blocks)
    o_refs = refs[:n_parts]
    w_refs = refs[n_parts:2 * n_parts]
    x_refs = refs[2 * n_parts:2 * n_parts + n_seg]
    g_ref, b_ref, whl_ref, br_ref, out_ref, info_ref, cnt_ref, run_scr, tri_scr = refs[2 * n_parts + n_seg:]
    i = pl.program_id(0)
    h = DEEPNORM_ALPHA * _segment_rows(i, x_refs, seg_blocks)
    for o_ref, w_ref in zip(o_refs, w_refs):
        h = h + jnp.dot(o_ref[...], w_ref[...], preferred_element_type=F32)
    out = _layer_norm(h, g_ref[...], b_ref[...])
    out_ref[...] = out
    _route(i, out, whl_ref, br_ref, info_ref, cnt_ref, run_scr, tri_scr)


def _outproj_ln(parts, weights, xs, g, b, w_router, b_router):
    tm = MM_TM
    seg_blocks = _segment_blocks(xs, tm)
    n = sum(seg_blocks) * tm
    n_parts = len(parts)
    wh = w_router.astype(BF16)
    whl = jnp.concatenate([wh, (w_router - wh.astype(F32)).astype(BF16)], axis=1)
    const = lambda shape: pl.BlockSpec(shape, lambda i: (0, 0))
    return pl.pallas_call(
        functools.partial(_outproj_ln_kernel, n_parts=n_parts, seg_blocks=seg_blocks),
        grid=(n // tm,),
        in_specs=([pl.BlockSpec((tm, p.shape[1]), lambda i: (i, 0)) for p in parts]
                  + [const(w.shape) for w in weights]
                  + _segment_specs(seg_blocks, tm, D_MODEL)
                  + [const((1, D_MODEL)), const((1, D_MODEL)),
                     const((D_MODEL, 2 * LANES)), const((1, LANES))]),
        out_specs=[pl.BlockSpec((tm, D_MODEL), lambda i: (i, 0)),
                   pl.BlockSpec((None, SUBLANES, tm), lambda i: (i, 0, 0)),
                   const((CLASS_ROWS, LANES))],
        out_shape=[jax.ShapeDtypeStruct((n, D_MODEL), F32),
                   jax.ShapeDtypeStruct((n // tm, SUBLANES, tm), F32),
                   jax.ShapeDtypeStruct((CLASS_ROWS, LANES), F32)],
        scratch_shapes=[pltpu.VMEM((CLASS_ROWS, LANES), F32), pltpu.VMEM((tm, tm), BF16)],
        compiler_params=_cparams(("arbitrary",)),
        name="outproj_ln",
    )(*parts, *weights, *xs, g.reshape(1, D_MODEL), b.reshape(1, D_MODEL), whl, b_router)


ATT_DEPTH = 3
B_DEPTH = 10


def _staged(n, weights, values, depth=ATT_DEPTH):
    for i in range(min(depth, n)):
        weights(i)
    for i in range(n):
        if i + depth < n:
            weights(i + depth)
        values(i)


def _attn_a_kernel(q_ref, kvm_ref, kvp_ref, kvn_ref, bias_ref, sink_ref, o_ref, kv_scr, p_scr,
                   *, seg_starts, seg_ends):
    w = A_WINDOW
    tb = A_TB
    n_sub = tb // w
    tok0 = pl.program_id(0) * tb
    is_first, is_last = _segment_flags(tok0, tb, seg_starts, seg_ends)
    kv_scr[0:w, :] = kvp_ref[...]
    kv_scr[w:w + tb, :] = kvm_ref[...]
    kv_scr[w + tb:, :] = kvn_ref[...]
    low = lax.broadcasted_iota(jnp.int32, (1, LANES), 1) < HEAD_DIM
    ch = A_CHUNK
    per = w // ch

    sinks = [jnp.concatenate([jnp.full((ch, 1), sink_ref[c + 4 * g], F32) for c in range(4)], axis=0)
             for g in range(A_KV_HEADS)]

    sink = jnp.concatenate(sinks, axis=0)

    def scores(t):
        j = t // per
        q = q_ref[t * ch:(t + 1) * ch, :]
        qg = jnp.concatenate([jnp.where(low, q[:, c * LANES:(c + 1) * LANES], 0) for c in range(4)]
                             + [jnp.where(low, 0, q[:, c * LANES:(c + 1) * LANES]) for c in range(4)], axis=0)
        k2 = kv_scr[j * w:(j + 3) * w, :LANES]
        return lax.dot_general(qg, k2, (((1,), (1,)), ((), ())), preferred_element_type=F32)

    rows = A_HEADS * ch
    rdens = {}

    def weights(t):
        j = t // per
        if j == 0:
            variant = jnp.where(is_first, 1, 0)
        elif j == n_sub - 1:
            variant = jnp.where(is_last, 2, 0)
        else:
            variant = 0
        s = scores(t) + bias_ref[variant, t % per]
        m = jnp.maximum(jnp.max(s, axis=-1, keepdims=True), sink)
        e = jnp.exp2(s - m)
        rdens[t] = 1.0 / (jnp.sum(e, axis=-1, keepdims=True) + jnp.exp2(sink - m))
        p_scr[t * rows:(t + 1) * rows, :] = e.astype(BF16)

    def values(t):
        j = t // per
        v2 = kv_scr[j * w:(j + 3) * w, LANES:]
        pv = jnp.dot(p_scr[t * rows:(t + 1) * rows, :], v2, preferred_element_type=F32) * rdens[t]
        for c in range(4):
            oc = jnp.where(low, pv[c * ch:(c + 1) * ch], pv[(4 + c) * ch:(5 + c) * ch])
            o_ref[t * ch:(t + 1) * ch, c * LANES:(c + 1) * LANES] = oc.astype(o_ref.dtype)

    _staged(tb // ch, weights, values)


A_CHUNK = 128
A_TB = 2 * ATT_TB


def _attn_a(a_qkv, bias, sink, seg_starts, seg_ends):
    n = a_qkv.shape[0]
    w = A_WINDOW
    tb = A_TB
    sub = tb // w
    nhb = n // w
    kv_col = QA_W // (2 * LANES)
    return pl.pallas_call(
        functools.partial(_attn_a_kernel, seg_starts=seg_starts, seg_ends=seg_ends),
        grid=(n // tb,),
        in_specs=[pl.BlockSpec((tb, QA_W), lambda i: (i, 0)),
                  pl.BlockSpec((tb, 2 * LANES), lambda i: (i, kv_col)),
                  pl.BlockSpec((w, 2 * LANES), lambda i: (jnp.maximum(i * sub - 1, 0), kv_col)),
                  pl.BlockSpec((w, 2 * LANES), lambda i: (jnp.minimum((i + 1) * sub, nhb - 1), kv_col)),
                  pl.BlockSpec(bias.shape, lambda i: (0,) * bias.ndim),
                  pl.BlockSpec(memory_space=pltpu.SMEM)],
        out_specs=pl.BlockSpec((tb, QA_W), lambda i: (i, 0)),
        out_shape=jax.ShapeDtypeStruct((n, QA_W), BF16),
        scratch_shapes=[pltpu.VMEM((tb + 2 * w, 2 * LANES), BF16),
                        pltpu.VMEM((A_HEADS * tb, 3 * w), BF16)],
        compiler_params=_cparams(("parallel",)),
        name="attn_a",
    )(a_qkv, a_qkv, a_qkv, a_qkv, bias, sink)


def _attn_b_kernel(q_ref, kp_ref, km_ref, kn_ref, vp_ref, vm_ref, vn_ref, bias_ref, o_ref,
                   k_scr, v_scr, o_scr, m_scr, l_scr, p_scr, *, seg_starts, seg_ends):
    tb = B_TB
    h = B_HALF
    tok0 = pl.program_id(1) * tb
    is_first, is_last = _segment_flags(tok0, tb, seg_starts, seg_ends)
    k_scr[0:tb, :] = kp_ref[...]
    k_scr[tb:2 * tb, :] = km_ref[...]
    k_scr[2 * tb:, :] = kn_ref[...]
    v_scr[0:tb, :] = vp_ref[...]
    v_scr[tb:2 * tb, :] = vm_ref[...]
    v_scr[2 * tb:, :] = vn_ref[...]
    lane = lax.broadcasted_iota(jnp.int32, (1, LANES), 1)
    low = lane < HEAD_DIM
    col = lax.broadcasted_iota(jnp.int32, (1, 3 * h), 1)

    pen_first = jnp.where(jnp.logical_and(col < h, is_first), NEG_INF, 0.0)
    pen_last = jnp.where(jnp.logical_and(col >= 2 * h, is_last), NEG_INF, 0.0)

    def slices(d, r, b):
        row0 = r + h * d * b
        if d == 1:
            return pl.ds(row0, h), pl.ds(tb + row0 - h, 3 * h)
        return pl.ds(row0, h, stride=d), pl.ds(tb + row0 - h * d, 3 * h, stride=d)

    def scores(br, d, r, b):
        qs, ks = slices(d, r, b)
        q = q_ref[qs, :].astype(BF16)
        k = k_scr[ks, :].astype(BF16)
        qq = jnp.concatenate([jnp.where(low, q, 0), jnp.where(low, 0, q)], axis=0)
        return lax.dot_general(qq, k, (((1,), (1,)), ((), ())), preferred_element_type=F32)

    tiles = [(br, d, r, b) for br, (_, d) in enumerate(B_BRANCHES)
             for r in range(d) for b in range(tb // (h * d))]

    def weights(i):
        br, d, r, b = tiles[i]
        qs, _ = slices(d, r, b)
        s = scores(br, d, r, b) + bias_ref[br]
        if b == 0:
            s = s + pen_first
        if b == tb // (h * d) - 1:
            s = s + pen_last
        m = jnp.max(s, axis=-1, keepdims=True)
        e = jnp.exp2(s - m)
        l = jnp.sum(e, axis=-1, keepdims=True)
        p_scr[i * 2 * h:(i + 1) * 2 * h, :] = e.astype(BF16)
        m_scr[br, qs, :] = jnp.where(low, m[:h], m[h:])
        l_scr[br, qs, :] = jnp.where(low, l[:h], l[h:])

    def values(i):
        br, d, r, b = tiles[i]
        qs, ks = slices(d, r, b)
        v = v_scr[ks, :].astype(BF16)
        pv = jnp.dot(p_scr[i * 2 * h:(i + 1) * 2 * h, :], v, preferred_element_type=F32)
        o_scr[br, qs, :] = jnp.where(low, pv[:h], pv[h:])

    _staged(len(tiles), weights, values, B_DEPTH)

    m_all = jnp.maximum(jnp.maximum(m_scr[0], m_scr[1]), m_scr[2])
    num = jnp.zeros((tb, LANES), F32)
    den = jnp.zeros((tb, LANES), F32)
    for br in range(len(B_BRANCHES)):
        a = jnp.exp2(m_scr[br] - m_all)
        num = num + a * o_scr[br]
        den = den + a * l_scr[br]
    o_ref[...] = (num / den).astype(o_ref.dtype)


B_TB = ATT_TB


def _attn_b(b_qkv, bias, seg_starts, seg_ends):
    n = b_qkv.shape[0]
    tb = B_TB
    nblk = n // tb
    npair = B_HEADS // 2
    prev = lambda i: jnp.maximum(i - 1, 0)
    nxt = lambda i: jnp.minimum(i + 1, nblk - 1)
    blk = lambda rowf, off: pl.BlockSpec((tb, LANES), lambda c, i: (rowf(i), off + c))
    same = lambda i: i
    stat = pltpu.VMEM((len(B_BRANCHES), tb, LANES), F32)
    return pl.pallas_call(
        functools.partial(_attn_b_kernel, seg_starts=seg_starts, seg_ends=seg_ends),
        grid=(npair, nblk),
        in_specs=[blk(same, 0),
                  blk(prev, npair), blk(same, npair), blk(nxt, npair),
                  blk(prev, 2 * npair), blk(same, 2 * npair), blk(nxt, 2 * npair),
                  pl.BlockSpec((None,) + bias.shape[1:], lambda c, i: (c, 0, 0, 0))],
        out_specs=pl.BlockSpec((tb, LANES), lambda c, i: (i, c)),
        out_shape=jax.ShapeDtypeStruct((n, B_W), BF16),
        scratch_shapes=[pltpu.VMEM((3 * tb, LANES), F32), pltpu.VMEM((3 * tb, LANES), F32),
                        stat, stat, stat,
                        pltpu.VMEM((len(B_BRANCHES) * 2 * tb, 3 * B_HALF), BF16)],
        compiler_params=_cparams(("parallel", "parallel")),
        name="attn_b",
    )(b_qkv, b_qkv, b_qkv, b_qkv, b_qkv, b_qkv, b_qkv, bias)


C_HALO = (NA_ROWS // 2) * GRID_W

def _attn_c_kernel(q_ref, kp_ref, km_ref, kn_ref, vp_ref, vm_ref, vn_ref, bias_ref, o_ref,
                   k_scr, v_scr, p_scr, *, seg_starts, seg_ends):
    tb = ATT_TB
    gw = GRID_W
    nkeys = NA_ROWS * gw
    tok0 = pl.program_id(1) * tb
    k_scr[0:C_HALO, :] = kp_ref[...]
    k_scr[C_HALO:C_HALO + tb, :] = km_ref[...]
    k_scr[C_HALO + tb:, :] = kn_ref[...]
    v_scr[0:C_HALO, :] = vp_ref[...]
    v_scr[C_HALO:C_HALO + tb, :] = vm_ref[...]
    v_scr[C_HALO + tb:, :] = vn_ref[...]
    seg_row0 = jnp.int32(0)
    seg_rows = jnp.int32(0)
    for s, e in zip(seg_starts, seg_ends):
        inside = jnp.logical_and(tok0 >= s, tok0 < e)
        seg_row0 = jnp.where(inside, s // gw, seg_row0)
        seg_rows = jnp.where(inside, (e - s) // gw, seg_rows)
    lane = lax.broadcasted_iota(jnp.int32, (1, LANES), 1)
    low = lane < HEAD_DIM

    def window(rr):
        rs = tok0 // gw + rr - seg_row0
        start = jnp.clip(rs - NA_ROWS // 2, 0, seg_rows - NA_ROWS)
        shift = rs - start
        return shift, pl.ds(pl.multiple_of((rr + NA_ROWS // 2 - shift) * gw, gw), nkeys)

    def scores(rr):
        _, ks = window(rr)
        q = q_ref[rr * gw:(rr + 1) * gw, :]
        qq = jnp.concatenate([jnp.where(low, q, 0), jnp.where(low, 0, q)], axis=0)
        return lax.dot_general(qq, k_scr[ks, :], (((1,), (1,)), ((), ())), preferred_element_type=F32)

    n_rows = tb // gw
    rdens = {}

    def weights(rr):
        shift, _ = window(rr)
        s = scores(rr) + bias_ref[shift]
        m = jnp.max(s, axis=-1, keepdims=True)
        e = jnp.exp2(s - m)
        rdens[rr] = 1.0 / jnp.sum(e, axis=-1, keepdims=True)
        p_scr[rr * 2 * gw:(rr + 1) * 2 * gw, :] = e.astype(BF16)

    def values(rr):
        _, ks = window(rr)
        pv = jnp.dot(p_scr[rr * 2 * gw:(rr + 1) * 2 * gw, :], v_scr[ks, :], preferred_element_type=F32) * rdens[rr]
        o_ref[rr * gw:(rr + 1) * gw, :] = jnp.where(low, pv[:gw], pv[gw:]).astype(o_ref.dtype)

    _staged(n_rows, weights, values)


def _attn_c(c_qkv, bias, seg_starts, seg_ends):
    n = c_qkv.shape[0]
    tb = ATT_TB
    npair = C_HEADS // 2
    sub = tb // C_HALO
    nhb = n // C_HALO
    main = lambda off: pl.BlockSpec((tb, LANES), lambda c, i: (i, off + c))
    prev = lambda off: pl.BlockSpec((C_HALO, LANES), lambda c, i: (jnp.maximum(i * sub - 1, 0), off + c))
    nxt = lambda off: pl.BlockSpec((C_HALO, LANES),
                                   lambda c, i: (jnp.minimum((i + 1) * sub, nhb - 1), off + c))
    return pl.pallas_call(
        functools.partial(_attn_c_kernel, seg_starts=seg_starts, seg_ends=seg_ends),
        grid=(npair, n // tb),
        in_specs=[main(0),
                  prev(npair), main(npair), nxt(npair),
                  prev(2 * npair), main(2 * npair), nxt(2 * npair),
                  pl.BlockSpec((None,) + bias.shape[1:], lambda c, i: (c, 0, 0, 0))],
        out_specs=pl.BlockSpec((tb, LANES), lambda c, i: (i, c)),
        out_shape=jax.ShapeDtypeStruct((n, C_W), BF16),
        scratch_shapes=[pltpu.VMEM((tb + 2 * C_HALO, LANES), BF16),
                        pltpu.VMEM((tb + 2 * C_HALO, LANES), BF16),
                        pltpu.VMEM((2 * tb, NA_ROWS * GRID_W), BF16)],
        compiler_params=_cparams(("parallel", "parallel")),
        name="attn_c",
    )(c_qkv, c_qkv, c_qkv, c_qkv, c_qkv, c_qkv, c_qkv, bias)


def _route(i, x, whl_ref, b_ref, info_ref, cnt_ref, run_scr, tri_scr):
    tm = x.shape[0]

    @pl.when(i == 0)
    def _():
        run_scr[...] = jnp.zeros_like(run_scr)

    xh = x.astype(BF16)
    xl = (x - xh.astype(F32)).astype(BF16)
    hh_hl = jnp.dot(xh, whl_ref[...], preferred_element_type=F32)
    logits = (hh_hl[:, :LANES] + jnp.dot(xl, whl_ref[:, :LANES], preferred_element_type=F32)
              + hh_hl[:, LANES:]) + b_ref[...]
    lt = logits.T
    first = lambda hit, n: jnp.min(jnp.where(hit, lax.broadcasted_iota(jnp.int32, (n, tm), 0).astype(F32),
                                             float(n)), axis=0, keepdims=True)
    lg = lt[0:N_GROUPS]
    g_sel = first(lg == jnp.max(lg, axis=0, keepdims=True), N_GROUPS)
    le = jnp.zeros((EXPERTS_PER_GROUP, tm), F32)
    for g in range(N_GROUPS):
        lo = N_GROUPS + g * EXPERTS_PER_GROUP
        le = jnp.where(g_sel == g, lt[lo:lo + EXPERTS_PER_GROUP], le)
    row = lax.broadcasted_iota(jnp.int32, (EXPERTS_PER_GROUP, tm), 0).astype(F32)
    i1 = first(le == jnp.max(le, axis=0, keepdims=True), EXPERTS_PER_GROUP)
    rest = jnp.where(row == i1, NEG_INF, le)
    i2 = first(jnp.logical_and(rest == jnp.max(rest, axis=0, keepdims=True), row != i1), EXPERTS_PER_GROUP)
    a = jnp.minimum(i1, i2)
    b = jnp.maximum(i1, i2)
    cls = g_sel * N_PAIRS + a * 3.0 - jnp.where(a == 2.0, 1.0, 0.0) + (b - a - 1.0)

    @pl.when(i == 0)
    def _():
        tri_scr[...] = (lax.broadcasted_iota(jnp.int32, (tm, tm), 0)
                        < lax.broadcasted_iota(jnp.int32, (tm, tm), 1)).astype(BF16)

    onehot = lax.broadcasted_iota(jnp.int32, (CLASS_ROWS, tm), 0).astype(F32) == cls
    before = jnp.dot(onehot.astype(BF16), tri_scr[...], preferred_element_type=F32) + run_scr[:, 0:1]
    rank = jnp.sum(jnp.where(onehot, before, 0.0), axis=0, keepdims=True)
    run_scr[...] = run_scr[...] + jnp.sum(onehot.astype(F32), axis=1, keepdims=True)
    srow = lax.broadcasted_iota(jnp.int32, (SUBLANES, tm), 0)
    info_ref[...] = jnp.where(srow == 0, cls, jnp.where(srow == 1, rank, 0.0))
    cnt_ref[...] = run_scr[...]


CLASS_ROWS = 32
SUBLANES = 8


def _tile_copy(src, src_row, dst, dst_row, sem):
    return pltpu.make_async_copy(src.at[pl.ds(pl.multiple_of(src_row, SUBLANES), SUBLANES), :],
                                 dst.at[pl.ds(pl.multiple_of(dst_row, SUBLANES), SUBLANES), :], sem)


ROW_UNROLL = 8


def _start_rows(copy, n):
    def body(g, carry):
        for u in range(ROW_UNROLL):
            copy(g * ROW_UNROLL + u).start(priority=u % 2)
        return carry

    lax.fori_loop(0, n // ROW_UNROLL, body, 0)


def _dispatch_kernel(dest_ref, pad_ref, x_ref, xs_ref, rec_scr, zero_scr, sems, zsem):
    i = pl.program_id(0)
    last = pl.num_programs(0) - 1
    tm = x_ref.shape[0]
    rows = tm * SUBLANES
    slot = i % 2
    tile_rows = MOE_TM * SUBLANES

    def zero_copy(c):
        start = pl.multiple_of(pad_ref[c] * SUBLANES, tile_rows)
        return pltpu.make_async_copy(zero_scr, xs_ref.at[pl.ds(start, tile_rows), :], zsem)

    @pl.when(i == 0)
    def _():
        zero_scr[...] = jnp.zeros_like(zero_scr)
        for c in range(pad_ref.shape[0]):
            @pl.when(pad_ref[c] >= 0)
            def _():
                zero_copy(c).start()
        for c in range(pad_ref.shape[0]):
            @pl.when(pad_ref[c] >= 0)
            def _():
                zero_copy(c).wait()

    base = i * tm
    for s in range(2):
        @pl.when(slot == s)
        def _(s=s):
            for j in range(D_MODEL // LANES):
                rec_scr[s, pl.ds(j, tm, stride=SUBLANES), :] = x_ref[:, j * LANES:(j + 1) * LANES]
            _start_rows(lambda r: _tile_copy(rec_scr.at[s], r * SUBLANES, xs_ref, dest_ref[base + r], sems.at[s]),
                        tm)

    def wait_step(s):
        pltpu.make_async_copy(rec_scr.at[s], xs_ref.at[pl.ds(0, rows), :], sems.at[s]).wait()

    @pl.when(i > 0)
    def _():
        wait_step(1 - slot)

    @pl.when(i == last)
    def _():
        wait_step(slot)


def _dispatch(dest, pad_start, x, n_sorted):
    n = x.shape[0]
    tm = ROW_TM
    return pl.pallas_call(
        _dispatch_kernel,
        grid_spec=pltpu.PrefetchScalarGridSpec(
            num_scalar_prefetch=2,
            grid=(n // tm,),
            in_specs=[pl.BlockSpec((tm, D_MODEL), lambda i, d, p: (i, 0))],
            out_specs=pl.BlockSpec(memory_space=pl.ANY),
            scratch_shapes=[pltpu.VMEM((2, tm * SUBLANES, LANES), F32),
                            pltpu.VMEM((MOE_TM * SUBLANES, LANES), F32),
                            pltpu.SemaphoreType.DMA((2,)), pltpu.SemaphoreType.DMA]),
        out_shape=jax.ShapeDtypeStruct((n_sorted * SUBLANES, LANES), F32),
        compiler_params=_cparams(("arbitrary",)),
        name="dispatch",
    )(dest, pad_start, x)


def _expert_kernel(ea_ref, eb_ref, nt_ref, xs_ref, wr_ref, br_ref,
                   wga_ref, wua_ref, wda_ref, wgb_ref, wub_ref, wdb_ref, g_ref, b_ref, ys_ref, hid_scr):
    p = pl.program_id(0)
    tm = MOE_TM

    @pl.when(p >= nt_ref[0])
    def _():
        ys_ref[...] = jnp.zeros_like(ys_ref)

    @pl.when(p < nt_ref[0])
    def _():
        x = jnp.concatenate([xs_ref[pl.ds(j, tm, stride=SUBLANES), :] for j in range(D_MODEL // LANES)],
                            axis=1)
        xb = x.astype(BF16)

        logits = jnp.dot(xb, wr_ref[...], preferred_element_type=F32) + br_ref[...]
        lane = lax.broadcasted_iota(jnp.int32, (tm, LANES), 1)
        lane_a = N_GROUPS + ea_ref[p]
        lane_b = N_GROUPS + eb_ref[p]
        grp = ea_ref[p] // EXPERTS_PER_GROUP
        is_g = lane < N_GROUPS
        mg = jnp.max(jnp.where(is_g, logits, NEG_INF), axis=-1, keepdims=True)
        eg = jnp.where(is_g, jnp.exp(logits - mg), 0.0)
        g_gate = (jnp.sum(jnp.where(lane == grp, eg, 0.0), axis=-1, keepdims=True)
                  / jnp.sum(eg, axis=-1, keepdims=True))
        l_a = jnp.sum(jnp.where(lane == lane_a, logits, 0.0), axis=-1, keepdims=True)
        l_b = jnp.sum(jnp.where(lane == lane_b, logits, 0.0), axis=-1, keepdims=True)
        mx = jnp.maximum(l_a, l_b)
        p_a = jnp.exp(l_a - mx)
        p_b = jnp.exp(l_b - mx)
        scale = g_gate / (p_a + p_b)
        w_a = p_a * scale
        w_b = p_b * scale

        for e, (wg_ref, wu_ref) in enumerate(((wga_ref, wua_ref), (wgb_ref, wub_ref))):
            gate = jnp.dot(xb, wg_ref[...], preferred_element_type=F32)
            up = jnp.dot(xb, wu_ref[...], preferred_element_type=F32)
            hid_scr[e] = (gate * (1.0 / (1.0 + jnp.exp(-gate))) * up).astype(BF16)
        y = (w_a * jnp.dot(hid_scr[0], wda_ref[...], preferred_element_type=F32)
             + w_b * jnp.dot(hid_scr[1], wdb_ref[...], preferred_element_type=F32))
        out = _layer_norm(DEEPNORM_ALPHA * x + y, g_ref[...], b_ref[...])
        for j in range(D_MODEL // LANES):
            ys_ref[pl.ds(j, tm, stride=SUBLANES), :] = out[:, j * LANES:(j + 1) * LANES]


def _experts(layer, tile_ea, tile_eb, n_tiles, xs, w_router, b_router, w_gate, w_up, w_down, g, b):
    tm = MOE_TM
    rows = tm * SUBLANES
    n_grid = xs.shape[0] // rows
    last = lambda p, nt: jnp.maximum(jnp.minimum(p, nt[0] - 1), 0)
    wspec_a = lambda shape: pl.BlockSpec((None, None) + shape, lambda p, ea, eb, nt: (layer, ea[p], 0, 0))
    wspec_b = lambda shape: pl.BlockSpec((None, None) + shape, lambda p, ea, eb, nt: (layer, eb[p], 0, 0))
    const = lambda shape: pl.BlockSpec(shape, lambda p, ea, eb, nt: (0, 0))
    up_shape = (D_MODEL, D_EXPERT)
    dn_shape = (D_EXPERT, D_MODEL)
    return pl.pallas_call(
        _expert_kernel,
        grid_spec=pltpu.PrefetchScalarGridSpec(
            num_scalar_prefetch=3,
            grid=(n_grid,),
            in_specs=[pl.BlockSpec((rows, LANES), lambda p, ea, eb, nt: (last(p, nt), 0)),
                      const((D_MODEL, LANES)), const((1, LANES)),
                      wspec_a(up_shape), wspec_a(up_shape), wspec_a(dn_shape),
                      wspec_b(up_shape), wspec_b(up_shape), wspec_b(dn_shape),
                      const((1, D_MODEL)), const((1, D_MODEL))],
            out_specs=pl.BlockSpec((rows, LANES), lambda p, ea, eb, nt: (p, 0)),
            scratch_shapes=[pltpu.VMEM((2, tm, D_EXPERT), BF16)]),
        out_shape=jax.ShapeDtypeStruct(xs.shape, F32),
        compiler_params=_cparams(("arbitrary",)),
        name="experts",
    )(tile_ea, tile_eb, n_tiles, xs, w_router.astype(BF16), b_router,
      w_gate, w_up, w_down, w_gate, w_up, w_down, g.reshape(1, D_MODEL), b.reshape(1, D_MODEL))


def _gather_kernel(dest_ref, ys_ref, *refs, seg_blocks):
    out_refs = refs[:len(seg_blocks)]
    rec_scr, sems = refs[len(seg_blocks):]
    i = pl.program_id(0)
    n_steps = pl.num_programs(0)
    tm = out_refs[0].shape[0]
    rows = tm * SUBLANES
    slot = i % 2

    def fetch(s, buf):
        base = s * tm
        _start_rows(lambda r: _tile_copy(ys_ref, dest_ref[base + r], rec_scr.at[buf], r * SUBLANES, sems.at[buf]),
                    tm)

    @pl.when(i == 0)
    def _():
        fetch(0, 0)

    for buf in range(2):
        @pl.when(jnp.logical_and(i + 1 < n_steps, (i + 1) % 2 == buf))
        def _(buf=buf):
            fetch(i + 1, buf)

    pltpu.make_async_copy(ys_ref.at[pl.ds(0, rows), :], rec_scr.at[slot], sems.at[slot]).wait()

    start = 0
    for out_ref, nb in zip(out_refs, seg_blocks):
        @pl.when(jnp.logical_and(i >= start, i < start + nb))
        def _(out_ref=out_ref):
            for j in range(D_MODEL // LANES):
                out_ref[:, j * LANES:(j + 1) * LANES] = rec_scr[slot, pl.ds(j, tm, stride=SUBLANES), :]
        start += nb


def _gather_rows(dest, ys, seg_rows):
    tm = ROW_TM
    seg_blocks = tuple(r // tm for r in seg_rows)
    return pl.pallas_call(
        functools.partial(_gather_kernel, seg_blocks=seg_blocks),
        grid_spec=pltpu.PrefetchScalarGridSpec(
            num_scalar_prefetch=1,
            grid=(sum(seg_blocks),),
            in_specs=[pl.BlockSpec(memory_space=pl.ANY)],
            out_specs=_segment_specs(seg_blocks, tm, D_MODEL),
            scratch_shapes=[pltpu.VMEM((2, tm * SUBLANES, LANES), F32), pltpu.SemaphoreType.DMA((2,))]),
        out_shape=[jax.ShapeDtypeStruct((r, D_MODEL), F32) for r in seg_rows],
        compiler_params=_cparams(("arbitrary",)),
        name="gather_rows",
    )(dest, ys)


_PAIR_A = np.array([0, 0, 0, 1, 1, 2], np.int32)
_PAIR_B = np.array([1, 2, 3, 2, 3, 3], np.int32)


def _moe_layer(layer, x, info, counts, w_router, b_router, w_gate, w_up, w_down, g, b, out_rows):
    n = x.shape[0]
    tm = MOE_TM
    n_sorted = n + N_CLASSES * tm
    cls = info[:, 0, :].reshape(n).astype(jnp.int32)
    rank = info[:, 1, :].reshape(n).astype(jnp.int32)
    counts = counts[:N_CLASSES, 0].astype(jnp.int32)
    padded = (counts + tm - 1) // tm * tm
    classes = jnp.arange(N_CLASSES, dtype=jnp.int32)
    ends = jnp.sum(jnp.where(classes[None, :] <= classes[:, None], padded[None, :], 0), axis=1)
    offs = ends - padded
    total = ends[N_CLASSES - 1]
    dest = rank + jnp.sum(jnp.where(cls[:, None] == classes[None, :], offs[None, :], 0), axis=1)
    unused = total + classes * tm
    pad_start = jnp.concatenate([jnp.where(padded > 0, ends - tm, -1),
                                 jnp.where(unused < n_sorted, unused, -1)]).astype(jnp.int32)
    tile_start = jnp.arange(n_sorted // tm, dtype=jnp.int32) * tm
    tile_start = jnp.minimum(tile_start, total - tm)
    tile_cls = jnp.sum((ends[None, :] <= tile_start[:, None]).astype(jnp.int32), axis=1)
    pair = tile_cls % N_PAIRS
    pair_a = jnp.sum(jnp.where(pair[:, None] == np.arange(N_PAIRS)[None, :], _PAIR_A[None, :], 0), axis=1)
    pair_b = jnp.sum(jnp.where(pair[:, None] == np.arange(N_PAIRS)[None, :], _PAIR_B[None, :], 0), axis=1)
    grp = tile_cls // N_PAIRS
    tile_ea = (grp * EXPERTS_PER_GROUP + pair_a).astype(jnp.int32)
    tile_eb = (grp * EXPERTS_PER_GROUP + pair_b).astype(jnp.int32)
    n_tiles = (total // tm).astype(jnp.int32).reshape(1)
    dest_row = (dest * SUBLANES).astype(jnp.int32)
    xs = _dispatch(dest_row, pad_start, x, n_sorted)
    ys = _experts(layer, tile_ea, tile_eb, n_tiles, xs, w_router, b_router, w_gate, w_up, w_down, g, b)
    return _gather_rows(dest_row, ys, out_rows)


_A_ORDER = np.array([0, 4, 1, 5, 2, 6, 3, 7])


def _prep_ab(w_in, w_out):
    qa = w_in[:, :QA_W].reshape(D_MODEL, A_HEADS, HEAD_DIM)[:, _A_ORDER].reshape(D_MODEL, QA_W) * ATTN_SCALE
    kva = w_in[:, QA_W:A_IN]
    qb = w_in[:, A_IN:A_IN + B_W] * ATTN_SCALE
    kvb = w_in[:, A_IN + B_W:]
    w = jnp.concatenate([qa, kva, qb, kvb], axis=1).astype(BF16)
    wo_a = w_out[:QA_W].reshape(A_HEADS, HEAD_DIM, D_MODEL)[_A_ORDER].reshape(QA_W, D_MODEL).astype(BF16)
    wo_b = w_out[QA_W:].astype(BF16)
    return w, wo_a, wo_b


def _prep_c(w_in, w_out):
    w = jnp.concatenate([w_in[:, :C_W] * ATTN_SCALE, w_in[:, C_W:]], axis=1).astype(BF16)
    return w, w_out.astype(BF16)


def _trunk(xs, seg_starts, seg_ends, rel_bias, w_in_ab, a_sink, w_out_ab, w_in_c, c_rpb, w_out_c,
           ln1_g, ln1_b, ln2_g, ln2_b, router_g_w, router_g_b, router_e_w, router_e_b,
           w_gate, w_up, w_down):
    io_rows = tuple(a.shape[0] for a in xs)
    n = sum(io_rows)
    bias_a = _bias_a(rel_bias)
    bias_b = _bias_b(rel_bias)
    w_gate = w_gate.astype(BF16)
    w_up = w_up.astype(BF16)
    w_down = w_down.astype(BF16)
    for l in range(DEPTH):
        i = l // 2
        pad = LANES - N_GROUPS - N_EXPERTS
        w_router = jnp.pad(jnp.concatenate([router_g_w[l], router_e_w[l]], axis=1), ((0, 0), (0, pad)))
        b_router = jnp.pad(jnp.concatenate([router_g_b[l], router_e_b[l]]), (0, pad)).reshape(1, LANES)
        if l % 2 == 0:
            w, wo_a, wo_b = _prep_ab(w_in_ab[i], w_out_ab[i])
            a_qkv, b_qkv = _inproj(xs, w, ((0, A_IN), (A_IN, A_IN + B_IN)), (BF16, F32))
            o_a = _attn_a(a_qkv, bias_a, a_sink[i].astype(F32) * LOG2E, seg_starts, seg_ends)
            o_b = _attn_b(b_qkv, bias_b, seg_starts, seg_ends)
            parts, weights = [o_a, o_b], [wo_a, wo_b]
        else:
            w, wo = _prep_c(w_in_c[i], w_out_c[i])
            (c_qkv,) = _inproj(xs, w, ((0, 3 * C_W),), (BF16,))
            parts, weights = [_attn_c(c_qkv, _bias_c(c_rpb[i]), seg_starts, seg_ends)], [wo]
        x, info, counts = _outproj_ln(parts, weights, xs, ln1_g[l], ln1_b[l], w_router, b_router)
        xs = _moe_layer(l, x, info, counts, w_router, b_router, w_gate, w_up, w_down, ln2_g[l], ln2_b[l],
                        io_rows if l == DEPTH - 1 else (n,))
    return xs


def kernel(x_prompt, x_sample, rel_bias, w_in_ab, a_sink, w_out_ab, w_in_c, c_rpb, w_out_c,
           ln1_g, ln1_b, ln2_g, ln2_b, router_g_w, router_g_b, router_e_w, router_e_b,
           w_gate, w_up, w_down):
    lens = [x_prompt.shape[1]] * x_prompt.shape[0] + [x_sample.shape[1]] * x_sample.shape[0]
    seg_ends = tuple(int(v) for v in np.cumsum(lens))
    seg_starts = tuple(e - n for e, n in zip(seg_ends, lens))
    for n in lens:
        assert n % ATT_TB == 0 and n % A_TB == 0 and n // GRID_W >= NA_ROWS
    xs = [x_prompt.reshape(-1, D_MODEL), x_sample.reshape(-1, D_MODEL)]
    y_p, y_s = _trunk(xs, seg_starts, seg_ends, rel_bias, w_in_ab, a_sink, w_out_ab, w_in_c, c_rpb, w_out_c,
                      ln1_g, ln1_b, ln2_g, ln2_b, router_g_w, router_g_b, router_e_w, router_e_b,
                      w_gate, w_up, w_down)
    return (y_p.reshape(x_prompt.shape), y_s.reshape(x_sample.shape))
```

```python
import functools
import math

import numpy as np
import jax
import jax.numpy as jnp
from jax import lax
from jax.experimental import pallas as pl
from jax.experimental.pallas import tpu as pltpu

F32 = jnp.float32
BF16 = jnp.bfloat16

D_MODEL = 1024
DEPTH = 4
HEAD_DIM = 64
LANES = 128
A_HEADS = 8
A_KV_HEADS = 2
A_WINDOW = 128
B_HEADS = 8
B_BRANCHES = ((128, 1), (512, 4), (2048, 16))
B_HALF = 64
C_HEADS = 16
GRID_W = 64
NA_ROWS = 8
NA_COLS = 16
REL_BUCKETS = 32
REL_MAX_DIST = 1024
N_GROUPS = 4
EXPERTS_PER_GROUP = 4
N_EXPERTS = 16
D_EXPERT = 512
N_PAIRS = 6
N_CLASSES = N_GROUPS * N_PAIRS
DEEPNORM_ALPHA = (2.0 * DEPTH) ** 0.25
LN_EPS = 1e-5
LOG2E = math.log2(math.e)
ATTN_SCALE = HEAD_DIM ** -0.5 * LOG2E
NEG_INF = -1e30

QA_W = A_HEADS * HEAD_DIM
KVA_W = A_KV_HEADS * HEAD_DIM
A_IN = QA_W + 2 * KVA_W
B_W = B_HEADS * HEAD_DIM
B_IN = 3 * B_W
C_W = C_HEADS * HEAD_DIM

ATT_TB = 1024
MM_TM = 512
MOE_TM = 256
ROW_TM = 1024
VMEM_LIMIT = 56 * 1024 * 1024


def _cparams(sem):
    return pltpu.CompilerParams(dimension_semantics=sem, vmem_limit_bytes=VMEM_LIMIT)


def _segment_flags(tok0, size, seg_starts, seg_ends):
    is_first = functools.reduce(jnp.logical_or, [tok0 == s for s in seg_starts])
    is_last = functools.reduce(jnp.logical_or, [tok0 + size == e for e in seg_ends])
    return is_first, is_last


def _t5_bucket_np(rel):
    half_b = REL_BUCKETS // 2
    max_exact = half_b // 2
    n = np.abs(rel)
    large = max_exact + (np.log(np.maximum(n, max_exact).astype(np.float32) / max_exact)
                         / math.log(REL_MAX_DIST / max_exact) * (half_b - max_exact)).astype(np.int32)
    large = np.minimum(large, half_b - 1)
    return np.where(rel > 0, half_b, 0) + np.where(n < max_exact, n, large)


def _banded_bias(table, half, dil, q_len=None):
    q_len = half if q_len is None else q_len
    rel = np.arange(q_len + 2 * half)[None, :] - half - np.arange(q_len)[:, None]
    bucket = jnp.asarray(_t5_bucket_np(rel * dil).astype(np.int32))
    hit = bucket[None] == jnp.arange(REL_BUCKETS, dtype=jnp.int32)[:, None, None]
    bias = jnp.sum(jnp.where(hit[:, None], table.astype(F32)[:, :, None, None], 0.0), axis=0)
    return jnp.where(jnp.asarray(np.abs(rel) <= half)[None], bias * LOG2E, NEG_INF)


def _bias_a(rel_bias):
    w = A_WINDOW
    ch = A_CHUNK
    b = _banded_bias(rel_bias[:, :A_HEADS], w, 1).reshape(A_HEADS, w // ch, ch, 3 * w)
    b = b.transpose(1, 0, 2, 3).reshape(w // ch, A_HEADS * ch, 3 * w)
    col = np.arange(3 * w)
    first = jnp.where(jnp.asarray(col < w), NEG_INF, b)
    last = jnp.where(jnp.asarray(col >= 2 * w), NEG_INF, b)
    return jnp.stack([b, first, last])


def _bias_b(rel_bias):
    out = []
    for (_, d), q in zip(B_BRANCHES, B_QBLK):
        b = _banded_bias(rel_bias[:, A_HEADS:], B_HALF, d, q)
        out.append(b.reshape(B_HEADS // 2, 2 * q, q + 2 * B_HALF))
    return out


def _bias_c(rpb):
    gw = GRID_W
    n_dr = 2 * NA_ROWS - 1
    side = gw - NA_COLS
    p = jnp.concatenate([jnp.repeat(rpb[..., :1], side, axis=-1), rpb.astype(F32),
                         jnp.repeat(rpb[..., -1:], side + 1, axis=-1)], axis=-1)
    z = jnp.broadcast_to(p[:, :, None, :], (C_HEADS, n_dr, gw, 2 * gw)).reshape(C_HEADS, n_dr, 2 * gw * gw)
    t = z[:, :, gw - 1:gw - 1 + gw * (2 * gw - 1)].reshape(C_HEADS, n_dr, gw, 2 * gw - 1)[..., :gw]
    cq = np.arange(gw)[:, None]
    w = np.arange(gw)[None, :]
    c0 = np.clip(cq - NA_COLS // 2, 0, gw - NA_COLS)
    t = jnp.where(jnp.asarray((w >= c0) & (w < c0 + NA_COLS)), t * LOG2E, NEG_INF)
    t = t.reshape(C_HEADS // 2, 2, n_dr, gw, gw).transpose(0, 1, 3, 2, 4)
    bias = jnp.stack([t[:, :, :, NA_ROWS - 1 - s:2 * NA_ROWS - 1 - s, :].reshape(C_HEADS // 2, 2, gw, NA_ROWS * gw)
                      for s in range(NA_ROWS)], axis=1)
    return bias.reshape(C_HEADS // 2, NA_ROWS, 2 * gw, NA_ROWS * gw)


def _segment_blocks(segs, tm):
    return tuple(a.shape[0] // tm for a in segs)


def _segment_specs(seg_blocks, tm, width):
    specs, start = [], 0
    for nb in seg_blocks:
        specs.append(pl.BlockSpec((tm, width), lambda i, *_, s=start, nb=nb: (jnp.clip(i - s, 0, nb - 1), 0)))
        start += nb
    return specs


def _segment_rows(i, refs, seg_blocks):
    x, start = refs[0][...], seg_blocks[0]
    for ref, nb in zip(refs[1:], seg_blocks[1:]):
        x = jnp.where(i >= start, ref[...], x)
        start += nb
    return x


def _inproj_kernel(*refs, splits, seg_blocks):
    n_seg = len(seg_blocks)
    w_ref = refs[n_seg]
    o_refs = refs[n_seg + 1:]
    x = _segment_rows(pl.program_id(0), refs[:n_seg], seg_blocks).astype(BF16)
    for o_ref, (lo, hi) in zip(o_refs, splits):
        o_ref[...] = jnp.dot(x, w_ref[:, lo:hi], preferred_element_type=F32).astype(o_ref.dtype)


def _inproj(xs, w, splits, dtypes):
    seg_blocks = _segment_blocks(xs, MM_TM)
    n = sum(seg_blocks) * MM_TM
    return pl.pallas_call(
        functools.partial(_inproj_kernel, splits=splits, seg_blocks=seg_blocks),
        grid=(n // MM_TM,),
        in_specs=_segment_specs(seg_blocks, MM_TM, D_MODEL) + [pl.BlockSpec(w.shape, lambda i: (0, 0))],
        out_specs=[pl.BlockSpec((MM_TM, hi - lo), lambda i: (i, 0)) for lo, hi in splits],
        out_shape=[jax.ShapeDtypeStruct((n, hi - lo), dt) for (lo, hi), dt in zip(splits, dtypes)],
        compiler_params=_cparams(("parallel",)),
        name="inproj",
    )(*xs, w)


def _layer_norm(z, g, b):
    mu = jnp.mean(z, axis=-1, keepdims=True)
    zc = z - mu
    var = jnp.mean(zc * zc, axis=-1, keepdims=True)
    return zc * lax.rsqrt(var + LN_EPS) * g + b


def _outproj_ln_kernel(*refs, n_parts, seg_blocks):
    n_seg = len(seg_blocks)
    o_refs = refs[:n_parts]
    w_refs = refs[n_parts:2 * n_parts]
    x_refs = refs[2 * n_parts:2 * n_parts + n_seg]
    g_ref, b_ref, whl_ref, br_ref, out_ref, info_ref, cnt_ref, run_scr, tri_scr = refs[2 * n_parts + n_seg:]
    i = pl.program_id(0)
    h = DEEPNORM_ALPHA * _segment_rows(i, x_refs, seg_blocks)
    for o_ref, w_ref in zip(o_refs, w_refs):
        h = h + jnp.dot(o_ref[...], w_ref[...], preferred_element_type=F32)
    out = _layer_norm(h, g_ref[...], b_ref[...])
    out_ref[...] = out
    _route(i, out, whl_ref, br_ref, info_ref, cnt_ref, run_scr, tri_scr)


def _outproj_ln(parts, weights, xs, g, b, w_router, b_router):
    tm = MM_TM
    seg_blocks = _segment_blocks(xs, tm)
    n = sum(seg_blocks) * tm
    n_parts = len(parts)
    wh = w_router.astype(BF16)
    whl = jnp.concatenate([wh, (w_router - wh.astype(F32)).astype(BF16)], axis=1)
    const = lambda shape: pl.BlockSpec(shape, lambda i: (0, 0))
    return pl.pallas_call(
        functools.partial(_outproj_ln_kernel, n_parts=n_parts, seg_blocks=seg_blocks),
        grid=(n // tm,),
        in_specs=([pl.BlockSpec((tm, p.shape[1]), lambda i: (i, 0)) for p in parts]
                  + [const(w.shape) for w in weights]
                  + _segment_specs(seg_blocks, tm, D_MODEL)
                  + [const((1, D_MODEL)), const((1, D_MODEL)),
                     const((D_MODEL, 2 * LANES)), const((1, LANES))]),
        out_specs=[pl.BlockSpec((tm, D_MODEL), lambda i: (i, 0)),
                   pl.BlockSpec((None, SUBLANES, tm), lambda i: (i, 0, 0)),
                   const((CLASS_ROWS, LANES))],
        out_shape=[jax.ShapeDtypeStruct((n, D_MODEL), F32),
                   jax.ShapeDtypeStruct((n // tm, SUBLANES, tm), F32),
                   jax.ShapeDtypeStruct((CLASS_ROWS, LANES), F32)],
        scratch_shapes=[pltpu.VMEM((CLASS_ROWS, LANES), F32), pltpu.VMEM((tm, tm), BF16)],
        compiler_params=_cparams(("arbitrary",)),
        name="outproj_ln",
    )(*parts, *weights, *xs, g.reshape(1, D_MODEL), b.reshape(1, D_MODEL), whl, b_router)


ATT_DEPTH = 3
B_DEPTH = 10


def _staged(n, weights, values, depth=ATT_DEPTH):
    for i in range(min(depth, n)):
        weights(i)
    for i in range(n):
        if i + depth < n:
            weights(i + depth)
        values(i)


def _attn_a_kernel(q_ref, kvm_ref, kvp_ref, kvn_ref, bias_ref, sink_ref, o_ref, kv_scr, p_scr,
                   *, seg_starts, seg_ends):
    w = A_WINDOW
    tb = A_TB
    n_sub = tb // w
    tok0 = pl.program_id(0) * tb
    is_first, is_last = _segment_flags(tok0, tb, seg_starts, seg_ends)
    kv_scr[0:w, :] = kvp_ref[...]
    kv_scr[w:w + tb, :] = kvm_ref[...]
    kv_scr[w + tb:, :] = kvn_ref[...]
    low = lax.broadcasted_iota(jnp.int32, (1, LANES), 1) < HEAD_DIM
    ch = A_CHUNK
    per = w // ch

    sinks = [jnp.concatenate([jnp.full((ch, 1), sink_ref[c + 4 * g], F32) for c in range(4)], axis=0)
             for g in range(A_KV_HEADS)]

    sink = jnp.concatenate(sinks, axis=0)

    def scores(t):
        j = t // per
        q = q_ref[t * ch:(t + 1) * ch, :]
        qg = jnp.concatenate([jnp.where(low, q[:, c * LANES:(c + 1) * LANES], 0) for c in range(4)]
                             + [jnp.where(low, 0, q[:, c * LANES:(c + 1) * LANES]) for c in range(4)], axis=0)
        k2 = kv_scr[j * w:(j + 3) * w, :LANES]
        return lax.dot_general(qg, k2, (((1,), (1,)), ((), ())), preferred_element_type=F32)

    rows = A_HEADS * ch
    rdens = {}

    def weights(t):
        j = t // per
        if j == 0:
            variant = jnp.where(is_first, 1, 0)
        elif j == n_sub - 1:
            variant = jnp.where(is_last, 2, 0)
        else:
            variant = 0
        s = scores(t) + bias_ref[variant, t % per]
        m = jnp.maximum(jnp.max(s, axis=-1, keepdims=True), sink)
        e = jnp.exp2(s - m)
        rdens[t] = 1.0 / (jnp.sum(e, axis=-1, keepdims=True) + jnp.exp2(sink - m))
        p_scr[t * rows:(t + 1) * rows, :] = e.astype(BF16)

    def values(t):
        j = t // per
        v2 = kv_scr[j * w:(j + 3) * w, LANES:]
        pv = jnp.dot(p_scr[t * rows:(t + 1) * rows, :], v2, preferred_element_type=F32) * rdens[t]
        for c in range(4):
            oc = jnp.where(low, pv[c * ch:(c + 1) * ch], pv[(4 + c) * ch:(5 + c) * ch])
            o_ref[t * ch:(t + 1) * ch, c * LANES:(c + 1) * LANES] = oc.astype(o_ref.dtype)

    _staged(tb // ch, weights, values)


A_CHUNK = 128
A_TB = 2 * ATT_TB


def _attn_a(a_qkv, bias, sink, seg_starts, seg_ends):
    n = a_qkv.shape[0]
    w = A_WINDOW
    tb = A_TB
    sub = tb // w
    nhb = n // w
    kv_col = QA_W // (2 * LANES)
    return pl.pallas_call(
        functools.partial(_attn_a_kernel, seg_starts=seg_starts, seg_ends=seg_ends),
        grid=(n // tb,),
        in_specs=[pl.BlockSpec((tb, QA_W), lambda i: (i, 0)),
                  pl.BlockSpec((tb, 2 * LANES), lambda i: (i, kv_col)),
                  pl.BlockSpec((w, 2 * LANES), lambda i: (jnp.maximum(i * sub - 1, 0), kv_col)),
                  pl.BlockSpec((w, 2 * LANES), lambda i: (jnp.minimum((i + 1) * sub, nhb - 1), kv_col)),
                  pl.BlockSpec(bias.shape, lambda i: (0,) * bias.ndim),
                  pl.BlockSpec(memory_space=pltpu.SMEM)],
        out_specs=pl.BlockSpec((tb, QA_W), lambda i: (i, 0)),
        out_shape=jax.ShapeDtypeStruct((n, QA_W), BF16),
        scratch_shapes=[pltpu.VMEM((tb + 2 * w, 2 * LANES), BF16),
                        pltpu.VMEM((A_HEADS * tb, 3 * w), BF16)],
        compiler_params=_cparams(("parallel",)),
        name="attn_a",
    )(a_qkv, a_qkv, a_qkv, a_qkv, bias, sink)


def _attn_b_kernel(q_ref, kp_ref, km_ref, kn_ref, vp_ref, vm_ref, vn_ref, bias0_ref, bias1_ref, bias2_ref, o_ref,
                   k_scr, v_scr, o_scr, m_scr, l_scr, p0_scr, p1_scr, p2_scr, *, seg_starts, seg_ends):
    tb = B_TB
    h = B_HALF
    tok0 = pl.program_id(1) * tb
    is_first, is_last = _segment_flags(tok0, tb, seg_starts, seg_ends)
    k_scr[0:tb, :] = kp_ref[...]
    k_scr[tb:2 * tb, :] = km_ref[...]
    k_scr[2 * tb:, :] = kn_ref[...]
    v_scr[0:tb, :] = vp_ref[...]
    v_scr[tb:2 * tb, :] = vm_ref[...]
    v_scr[2 * tb:, :] = vn_ref[...]
    lane = lax.broadcasted_iota(jnp.int32, (1, LANES), 1)
    low = lane < HEAD_DIM
    bias_refs = (bias0_ref, bias1_ref, bias2_ref)
    p_scrs = (p0_scr, p1_scr, p2_scr)

    pens = []
    for q in B_QBLK:
        col = lax.broadcasted_iota(jnp.int32, (1, q + 2 * h), 1)
        pens.append((jnp.where(jnp.logical_and(col < h, is_first), NEG_INF, 0.0),
                     jnp.where(jnp.logical_and(col >= q + h, is_last), NEG_INF, 0.0)))

    def slices(br, d, r, b):
        q = B_QBLK[br]
        row0 = r + q * d * b
        if d == 1:
            return pl.ds(row0, q), pl.ds(tb + row0 - h, q + 2 * h)
        return pl.ds(row0, q, stride=d), pl.ds(tb + row0 - h * d, q + 2 * h, stride=d)

    def scores(br, d, r, b):
        qs, ks = slices(br, d, r, b)
        q = q_ref[qs, :].astype(BF16)
        k = k_scr[ks, :].astype(BF16)
        qq = jnp.concatenate([jnp.where(low, q, 0), jnp.where(low, 0, q)], axis=0)
        return lax.dot_general(qq, k, (((1,), (1,)), ((), ())), preferred_element_type=F32)

    tiles = [(br, d, r, b, r * (tb // (B_QBLK[br] * d)) + b) for br, (_, d) in reversed(list(enumerate(B_BRANCHES)))
             for r in range(d) for b in range(tb // (B_QBLK[br] * d))]

    def weights(i):
        br, d, r, b, t = tiles[i]
        q = B_QBLK[br]
        qs, _ = slices(br, d, r, b)
        s = scores(br, d, r, b) + bias_refs[br][...]
        if b == 0:
            s = s + pens[br][0]
        if b == tb // (q * d) - 1:
            s = s + pens[br][1]
        m = jnp.max(s, axis=-1, keepdims=True)
        e = jnp.exp2(s - m)
        l = jnp.sum(e, axis=-1, keepdims=True)
        p_scrs[br][t * 2 * q:(t + 1) * 2 * q, :] = e.astype(BF16)
        m_scr[br, qs, :] = jnp.where(low, m[:q], m[q:])
        l_scr[br, qs, :] = jnp.where(low, l[:q], l[q:])

    def values(i):
        br, d, r, b, t = tiles[i]
        q = B_QBLK[br]
        qs, ks = slices(br, d, r, b)
        v = v_scr[ks, :].astype(BF16)
        pv = jnp.dot(p_scrs[br][t * 2 * q:(t + 1) * 2 * q, :], v, preferred_element_type=F32)
        o_scr[br, qs, :] = jnp.where(low, pv[:q], pv[q:])

    _staged(len(tiles), weights, values, B_DEPTH)

    m_all = jnp.maximum(jnp.maximum(m_scr[0], m_scr[1]), m_scr[2])
    num = jnp.zeros((tb, LANES), F32)
    den = jnp.zeros((tb, LANES), F32)
    for br in range(len(B_BRANCHES)):
        a = jnp.exp2(m_scr[br] - m_all)
        num = num + a * o_scr[br]
        den = den + a * l_scr[br]
    o_ref[...] = (num / den).astype(o_ref.dtype)


B_TB = ATT_TB
B_QBLK = tuple(B_HALF for _ in B_BRANCHES)


def _attn_b(b_qkv, biases, seg_starts, seg_ends):
    n = b_qkv.shape[0]
    tb = B_TB
    nblk = n // tb
    npair = B_HEADS // 2
    prev = lambda i: jnp.maximum(i - 1, 0)
    nxt = lambda i: jnp.minimum(i + 1, nblk - 1)
    blk = lambda rowf, off: pl.BlockSpec((tb, LANES), lambda c, i: (rowf(i), off + c))
    same = lambda i: i
    stat = pltpu.VMEM((len(B_BRANCHES), tb, LANES), F32)
    return pl.pallas_call(
        functools.partial(_attn_b_kernel, seg_starts=seg_starts, seg_ends=seg_ends),
        grid=(npair, nblk),
        in_specs=[blk(same, 0),
                  blk(prev, npair), blk(same, npair), blk(nxt, npair),
                  blk(prev, 2 * npair), blk(same, 2 * npair), blk(nxt, 2 * npair)]
                 + [pl.BlockSpec((None,) + b.shape[1:], lambda c, i: (c, 0, 0)) for b in biases],
        out_specs=pl.BlockSpec((tb, LANES), lambda c, i: (i, c)),
        out_shape=jax.ShapeDtypeStruct((n, B_W), BF16),
        scratch_shapes=[pltpu.VMEM((3 * tb, LANES), F32), pltpu.VMEM((3 * tb, LANES), F32),
                        stat, stat, stat]
                       + [pltpu.VMEM((2 * tb, q + 2 * B_HALF), BF16) for q in B_QBLK],
        compiler_params=_cparams(("parallel", "parallel")),
        name="attn_b",
    )(b_qkv, b_qkv, b_qkv, b_qkv, b_qkv, b_qkv, b_qkv, *biases)


C_HALO = (NA_ROWS // 2) * GRID_W

def _attn_c_kernel(q_ref, kp_ref, km_ref, kn_ref, vp_ref, vm_ref, vn_ref, bias_ref, o_ref,
                   k_scr, v_scr, p_scr, *, seg_starts, seg_ends):
    tb = ATT_TB
    gw = GRID_W
    nkeys = NA_ROWS * gw
    tok0 = pl.program_id(1) * tb
    k_scr[0:C_HALO, :] = kp_ref[...]
    k_scr[C_HALO:C_HALO + tb, :] = km_ref[...]
    k_scr[C_HALO + tb:, :] = kn_ref[...]
    v_scr[0:C_HALO, :] = vp_ref[...]
    v_scr[C_HALO:C_HALO + tb, :] = vm_ref[...]
    v_scr[C_HALO + tb:, :] = vn_ref[...]
    seg_row0 = jnp.int32(0)
    seg_rows = jnp.int32(0)
    for s, e in zip(seg_starts, seg_ends):
        inside = jnp.logical_and(tok0 >= s, tok0 < e)
        seg_row0 = jnp.where(inside, s // gw, seg_row0)
        seg_rows = jnp.where(inside, (e - s) // gw, seg_rows)
    lane = lax.broadcasted_iota(jnp.int32, (1, LANES), 1)
    low = lane < HEAD_DIM

    def window(rr):
        rs = tok0 // gw + rr - seg_row0
        start = jnp.clip(rs - NA_ROWS // 2, 0, seg_rows - NA_ROWS)
        shift = rs - start
        return shift, pl.ds(pl.multiple_of((rr + NA_ROWS // 2 - shift) * gw, gw), nkeys)

    def scores(rr):
        _, ks = window(rr)
        q = q_ref[rr * gw:(rr + 1) * gw, :]
        qq = jnp.concatenate([jnp.where(low, q, 0), jnp.where(low, 0, q)], axis=0)
        return lax.dot_general(qq, k_scr[ks, :], (((1,), (1,)), ((), ())), preferred_element_type=F32)

    n_rows = tb // gw
    rdens = {}

    def weights(rr):
        shift, _ = window(rr)
        s = scores(rr) + bias_ref[shift]
        m = jnp.max(s, axis=-1, keepdims=True)
        e = jnp.exp2(s - m)
        rdens[rr] = 1.0 / jnp.sum(e, axis=-1, keepdims=True)
        p_scr[rr * 2 * gw:(rr + 1) * 2 * gw, :] = e.astype(BF16)

    def values(rr):
        _, ks = window(rr)
        pv = jnp.dot(p_scr[rr * 2 * gw:(rr + 1) * 2 * gw, :], v_scr[ks, :], preferred_element_type=F32) * rdens[rr]
        o_ref[rr * gw:(rr + 1) * gw, :] = jnp.where(low, pv[:gw], pv[gw:]).astype(o_ref.dtype)

    _staged(n_rows, weights, values)


def _attn_c(c_qkv, bias, seg_starts, seg_ends):
    n = c_qkv.shape[0]
    tb = ATT_TB
    npair = C_HEADS // 2
    sub = tb // C_HALO
    nhb = n // C_HALO
    main = lambda off: pl.BlockSpec((tb, LANES), lambda c, i: (i, off + c))
    prev = lambda off: pl.BlockSpec((C_HALO, LANES), lambda c, i: (jnp.maximum(i * sub - 1, 0), off + c))
    nxt = lambda off: pl.BlockSpec((C_HALO, LANES),
                                   lambda c, i: (jnp.minimum((i + 1) * sub, nhb - 1), off + c))
    return pl.pallas_call(
        functools.partial(_attn_c_kernel, seg_starts=seg_starts, seg_ends=seg_ends),
        grid=(npair, n // tb),
        in_specs=[main(0),
                  prev(npair), main(npair), nxt(npair),
                  prev(2 * npair), main(2 * npair), nxt(2 * npair),
                  pl.BlockSpec((None,) + bias.shape[1:], lambda c, i: (c, 0, 0, 0))],
        out_specs=pl.BlockSpec((tb, LANES), lambda c, i: (i, c)),
        out_shape=jax.ShapeDtypeStruct((n, C_W), BF16),
        scratch_shapes=[pltpu.VMEM((tb + 2 * C_HALO, LANES), BF16),
                        pltpu.VMEM((tb + 2 * C_HALO, LANES), BF16),
                        pltpu.VMEM((2 * tb, NA_ROWS * GRID_W), BF16)],
        compiler_params=_cparams(("parallel", "parallel")),
        name="attn_c",
    )(c_qkv, c_qkv, c_qkv, c_qkv, c_qkv, c_qkv, c_qkv, bias)


def _route(i, x, whl_ref, b_ref, info_ref, cnt_ref, run_scr, tri_scr):
    tm = x.shape[0]

    @pl.when(i == 0)
    def _():
        run_scr[...] = jnp.zeros_like(run_scr)

    xh = x.astype(BF16)
    xl = (x - xh.astype(F32)).astype(BF16)
    hh_hl = jnp.dot(xh, whl_ref[...], preferred_element_type=F32)
    logits = (hh_hl[:, :LANES] + jnp.dot(xl, whl_ref[:, :LANES], preferred_element_type=F32)
              + hh_hl[:, LANES:]) + b_ref[...]
    lt = logits.T
    first = lambda hit, n: jnp.min(jnp.where(hit, lax.broadcasted_iota(jnp.int32, (n, tm), 0).astype(F32),
                                             float(n)), axis=0, keepdims=True)
    lg = lt[0:N_GROUPS]
    g_sel = first(lg == jnp.max(lg, axis=0, keepdims=True), N_GROUPS)
    le = jnp.zeros((EXPERTS_PER_GROUP, tm), F32)
    for g in range(N_GROUPS):
        lo = N_GROUPS + g * EXPERTS_PER_GROUP
        le = jnp.where(g_sel == g, lt[lo:lo + EXPERTS_PER_GROUP], le)
    row = lax.broadcasted_iota(jnp.int32, (EXPERTS_PER_GROUP, tm), 0).astype(F32)
    i1 = first(le == jnp.max(le, axis=0, keepdims=True), EXPERTS_PER_GROUP)
    rest = jnp.where(row == i1, NEG_INF, le)
    i2 = first(jnp.logical_and(rest == jnp.max(rest, axis=0, keepdims=True), row != i1), EXPERTS_PER_GROUP)
    a = jnp.minimum(i1, i2)
    b = jnp.maximum(i1, i2)
    cls = g_sel * N_PAIRS + a * 3.0 - jnp.where(a == 2.0, 1.0, 0.0) + (b - a - 1.0)

    @pl.when(i == 0)
    def _():
        tri_scr[...] = (lax.broadcasted_iota(jnp.int32, (tm, tm), 0)
                        < lax.broadcasted_iota(jnp.int32, (tm, tm), 1)).astype(BF16)

    onehot = lax.broadcasted_iota(jnp.int32, (CLASS_ROWS, tm), 0).astype(F32) == cls
    before = jnp.dot(onehot.astype(BF16), tri_scr[...], preferred_element_type=F32) + run_scr[:, 0:1]
    rank = jnp.sum(jnp.where(onehot, before, 0.0), axis=0, keepdims=True)
    run_scr[...] = run_scr[...] + jnp.sum(onehot.astype(F32), axis=1, keepdims=True)
    srow = lax.broadcasted_iota(jnp.int32, (SUBLANES, tm), 0)
    info_ref[...] = jnp.where(srow == 0, cls, jnp.where(srow == 1, rank, 0.0))
    cnt_ref[...] = run_scr[...]


CLASS_ROWS = 32
SUBLANES = 8


def _tile_copy(src, src_row, dst, dst_row, sem):
    return pltpu.make_async_copy(src.at[pl.ds(pl.multiple_of(src_row, SUBLANES), SUBLANES), :],
                                 dst.at[pl.ds(pl.multiple_of(dst_row, SUBLANES), SUBLANES), :], sem)


ROW_UNROLL = 8


def _start_rows(copy, n):
    def body(g, carry):
        for u in range(ROW_UNROLL):
            copy(g * ROW_UNROLL + u).start(priority=u % 2)
        return carry

    lax.fori_loop(0, n // ROW_UNROLL, body, 0)


def _dispatch_kernel(dest_ref, pad_ref, x_ref, xs_ref, rec_scr, zero_scr, sems, zsem):
    i = pl.program_id(0)
    last = pl.num_programs(0) - 1
    tm = x_ref.shape[0]
    rows = tm * SUBLANES
    slot = i % 2
    tile_rows = MOE_TM * SUBLANES

    def zero_copy(c):
        start = pl.multiple_of(pad_ref[c] * SUBLANES, tile_rows)
        return pltpu.make_async_copy(zero_scr, xs_ref.at[pl.ds(start, tile_rows), :], zsem)

    @pl.when(i == 0)
    def _():
        zero_scr[...] = jnp.zeros_like(zero_scr)
        for c in range(pad_ref.shape[0]):
            @pl.when(pad_ref[c] >= 0)
            def _():
                zero_copy(c).start()
        for c in range(pad_ref.shape[0]):
            @pl.when(pad_ref[c] >= 0)
            def _():
                zero_copy(c).wait()

    base = i * tm
    for s in range(2):
        @pl.when(slot == s)
        def _(s=s):
            for j in range(D_MODEL // LANES):
                rec_scr[s, pl.ds(j, tm, stride=SUBLANES), :] = x_ref[:, j * LANES:(j + 1) * LANES]
            _start_rows(lambda r: _tile_copy(rec_scr.at[s], r * SUBLANES, xs_ref, dest_ref[base + r], sems.at[s]),
                        tm)

    def wait_step(s):
        pltpu.make_async_copy(rec_scr.at[s], xs_ref.at[pl.ds(0, rows), :], sems.at[s]).wait()

    @pl.when(i > 0)
    def _():
        wait_step(1 - slot)

    @pl.when(i == last)
    def _():
        wait_step(slot)


def _dispatch(dest, pad_start, x, n_sorted):
    n = x.shape[0]
    tm = ROW_TM
    return pl.pallas_call(
        _dispatch_kernel,
        grid_spec=pltpu.PrefetchScalarGridSpec(
            num_scalar_prefetch=2,
            grid=(n // tm,),
            in_specs=[pl.BlockSpec((tm, D_MODEL), lambda i, d, p: (i, 0))],
            out_specs=pl.BlockSpec(memory_space=pl.ANY),
            scratch_shapes=[pltpu.VMEM((2, tm * SUBLANES, LANES), F32),
                            pltpu.VMEM((MOE_TM * SUBLANES, LANES), F32),
                            pltpu.SemaphoreType.DMA((2,)), pltpu.SemaphoreType.DMA]),
        out_shape=jax.ShapeDtypeStruct((n_sorted * SUBLANES, LANES), F32),
        compiler_params=_cparams(("arbitrary",)),
        name="dispatch",
    )(dest, pad_start, x)


def _expert_kernel(ea_ref, eb_ref, nt_ref, xs_ref, wr_ref, br_ref,
                   wga_ref, wua_ref, wda_ref, wgb_ref, wub_ref, wdb_ref, g_ref, b_ref, ys_ref, hid_scr):
    p = pl.program_id(0)
    tm = MOE_TM

    @pl.when(p >= nt_ref[0])
    def _():
        ys_ref[...] = jnp.zeros_like(ys_ref)

    @pl.when(p < nt_ref[0])
    def _():
        x = jnp.concatenate([xs_ref[pl.ds(j, tm, stride=SUBLANES), :] for j in range(D_MODEL // LANES)],
                            axis=1)
        xb = x.astype(BF16)

        logits = jnp.dot(xb, wr_ref[...], preferred_element_type=F32) + br_ref[...]
        lane = lax.broadcasted_iota(jnp.int32, (tm, LANES), 1)
        lane_a = N_GROUPS + ea_ref[p]
        lane_b = N_GROUPS + eb_ref[p]
        grp = ea_ref[p] // EXPERTS_PER_GROUP
        is_g = lane < N_GROUPS
        mg = jnp.max(jnp.where(is_g, logits, NEG_INF), axis=-1, keepdims=True)
        eg = jnp.where(is_g, jnp.exp(logits - mg), 0.0)
        g_gate = (jnp.sum(jnp.where(lane == grp, eg, 0.0), axis=-1, keepdims=True)
                  / jnp.sum(eg, axis=-1, keepdims=True))
        l_a = jnp.sum(jnp.where(lane == lane_a, logits, 0.0), axis=-1, keepdims=True)
        l_b = jnp.sum(jnp.where(lane == lane_b, logits, 0.0), axis=-1, keepdims=True)
        mx = jnp.maximum(l_a, l_b)
        p_a = jnp.exp(l_a - mx)
        p_b = jnp.exp(l_b - mx)
        scale = g_gate / (p_a + p_b)
        w_a = p_a * scale
        w_b = p_b * scale

        for e, (wg_ref, wu_ref) in enumerate(((wga_ref, wua_ref), (wgb_ref, wub_ref))):
            gate = jnp.dot(xb, wg_ref[...], preferred_element_type=F32)
            up = jnp.dot(xb, wu_ref[...], preferred_element_type=F32)
            hid_scr[e] = (gate * (1.0 / (1.0 + jnp.exp(-gate))) * up).astype(BF16)
        y = (w_a * jnp.dot(hid_scr[0], wda_ref[...], preferred_element_type=F32)
             + w_b * jnp.dot(hid_scr[1], wdb_ref[...], preferred_element_type=F32))
        out = _layer_norm(DEEPNORM_ALPHA * x + y, g_ref[...], b_ref[...])
        for j in range(D_MODEL // LANES):
            ys_ref[pl.ds(j, tm, stride=SUBLANES), :] = out[:, j * LANES:(j + 1) * LANES]


def _experts(layer, tile_ea, tile_eb, n_tiles, xs, w_router, b_router, w_gate, w_up, w_down, g, b):
    tm = MOE_TM
    rows = tm * SUBLANES
    n_grid = xs.shape[0] // rows
    last = lambda p, nt: jnp.maximum(jnp.minimum(p, nt[0] - 1), 0)
    wspec_a = lambda shape: pl.BlockSpec((None, None) + shape, lambda p, ea, eb, nt: (layer, ea[p], 0, 0))
    wspec_b = lambda shape: pl.BlockSpec((None, None) + shape, lambda p, ea, eb, nt: (layer, eb[p], 0, 0))
    const = lambda shape: pl.BlockSpec(shape, lambda p, ea, eb, nt: (0, 0))
    up_shape = (D_MODEL, D_EXPERT)
    dn_shape = (D_EXPERT, D_MODEL)
    return pl.pallas_call(
        _expert_kernel,
        grid_spec=pltpu.PrefetchScalarGridSpec(
            num_scalar_prefetch=3,
            grid=(n_grid,),
            in_specs=[pl.BlockSpec((rows, LANES), lambda p, ea, eb, nt: (last(p, nt), 0)),
                      const((D_MODEL, LANES)), const((1, LANES)),
                      wspec_a(up_shape), wspec_a(up_shape), wspec_a(dn_shape),
                      wspec_b(up_shape), wspec_b(up_shape), wspec_b(dn_shape),
                      const((1, D_MODEL)), const((1, D_MODEL))],
            out_specs=pl.BlockSpec((rows, LANES), lambda p, ea, eb, nt: (p, 0)),
            scratch_shapes=[pltpu.VMEM((2, tm, D_EXPERT), BF16)]),
        out_shape=jax.ShapeDtypeStruct(xs.shape, F32),
        compiler_params=_cparams(("arbitrary",)),
        name="experts",
    )(tile_ea, tile_eb, n_tiles, xs, w_router.astype(BF16), b_router,
      w_gate, w_up, w_down, w_gate, w_up, w_down, g.reshape(1, D_MODEL), b.reshape(1, D_MODEL))


def _gather_kernel(dest_ref, ys_ref, *refs, seg_blocks):
    out_refs = refs[:len(seg_blocks)]
    rec_scr, sems = refs[len(seg_blocks):]
    i = pl.program_id(0)
    n_steps = pl.num_programs(0)
    tm = out_refs[0].shape[0]
    rows = tm * SUBLANES
    slot = i % 2

    def fetch(s, buf):
        base = s * tm
        _start_rows(lambda r: _tile_copy(ys_ref, dest_ref[base + r], rec_scr.at[buf], r * SUBLANES, sems.at[buf]),
                    tm)

    @pl.when(i == 0)
    def _():
        fetch(0, 0)

    for buf in range(2):
        @pl.when(jnp.logical_and(i + 1 < n_steps, (i + 1) % 2 == buf))
        def _(buf=buf):
            fetch(i + 1, buf)

    pltpu.make_async_copy(ys_ref.at[pl.ds(0, rows), :], rec_scr.at[slot], sems.at[slot]).wait()

    start = 0
    for out_ref, nb in zip(out_refs, seg_blocks):
        @pl.when(jnp.logical_and(i >= start, i < start + nb))
        def _(out_ref=out_ref):
            for j in range(D_MODEL // LANES):
                out_ref[:, j * LANES:(j + 1) * LANES] = rec_scr[slot, pl.ds(j, tm, stride=SUBLANES), :]
        start += nb


def _gather_rows(dest, ys, seg_rows):
    tm = ROW_TM
    seg_blocks = tuple(r // tm for r in seg_rows)
    return pl.pallas_call(
        functools.partial(_gather_kernel, seg_blocks=seg_blocks),
        grid_spec=pltpu.PrefetchScalarGridSpec(
            num_scalar_prefetch=1,
            grid=(sum(seg_blocks),),
            in_specs=[pl.BlockSpec(memory_space=pl.ANY)],
            out_specs=_segment_specs(seg_blocks, tm, D_MODEL),
            scratch_shapes=[pltpu.VMEM((2, tm * SUBLANES, LANES), F32), pltpu.SemaphoreType.DMA((2,))]),
        out_shape=[jax.ShapeDtypeStruct((r, D_MODEL), F32) for r in seg_rows],
        compiler_params=_cparams(("arbitrary",)),
        name="gather_rows",
    )(dest, ys)


_PAIR_A = np.array([0, 0, 0, 1, 1, 2], np.int32)
_PAIR_B = np.array([1, 2, 3, 2, 3, 3], np.int32)


def _moe_layer(layer, x, info, counts, w_router, b_router, w_gate, w_up, w_down, g, b, out_rows):
    n = x.shape[0]
    tm = MOE_TM
    n_sorted = n + N_CLASSES * tm
    cls = info[:, 0, :].reshape(n).astype(jnp.int32)
    rank = info[:, 1, :].reshape(n).astype(jnp.int32)
    counts = counts[:N_CLASSES, 0].astype(jnp.int32)
    padded = (counts + tm - 1) // tm * tm
    classes = jnp.arange(N_CLASSES, dtype=jnp.int32)
    ends = jnp.sum(jnp.where(classes[None, :] <= classes[:, None], padded[None, :], 0), axis=1)
    offs = ends - padded
    total = ends[N_CLASSES - 1]
    dest = rank + jnp.sum(jnp.where(cls[:, None] == classes[None, :], offs[None, :], 0), axis=1)
    unused = total + classes * tm
    pad_start = jnp.concatenate([jnp.where(padded > 0, ends - tm, -1),
                                 jnp.where(unused < n_sorted, unused, -1)]).astype(jnp.int32)
    tile_start = jnp.arange(n_sorted // tm, dtype=jnp.int32) * tm
    tile_start = jnp.minimum(tile_start, total - tm)
    tile_cls = jnp.sum((ends[None, :] <= tile_start[:, None]).astype(jnp.int32), axis=1)
    pair = tile_cls % N_PAIRS
    pair_a = jnp.sum(jnp.where(pair[:, None] == np.arange(N_PAIRS)[None, :], _PAIR_A[None, :], 0), axis=1)
    pair_b = jnp.sum(jnp.where(pair[:, None] == np.arange(N_PAIRS)[None, :], _PAIR_B[None, :], 0), axis=1)
    grp = tile_cls // N_PAIRS
    tile_ea = (grp * EXPERTS_PER_GROUP + pair_a).astype(jnp.int32)
    tile_eb = (grp * EXPERTS_PER_GROUP + pair_b).astype(jnp.int32)
    n_tiles = (total // tm).astype(jnp.int32).reshape(1)
    dest_row = (dest * SUBLANES).astype(jnp.int32)
    xs = _dispatch(dest_row, pad_start, x, n_sorted)
    ys = _experts(layer, tile_ea, tile_eb, n_tiles, xs, w_router, b_router, w_gate, w_up, w_down, g, b)
    return _gather_rows(dest_row, ys, out_rows)


_A_ORDER = np.array([0, 4, 1, 5, 2, 6, 3, 7])


def _prep_ab(w_in, w_out):
    qa = w_in[:, :QA_W].reshape(D_MODEL, A_HEADS, HEAD_DIM)[:, _A_ORDER].reshape(D_MODEL, QA_W) * ATTN_SCALE
    kva = w_in[:, QA_W:A_IN]
    qb = w_in[:, A_IN:A_IN + B_W] * ATTN_SCALE
    kvb = w_in[:, A_IN + B_W:]
    w = jnp.concatenate([qa, kva, qb, kvb], axis=1).astype(BF16)
    wo_a = w_out[:QA_W].reshape(A_HEADS, HEAD_DIM, D_MODEL)[_A_ORDER].reshape(QA_W, D_MODEL).astype(BF16)
    wo_b = w_out[QA_W:].astype(BF16)
    return w, wo_a, wo_b


def _prep_c(w_in, w_out):
    w = jnp.concatenate([w_in[:, :C_W] * ATTN_SCALE, w_in[:, C_W:]], axis=1).astype(BF16)
    return w, w_out.astype(BF16)


def _trunk(xs, seg_starts, seg_ends, rel_bias, w_in_ab, a_sink, w_out_ab, w_in_c, c_rpb, w_out_c,
           ln1_g, ln1_b, ln2_g, ln2_b, router_g_w, router_g_b, router_e_w, router_e_b,
           w_gate, w_up, w_down):
    io_rows = tuple(a.shape[0] for a in xs)
    n = sum(io_rows)
    bias_a = _bias_a(rel_bias)
    bias_b = _bias_b(rel_bias)
    w_gate = w_gate.astype(BF16)
    w_up = w_up.astype(BF16)
    w_down = w_down.astype(BF16)
    for l in range(DEPTH):
        i = l // 2
        pad = LANES - N_GROUPS - N_EXPERTS
        w_router = jnp.pad(jnp.concatenate([router_g_w[l], router_e_w[l]], axis=1), ((0, 0), (0, pad)))
        b_router = jnp.pad(jnp.concatenate([router_g_b[l], router_e_b[l]]), (0, pad)).reshape(1, LANES)
        if l % 2 == 0:
            w, wo_a, wo_b = _prep_ab(w_in_ab[i], w_out_ab[i])
            a_qkv, b_qkv = _inproj(xs, w, ((0, A_IN), (A_IN, A_IN + B_IN)), (BF16, F32))
            o_a = _attn_a(a_qkv, bias_a, a_sink[i].astype(F32) * LOG2E, seg_starts, seg_ends)
            o_b = _attn_b(b_qkv, bias_b, seg_starts, seg_ends)
            parts, weights = [o_a, o_b], [wo_a, wo_b]
        else:
            w, wo = _prep_c(w_in_c[i], w_out_c[i])
            (c_qkv,) = _inproj(xs, w, ((0, 3 * C_W),), (BF16,))
            parts, weights = [_attn_c(c_qkv, _bias_c(c_rpb[i]), seg_starts, seg_ends)], [wo]
        x, info, counts = _outproj_ln(parts, weights, xs, ln1_g[l], ln1_b[l], w_router, b_router)
        xs = _moe_layer(l, x, info, counts, w_router, b_router, w_gate, w_up, w_down, ln2_g[l], ln2_b[l],
                        io_rows if l == DEPTH - 1 else (n,))
    return xs


def kernel(x_prompt, x_sample, rel_bias, w_in_ab, a_sink, w_out_ab, w_in_c, c_rpb, w_out_c,
           ln1_g, ln1_b, ln2_g, ln2_b, router_g_w, router_g_b, router_e_w, router_e_b,
           w_gate, w_up, w_down):
    lens = [x_prompt.shape[1]] * x_prompt.shape[0] + [x_sample.shape[1]] * x_sample.shape[0]
    seg_ends = tuple(int(v) for v in np.cumsum(lens))
    seg_starts = tuple(e - n for e, n in zip(seg_ends, lens))
    for n in lens:
        assert n % ATT_TB == 0 and n % A_TB == 0 and n // GRID_W >= NA_ROWS
    xs = [x_prompt.reshape(-1, D_MODEL), x_sample.reshape(-1, D_MODEL)]
    y_p, y_s = _trunk(xs, seg_starts, seg_ends, rel_bias, w_in_ab, a_sink, w_out_ab, w_in_c, c_rpb, w_out_c,
                      ln1_g, ln1_b, ln2_g, ln2_b, router_g_w, router_g_b, router_e_w, router_e_b,
                      w_gate, w_up, w_down)
    return (y_p.reshape(x_prompt.shape), y_s.reshape(x_sample.shape))
```

```python
import functools
import math

import numpy as np
import jax
import jax.numpy as jnp
from jax import lax
from jax.experimental import pallas as pl
from jax.experimental.pallas import tpu as pltpu

F32 = jnp.float32
BF16 = jnp.bfloat16

D_MODEL = 1024
DEPTH = 4
HEAD_DIM = 64
LANES = 128
A_HEADS = 8
A_KV_HEADS = 2
A_WINDOW = 128
B_HEADS = 8
B_BRANCHES = ((128, 1), (512, 4), (2048, 16))
B_HALF = 64
C_HEADS = 16
GRID_W = 64
NA_ROWS = 8
NA_COLS = 16
REL_BUCKETS = 32
REL_MAX_DIST = 1024
N_GROUPS = 4
EXPERTS_PER_GROUP = 4
N_EXPERTS = 16
D_EXPERT = 512
N_PAIRS = 6
N_CLASSES = N_GROUPS * N_PAIRS
DEEPNORM_ALPHA = (2.0 * DEPTH) ** 0.25
LN_EPS = 1e-5
LOG2E = math.log2(math.e)
ATTN_SCALE = HEAD_DIM ** -0.5 * LOG2E
NEG_INF = -1e30

QA_W = A_HEADS * HEAD_DIM
KVA_W = A_KV_HEADS * HEAD_DIM
A_IN = QA_W + 2 * KVA_W
B_W = B_HEADS * HEAD_DIM
B_IN = 3 * B_W
C_W = C_HEADS * HEAD_DIM

ATT_TB = 1024
MM_TM = 512
MOE_TM = 256
ROW_TM = 1024
VMEM_LIMIT = 56 * 1024 * 1024


def _cparams(sem):
    return pltpu.CompilerParams(dimension_semantics=sem, vmem_limit_bytes=VMEM_LIMIT)


def _segment_flags(tok0, size, seg_starts, seg_ends):
    is_first = functools.reduce(jnp.logical_or, [tok0 == s for s in seg_starts])
    is_last = functools.reduce(jnp.logical_or, [tok0 + size == e for e in seg_ends])
    return is_first, is_last


def _t5_bucket_np(rel):
    half_b = REL_BUCKETS // 2
    max_exact = half_b // 2
    n = np.abs(rel)
    large = max_exact + (np.log(np.maximum(n, max_exact).astype(np.float32) / max_exact)
                         / math.log(REL_MAX_DIST / max_exact) * (half_b - max_exact)).astype(np.int32)
    large = np.minimum(large, half_b - 1)
    return np.where(rel > 0, half_b, 0) + np.where(n < max_exact, n, large)


def _banded_bias(table, half, dil, q_len=None):
    q_len = half if q_len is None else q_len
    rel = np.arange(q_len + 2 * half)[None, :] - half - np.arange(q_len)[:, None]
    bucket = jnp.asarray(_t5_bucket_np(rel * dil).astype(np.int32))
    hit = bucket[None] == jnp.arange(REL_BUCKETS, dtype=jnp.int32)[:, None, None]
    bias = jnp.sum(jnp.where(hit[:, None], table.astype(F32)[:, :, None, None], 0.0), axis=0)
    return jnp.where(jnp.asarray(np.abs(rel) <= half)[None], bias * LOG2E, NEG_INF)


def _bias_a(rel_bias):
    w = A_WINDOW
    ch = A_CHUNK
    b = _banded_bias(rel_bias[:, :A_HEADS], w, 1).reshape(A_HEADS, w // ch, ch, 3 * w)
    b = b.transpose(1, 0, 2, 3).reshape(w // ch, A_HEADS * ch, 3 * w)
    col = np.arange(3 * w)
    first = jnp.where(jnp.asarray(col < w), NEG_INF, b)
    last = jnp.where(jnp.asarray(col >= 2 * w), NEG_INF, b)
    return jnp.stack([b, first, last])


def _bias_b(rel_bias):
    out = []
    for (_, d), q in zip(B_BRANCHES, B_QBLK):
        b = _banded_bias(rel_bias[:, A_HEADS:], B_HALF, d, q)
        out.append(b.reshape(B_HEADS // 2, 2 * q, q + 2 * B_HALF))
    return out


def _bias_c(rpb):
    gw = GRID_W
    n_dr = 2 * NA_ROWS - 1
    side = gw - NA_COLS
    p = jnp.concatenate([jnp.repeat(rpb[..., :1], side, axis=-1), rpb.astype(F32),
                         jnp.repeat(rpb[..., -1:], side + 1, axis=-1)], axis=-1)
    z = jnp.broadcast_to(p[:, :, None, :], (C_HEADS, n_dr, gw, 2 * gw)).reshape(C_HEADS, n_dr, 2 * gw * gw)
    t = z[:, :, gw - 1:gw - 1 + gw * (2 * gw - 1)].reshape(C_HEADS, n_dr, gw, 2 * gw - 1)[..., :gw]
    cq = np.arange(gw)[:, None]
    w = np.arange(gw)[None, :]
    c0 = np.clip(cq - NA_COLS // 2, 0, gw - NA_COLS)
    t = jnp.where(jnp.asarray((w >= c0) & (w < c0 + NA_COLS)), t * LOG2E, NEG_INF)
    t = t.reshape(C_HEADS // 2, 2, n_dr, gw, gw).transpose(0, 1, 3, 2, 4)
    bias = jnp.stack([t[:, :, :, NA_ROWS - 1 - s:2 * NA_ROWS - 1 - s, :].reshape(C_HEADS // 2, 2, gw, NA_ROWS * gw)
                      for s in range(NA_ROWS)], axis=1)
    return bias.reshape(C_HEADS // 2, NA_ROWS, 2 * gw, NA_ROWS * gw)


def _segment_blocks(segs, tm):
    return tuple(a.shape[0] // tm for a in segs)


def _segment_specs(seg_blocks, tm, width):
    specs, start = [], 0
    for nb in seg_blocks:
        specs.append(pl.BlockSpec((tm, width), lambda i, *_, s=start, nb=nb: (jnp.clip(i - s, 0, nb - 1), 0)))
        start += nb
    return specs


def _segment_rows(i, refs, seg_blocks):
    x, start = refs[0][...], seg_blocks[0]
    for ref, nb in zip(refs[1:], seg_blocks[1:]):
        x = jnp.where(i >= start, ref[...], x)
        start += nb
    return x


def _inproj_kernel(*refs, splits, seg_blocks):
    n_seg = len(seg_blocks)
    w_ref = refs[n_seg]
    o_refs = refs[n_seg + 1:]
    x = _segment_rows(pl.program_id(0), refs[:n_seg], seg_blocks).astype(BF16)
    for o_ref, (lo, hi) in zip(o_refs, splits):
        o_ref[...] = jnp.dot(x, w_ref[:, lo:hi], preferred_element_type=F32).astype(o_ref.dtype)


def _inproj(xs, w, splits, dtypes):
    seg_blocks = _segment_blocks(xs, MM_TM)
    n = sum(seg_blocks) * MM_TM
    return pl.pallas_call(
        functools.partial(_inproj_kernel, splits=splits, seg_blocks=seg_blocks),
        grid=(n // MM_TM,),
        in_specs=_segment_specs(seg_blocks, MM_TM, D_MODEL) + [pl.BlockSpec(w.shape, lambda i: (0, 0))],
        out_specs=[pl.BlockSpec((MM_TM, hi - lo), lambda i: (i, 0)) for lo, hi in splits],
        out_shape=[jax.ShapeDtypeStruct((n, hi - lo), dt) for (lo, hi), dt in zip(splits, dtypes)],
        compiler_params=_cparams(("parallel",)),
        name="inproj",
    )(*xs, w)


def _layer_norm(z, g, b):
    mu = jnp.mean(z, axis=-1, keepdims=True)
    zc = z - mu
    var = jnp.mean(zc * zc, axis=-1, keepdims=True)
    return zc * lax.rsqrt(var + LN_EPS) * g + b


def _outproj_ln_kernel(*refs, n_parts, seg_blocks):
    n_seg = len(seg_blocks)
    o_refs = refs[:n_parts]
    w_refs = refs[n_parts:2 * n_parts]
    x_refs = refs[2 * n_parts:2 * n_parts + n_seg]
    g_ref, b_ref, whl_ref, br_ref, out_ref, info_ref, cnt_ref, run_scr, tri_scr = refs[2 * n_parts + n_seg:]
    i = pl.program_id(0)
    h = DEEPNORM_ALPHA * _segment_rows(i, x_refs, seg_blocks)
    for o_ref, w_ref in zip(o_refs, w_refs):
        h = h + jnp.dot(o_ref[...], w_ref[...], preferred_element_type=F32)
    out = _layer_norm(h, g_ref[...], b_ref[...])
    out_ref[...] = out
    _route(i, out, whl_ref, br_ref, info_ref, cnt_ref, run_scr, tri_scr)


def _outproj_ln(parts, weights, xs, g, b, w_router, b_router):
    tm = MM_TM
    seg_blocks = _segment_blocks(xs, tm)
    n = sum(seg_blocks) * tm
    n_parts = len(parts)
    wh = w_router.astype(BF16)
    whl = jnp.concatenate([wh, (w_router - wh.astype(F32)).astype(BF16)], axis=1)
    const = lambda shape: pl.BlockSpec(shape, lambda i: (0, 0))
    return pl.pallas_call(
        functools.partial(_outproj_ln_kernel, n_parts=n_parts, seg_blocks=seg_blocks),
        grid=(n // tm,),
        in_specs=([pl.BlockSpec((tm, p.shape[1]), lambda i: (i, 0)) for p in parts]
                  + [const(w.shape) for w in weights]
                  + _segment_specs(seg_blocks, tm, D_MODEL)
                  + [const((1, D_MODEL)), const((1, D_MODEL)),
                     const((D_MODEL, 2 * LANES)), const((1, LANES))]),
        out_specs=[pl.BlockSpec((tm, D_MODEL), lambda i: (i, 0)),
                   pl.BlockSpec((None, SUBLANES, tm), lambda i: (i, 0, 0)),
                   const((CLASS_ROWS, LANES))],
        out_shape=[jax.ShapeDtypeStruct((n, D_MODEL), F32),
                   jax.ShapeDtypeStruct((n // tm, SUBLANES, tm), F32),
                   jax.ShapeDtypeStruct((CLASS_ROWS, LANES), F32)],
        scratch_shapes=[pltpu.VMEM((CLASS_ROWS, LANES), F32), pltpu.VMEM((tm, tm), BF16)],
        compiler_params=_cparams(("arbitrary",)),
        name="outproj_ln",
    )(*parts, *weights, *xs, g.reshape(1, D_MODEL), b.reshape(1, D_MODEL), whl, b_router)


ATT_DEPTH = 3
B_DEPTH = 10


def _staged(n, weights, values, depth=ATT_DEPTH):
    for i in range(min(depth, n)):
        weights(i)
    for i in range(n):
        if i + depth < n:
            weights(i + depth)
        values(i)


def _attn_a_kernel(q_ref, kvm_ref, kvp_ref, kvn_ref, bias_ref, sink_ref, o_ref, kv_scr, p_scr,
                   *, seg_starts, seg_ends):
    w = A_WINDOW
    tb = A_TB
    n_sub = tb // w
    tok0 = pl.program_id(0) * tb
    is_first, is_last = _segment_flags(tok0, tb, seg_starts, seg_ends)
    kv_scr[0:w, :] = kvp_ref[...]
    kv_scr[w:w + tb, :] = kvm_ref[...]
    kv_scr[w + tb:, :] = kvn_ref[...]
    low = lax.broadcasted_iota(jnp.int32, (1, LANES), 1) < HEAD_DIM
    ch = A_CHUNK
    per = w // ch

    sinks = [jnp.concatenate([jnp.full((ch, 1), sink_ref[c + 4 * g], F32) for c in range(4)], axis=0)
             for g in range(A_KV_HEADS)]

    sink = jnp.concatenate(sinks, axis=0)

    def scores(t):
        j = t // per
        q = q_ref[t * ch:(t + 1) * ch, :]
        qg = jnp.concatenate([jnp.where(low, q[:, c * LANES:(c + 1) * LANES], 0) for c in range(4)]
                             + [jnp.where(low, 0, q[:, c * LANES:(c + 1) * LANES]) for c in range(4)], axis=0)
        k2 = kv_scr[j * w:(j + 3) * w, :LANES]
        return lax.dot_general(qg, k2, (((1,), (1,)), ((), ())), preferred_element_type=F32)

    rows = A_HEADS * ch
    rdens = {}

    def weights(t):
        j = t // per
        if j == 0:
            variant = jnp.where(is_first, 1, 0)
        elif j == n_sub - 1:
            variant = jnp.where(is_last, 2, 0)
        else:
            variant = 0
        s = scores(t) + bias_ref[variant, t % per]
        m = jnp.maximum(jnp.max(s, axis=-1, keepdims=True), sink)
        e = jnp.exp2(s - m)
        rdens[t] = 1.0 / (jnp.sum(e, axis=-1, keepdims=True) + jnp.exp2(sink - m))
        p_scr[t * rows:(t + 1) * rows, :] = e.astype(BF16)

    def values(t):
        j = t // per
        v2 = kv_scr[j * w:(j + 3) * w, LANES:]
        pv = jnp.dot(p_scr[t * rows:(t + 1) * rows, :], v2, preferred_element_type=F32) * rdens[t]
        for c in range(4):
            oc = jnp.where(low, pv[c * ch:(c + 1) * ch], pv[(4 + c) * ch:(5 + c) * ch])
            o_ref[t * ch:(t + 1) * ch, c * LANES:(c + 1) * LANES] = oc.astype(o_ref.dtype)

    _staged(tb // ch, weights, values)


A_CHUNK = 128
A_TB = 2 * ATT_TB


def _attn_a(a_qkv, bias, sink, seg_starts, seg_ends):
    n = a_qkv.shape[0]
    w = A_WINDOW
    tb = A_TB
    sub = tb // w
    nhb = n // w
    kv_col = QA_W // (2 * LANES)
    return pl.pallas_call(
        functools.partial(_attn_a_kernel, seg_starts=seg_starts, seg_ends=seg_ends),
        grid=(n // tb,),
        in_specs=[pl.BlockSpec((tb, QA_W), lambda i: (i, 0)),
                  pl.BlockSpec((tb, 2 * LANES), lambda i: (i, kv_col)),
                  pl.BlockSpec((w, 2 * LANES), lambda i: (jnp.maximum(i * sub - 1, 0), kv_col)),
                  pl.BlockSpec((w, 2 * LANES), lambda i: (jnp.minimum((i + 1) * sub, nhb - 1), kv_col)),
                  pl.BlockSpec(bias.shape, lambda i: (0,) * bias.ndim),
                  pl.BlockSpec(memory_space=pltpu.SMEM)],
        out_specs=pl.BlockSpec((tb, QA_W), lambda i: (i, 0)),
        out_shape=jax.ShapeDtypeStruct((n, QA_W), BF16),
        scratch_shapes=[pltpu.VMEM((tb + 2 * w, 2 * LANES), BF16),
                        pltpu.VMEM((A_HEADS * tb, 3 * w), BF16)],
        compiler_params=_cparams(("parallel",)),
        name="attn_a",
    )(a_qkv, a_qkv, a_qkv, a_qkv, bias, sink)


def _attn_b_kernel(q_ref, kp_ref, km_ref, kn_ref, vp_ref, vm_ref, vn_ref, bias0_ref, bias1_ref, bias2_ref, o_ref,
                   k_scr, v_scr, o_scr, m_scr, l_scr, p0_scr, p1_scr, p2_scr, *, seg_starts, seg_ends):
    tb = B_TB
    h = B_HALF
    tok0 = pl.program_id(1) * tb
    is_first, is_last = _segment_flags(tok0, tb, seg_starts, seg_ends)
    k_scr[0:tb, :] = kp_ref[...]
    k_scr[tb:2 * tb, :] = km_ref[...]
    k_scr[2 * tb:, :] = kn_ref[...]
    v_scr[0:tb, :] = vp_ref[...]
    v_scr[tb:2 * tb, :] = vm_ref[...]
    v_scr[2 * tb:, :] = vn_ref[...]
    lane = lax.broadcasted_iota(jnp.int32, (1, LANES), 1)
    low = lane < HEAD_DIM
    bias_refs = (bias0_ref, bias1_ref, bias2_ref)
    p_scrs = (p0_scr, p1_scr, p2_scr)

    pens = []
    for q in B_QBLK:
        col = lax.broadcasted_iota(jnp.int32, (1, q + 2 * h), 1)
        pens.append((jnp.where(jnp.logical_and(col < h, is_first), NEG_INF, 0.0),
                     jnp.where(jnp.logical_and(col >= q + h, is_last), NEG_INF, 0.0)))

    def slices(br, d, r, b):
        q = B_QBLK[br]
        row0 = r + q * d * b
        if d == 1:
            return pl.ds(row0, q), pl.ds(tb + row0 - h, q + 2 * h)
        return pl.ds(row0, q, stride=d), pl.ds(tb + row0 - h * d, q + 2 * h, stride=d)

    def scores(br, d, r, b):
        qs, ks = slices(br, d, r, b)
        q = q_ref[qs, :].astype(BF16)
        k = k_scr[ks, :].astype(BF16)
        qq = jnp.concatenate([jnp.where(low, q, 0), jnp.where(low, 0, q)], axis=0)
        return lax.dot_general(qq, k, (((1,), (1,)), ((), ())), preferred_element_type=F32)

    tiles = [(br, d, r, b, r * (tb // (B_QBLK[br] * d)) + b) for br, (_, d) in reversed(list(enumerate(B_BRANCHES)))
             for r in range(d) for b in range(tb // (B_QBLK[br] * d))]

    def weights(i):
        br, d, r, b, t = tiles[i]
        q = B_QBLK[br]
        qs, _ = slices(br, d, r, b)
        s = scores(br, d, r, b) + bias_refs[br][...]
        if b == 0:
            s = s + pens[br][0]
        if b == tb // (q * d) - 1:
            s = s + pens[br][1]
        m = jnp.max(s, axis=-1, keepdims=True)
        e = jnp.exp2(s - m)
        l = jnp.sum(e, axis=-1, keepdims=True)
        p_scrs[br][t * 2 * q:(t + 1) * 2 * q, :] = e.astype(BF16)
        m_scr[br, qs, :] = jnp.where(low, m[:q], m[q:])
        l_scr[br, qs, :] = jnp.where(low, l[:q], l[q:])

    def values(i):
        br, d, r, b, t = tiles[i]
        q = B_QBLK[br]
        qs, ks = slices(br, d, r, b)
        v = v_scr[ks, :].astype(BF16)
        pv = jnp.dot(p_scrs[br][t * 2 * q:(t + 1) * 2 * q, :], v, preferred_element_type=F32)
        o_scr[br, qs, :] = jnp.where(low, pv[:q], pv[q:])

    _staged(len(tiles), weights, values, B_DEPTH)

    m_all = jnp.maximum(jnp.maximum(m_scr[0], m_scr[1]), m_scr[2])
    num = jnp.zeros((tb, LANES), F32)
    den = jnp.zeros((tb, LANES), F32)
    for br in range(len(B_BRANCHES)):
        a = jnp.exp2(m_scr[br] - m_all)
        num = num + a * o_scr[br]
        den = den + a * l_scr[br]
    o_ref[...] = (num / den).astype(o_ref.dtype)


B_TB = ATT_TB
B_QBLK = tuple(B_HALF for _ in B_BRANCHES)


def _attn_b(b_qkv, biases, seg_starts, seg_ends):
    n = b_qkv.shape[0]
    tb = B_TB
    nblk = n // tb
    npair = B_HEADS // 2
    prev = lambda i: jnp.maximum(i - 1, 0)
    nxt = lambda i: jnp.minimum(i + 1, nblk - 1)
    blk = lambda rowf, off: pl.BlockSpec((tb, LANES), lambda c, i: (rowf(i), off + c))
    same = lambda i: i
    stat = pltpu.VMEM((len(B_BRANCHES), tb, LANES), F32)
    return pl.pallas_call(
        functools.partial(_attn_b_kernel, seg_starts=seg_starts, seg_ends=seg_ends),
        grid=(npair, nblk),
        in_specs=[blk(same, 0),
                  blk(prev, npair), blk(same, npair), blk(nxt, npair),
                  blk(prev, 2 * npair), blk(same, 2 * npair), blk(nxt, 2 * npair)]
                 + [pl.BlockSpec((None,) + b.shape[1:], lambda c, i: (c, 0, 0)) for b in biases],
        out_specs=pl.BlockSpec((tb, LANES), lambda c, i: (i, c)),
        out_shape=jax.ShapeDtypeStruct((n, B_W), BF16),
        scratch_shapes=[pltpu.VMEM((3 * tb, LANES), F32), pltpu.VMEM((3 * tb, LANES), F32),
                        stat, stat, stat]
                       + [pltpu.VMEM((2 * tb, q + 2 * B_HALF), BF16) for q in B_QBLK],
        compiler_params=_cparams(("parallel", "parallel")),
        name="attn_b",
    )(b_qkv, b_qkv, b_qkv, b_qkv, b_qkv, b_qkv, b_qkv, *biases)


C_HALO = (NA_ROWS // 2) * GRID_W

def _attn_c_kernel(q_ref, kp_ref, km_ref, kn_ref, vp_ref, vm_ref, vn_ref, bias_ref, o_ref,
                   k_scr, v_scr, p_scr, *, seg_starts, seg_ends):
    tb = C_TB
    gw = GRID_W
    nkeys = NA_ROWS * gw
    tok0 = pl.program_id(1) * tb
    k_scr[0:C_HALO, :] = kp_ref[...]
    k_scr[C_HALO:C_HALO + tb, :] = km_ref[...]
    k_scr[C_HALO + tb:, :] = kn_ref[...]
    v_scr[0:C_HALO, :] = vp_ref[...]
    v_scr[C_HALO:C_HALO + tb, :] = vm_ref[...]
    v_scr[C_HALO + tb:, :] = vn_ref[...]
    seg_row0 = jnp.int32(0)
    seg_rows = jnp.int32(0)
    for s, e in zip(seg_starts, seg_ends):
        inside = jnp.logical_and(tok0 >= s, tok0 < e)
        seg_row0 = jnp.where(inside, s // gw, seg_row0)
        seg_rows = jnp.where(inside, (e - s) // gw, seg_rows)
    lane = lax.broadcasted_iota(jnp.int32, (1, LANES), 1)
    low = lane < HEAD_DIM

    def window(rr):
        rs = tok0 // gw + rr - seg_row0
        start = jnp.clip(rs - NA_ROWS // 2, 0, seg_rows - NA_ROWS)
        shift = rs - start
        return shift, pl.ds(pl.multiple_of((rr + NA_ROWS // 2 - shift) * gw, gw), nkeys)

    def scores(rr):
        _, ks = window(rr)
        q = q_ref[rr * gw:(rr + 1) * gw, :]
        qq = jnp.concatenate([jnp.where(low, q, 0), jnp.where(low, 0, q)], axis=0)
        return lax.dot_general(qq, k_scr[ks, :], (((1,), (1,)), ((), ())), preferred_element_type=F32)

    n_rows = tb // gw
    rdens = {}

    def weights(rr):
        shift, _ = window(rr)
        s = scores(rr) + bias_ref[shift]
        m = jnp.max(s, axis=-1, keepdims=True)
        e = jnp.exp2(s - m)
        rdens[rr] = 1.0 / jnp.sum(e, axis=-1, keepdims=True)
        p_scr[rr * 2 * gw:(rr + 1) * 2 * gw, :] = e.astype(BF16)

    def values(rr):
        _, ks = window(rr)
        pv = jnp.dot(p_scr[rr * 2 * gw:(rr + 1) * 2 * gw, :], v_scr[ks, :], preferred_element_type=F32) * rdens[rr]
        o_ref[rr * gw:(rr + 1) * gw, :] = jnp.where(low, pv[:gw], pv[gw:]).astype(o_ref.dtype)

    _staged(n_rows, weights, values)


C_TB = 2 * ATT_TB


def _attn_c(c_qkv, bias, seg_starts, seg_ends):
    n = c_qkv.shape[0]
    tb = C_TB
    npair = C_HEADS // 2
    sub = tb // C_HALO
    nhb = n // C_HALO
    main = lambda off: pl.BlockSpec((tb, LANES), lambda c, i: (i, off + c))
    prev = lambda off: pl.BlockSpec((C_HALO, LANES), lambda c, i: (jnp.maximum(i * sub - 1, 0), off + c))
    nxt = lambda off: pl.BlockSpec((C_HALO, LANES),
                                   lambda c, i: (jnp.minimum((i + 1) * sub, nhb - 1), off + c))
    return pl.pallas_call(
        functools.partial(_attn_c_kernel, seg_starts=seg_starts, seg_ends=seg_ends),
        grid=(npair, n // tb),
        in_specs=[main(0),
                  prev(npair), main(npair), nxt(npair),
                  prev(2 * npair), main(2 * npair), nxt(2 * npair),
                  pl.BlockSpec((None,) + bias.shape[1:], lambda c, i: (c, 0, 0, 0))],
        out_specs=pl.BlockSpec((tb, LANES), lambda c, i: (i, c)),
        out_shape=jax.ShapeDtypeStruct((n, C_W), BF16),
        scratch_shapes=[pltpu.VMEM((tb + 2 * C_HALO, LANES), BF16),
                        pltpu.VMEM((tb + 2 * C_HALO, LANES), BF16),
                        pltpu.VMEM((2 * tb, NA_ROWS * GRID_W), BF16)],
        compiler_params=_cparams(("parallel", "parallel")),
        name="attn_c",
    )(c_qkv, c_qkv, c_qkv, c_qkv, c_qkv, c_qkv, c_qkv, bias)


def _route(i, x, whl_ref, b_ref, info_ref, cnt_ref, run_scr, tri_scr):
    tm = x.shape[0]

    @pl.when(i == 0)
    def _():
        run_scr[...] = jnp.zeros_like(run_scr)

    xh = x.astype(BF16)
    xl = (x - xh.astype(F32)).astype(BF16)
    hh_hl = jnp.dot(xh, whl_ref[...], preferred_element_type=F32)
    logits = (hh_hl[:, :LANES] + jnp.dot(xl, whl_ref[:, :LANES], preferred_element_type=F32)
              + hh_hl[:, LANES:]) + b_ref[...]
    lt = logits.T
    first = lambda hit, n: jnp.min(jnp.where(hit, lax.broadcasted_iota(jnp.int32, (n, tm), 0).astype(F32),
                                             float(n)), axis=0, keepdims=True)
    lg = lt[0:N_GROUPS]
    g_sel = first(lg == jnp.max(lg, axis=0, keepdims=True), N_GROUPS)
    le = jnp.zeros((EXPERTS_PER_GROUP, tm), F32)
    for g in range(N_GROUPS):
        lo = N_GROUPS + g * EXPERTS_PER_GROUP
        le = jnp.where(g_sel == g, lt[lo:lo + EXPERTS_PER_GROUP], le)
    row = lax.broadcasted_iota(jnp.int32, (EXPERTS_PER_GROUP, tm), 0).astype(F32)
    i1 = first(le == jnp.max(le, axis=0, keepdims=True), EXPERTS_PER_GROUP)
    rest = jnp.where(row == i1, NEG_INF, le)
    i2 = first(jnp.logical_and(rest == jnp.max(rest, axis=0, keepdims=True), row != i1), EXPERTS_PER_GROUP)
    a = jnp.minimum(i1, i2)
    b = jnp.maximum(i1, i2)
    cls = g_sel * N_PAIRS + a * 3.0 - jnp.where(a == 2.0, 1.0, 0.0) + (b - a - 1.0)

    @pl.when(i == 0)
    def _():
        tri_scr[...] = (lax.broadcasted_iota(jnp.int32, (tm, tm), 0)
                        < lax.broadcasted_iota(jnp.int32, (tm, tm), 1)).astype(BF16)

    onehot = lax.broadcasted_iota(jnp.int32, (CLASS_ROWS, tm), 0).astype(F32) == cls
    before = jnp.dot(onehot.astype(BF16), tri_scr[...], preferred_element_type=F32) + run_scr[:, 0:1]
    rank = jnp.sum(jnp.where(onehot, before, 0.0), axis=0, keepdims=True)
    run_scr[...] = run_scr[...] + jnp.sum(onehot.astype(F32), axis=1, keepdims=True)
    srow = lax.broadcasted_iota(jnp.int32, (SUBLANES, tm), 0)
    info_ref[...] = jnp.where(srow == 0, cls, jnp.where(srow == 1, rank, 0.0))
    cnt_ref[...] = run_scr[...]


CLASS_ROWS = 32
SUBLANES = 8


def _tile_copy(src, src_row, dst, dst_row, sem):
    return pltpu.make_async_copy(src.at[pl.ds(pl.multiple_of(src_row, SUBLANES), SUBLANES), :],
                                 dst.at[pl.ds(pl.multiple_of(dst_row, SUBLANES), SUBLANES), :], sem)


ROW_UNROLL = 8


def _start_rows(copy, n):
    def body(g, carry):
        for u in range(ROW_UNROLL):
            copy(g * ROW_UNROLL + u).start(priority=u % 2)
        return carry

    lax.fori_loop(0, n // ROW_UNROLL, body, 0)


def _dispatch_kernel(dest_ref, pad_ref, x_ref, xs_ref, rec_scr, zero_scr, sems, zsem):
    i = pl.program_id(0)
    last = pl.num_programs(0) - 1
    tm = x_ref.shape[0]
    rows = tm * SUBLANES
    slot = i % 2
    tile_rows = MOE_TM * SUBLANES

    def zero_copy(c):
        start = pl.multiple_of(pad_ref[c] * SUBLANES, tile_rows)
        return pltpu.make_async_copy(zero_scr, xs_ref.at[pl.ds(start, tile_rows), :], zsem)

    @pl.when(i == 0)
    def _():
        zero_scr[...] = jnp.zeros_like(zero_scr)
        for c in range(pad_ref.shape[0]):
            @pl.when(pad_ref[c] >= 0)
            def _():
                zero_copy(c).start()
        for c in range(pad_ref.shape[0]):
            @pl.when(pad_ref[c] >= 0)
            def _():
                zero_copy(c).wait()

    base = i * tm
    for s in range(2):
        @pl.when(slot == s)
        def _(s=s):
            for j in range(D_MODEL // LANES):
                rec_scr[s, pl.ds(j, tm, stride=SUBLANES), :] = x_ref[:, j * LANES:(j + 1) * LANES]
            _start_rows(lambda r: _tile_copy(rec_scr.at[s], r * SUBLANES, xs_ref, dest_ref[base + r], sems.at[s]),
                        tm)

    def wait_step(s):
        pltpu.make_async_copy(rec_scr.at[s], xs_ref.at[pl.ds(0, rows), :], sems.at[s]).wait()

    @pl.when(i > 0)
    def _():
        wait_step(1 - slot)

    @pl.when(i == last)
    def _():
        wait_step(slot)


def _dispatch(dest, pad_start, x, n_sorted):
    n = x.shape[0]
    tm = ROW_TM
    return pl.pallas_call(
        _dispatch_kernel,
        grid_spec=pltpu.PrefetchScalarGridSpec(
            num_scalar_prefetch=2,
            grid=(n // tm,),
            in_specs=[pl.BlockSpec((tm, D_MODEL), lambda i, d, p: (i, 0))],
            out_specs=pl.BlockSpec(memory_space=pl.ANY),
            scratch_shapes=[pltpu.VMEM((2, tm * SUBLANES, LANES), F32),
                            pltpu.VMEM((MOE_TM * SUBLANES, LANES), F32),
                            pltpu.SemaphoreType.DMA((2,)), pltpu.SemaphoreType.DMA]),
        out_shape=jax.ShapeDtypeStruct((n_sorted * SUBLANES, LANES), F32),
        compiler_params=_cparams(("arbitrary",)),
        name="dispatch",
    )(dest, pad_start, x)


def _expert_kernel(ea_ref, eb_ref, nt_ref, xs_ref, wr_ref, br_ref,
                   wga_ref, wua_ref, wda_ref, wgb_ref, wub_ref, wdb_ref, g_ref, b_ref, ys_ref, hid_scr):
    p = pl.program_id(0)
    tm = MOE_TM

    @pl.when(p >= nt_ref[0])
    def _():
        ys_ref[...] = jnp.zeros_like(ys_ref)

    @pl.when(p < nt_ref[0])
    def _():
        x = jnp.concatenate([xs_ref[pl.ds(j, tm, stride=SUBLANES), :] for j in range(D_MODEL // LANES)],
                            axis=1)
        xb = x.astype(BF16)

        logits = jnp.dot(xb, wr_ref[...], preferred_element_type=F32) + br_ref[...]
        lane = lax.broadcasted_iota(jnp.int32, (tm, LANES), 1)
        lane_a = N_GROUPS + ea_ref[p]
        lane_b = N_GROUPS + eb_ref[p]
        grp = ea_ref[p] // EXPERTS_PER_GROUP
        is_g = lane < N_GROUPS
        mg = jnp.max(jnp.where(is_g, logits, NEG_INF), axis=-1, keepdims=True)
        eg = jnp.where(is_g, jnp.exp(logits - mg), 0.0)
        g_gate = (jnp.sum(jnp.where(lane == grp, eg, 0.0), axis=-1, keepdims=True)
                  / jnp.sum(eg, axis=-1, keepdims=True))
        l_a = jnp.sum(jnp.where(lane == lane_a, logits, 0.0), axis=-1, keepdims=True)
        l_b = jnp.sum(jnp.where(lane == lane_b, logits, 0.0), axis=-1, keepdims=True)
        mx = jnp.maximum(l_a, l_b)
        p_a = jnp.exp(l_a - mx)
        p_b = jnp.exp(l_b - mx)
        scale = g_gate / (p_a + p_b)
        w_a = p_a * scale
        w_b = p_b * scale

        for e, (wg_ref, wu_ref) in enumerate(((wga_ref, wua_ref), (wgb_ref, wub_ref))):
            gate = jnp.dot(xb, wg_ref[...], preferred_element_type=F32)
            up = jnp.dot(xb, wu_ref[...], preferred_element_type=F32)
            hid_scr[e] = (gate * (1.0 / (1.0 + jnp.exp(-gate))) * up).astype(BF16)
        y = (w_a * jnp.dot(hid_scr[0], wda_ref[...], preferred_element_type=F32)
             + w_b * jnp.dot(hid_scr[1], wdb_ref[...], preferred_element_type=F32))
        out = _layer_norm(DEEPNORM_ALPHA * x + y, g_ref[...], b_ref[...])
        for j in range(D_MODEL // LANES):
            ys_ref[pl.ds(j, tm, stride=SUBLANES), :] = out[:, j * LANES:(j + 1) * LANES]


def _experts(layer, tile_ea, tile_eb, n_tiles, xs, w_router, b_router, w_gate, w_up, w_down, g, b):
    tm = MOE_TM
    rows = tm * SUBLANES
    n_grid = xs.shape[0] // rows
    last = lambda p, nt: jnp.maximum(jnp.minimum(p, nt[0] - 1), 0)
    wspec_a = lambda shape: pl.BlockSpec((None, None) + shape, lambda p, ea, eb, nt: (layer, ea[p], 0, 0))
    wspec_b = lambda shape: pl.BlockSpec((None, None) + shape, lambda p, ea, eb, nt: (layer, eb[p], 0, 0))
    const = lambda shape: pl.BlockSpec(shape, lambda p, ea, eb, nt: (0, 0))
    up_shape = (D_MODEL, D_EXPERT)
    dn_shape = (D_EXPERT, D_MODEL)
    return pl.pallas_call(
        _expert_kernel,
        grid_spec=pltpu.PrefetchScalarGridSpec(
            num_scalar_prefetch=3,
            grid=(n_grid,),
            in_specs=[pl.BlockSpec((rows, LANES), lambda p, ea, eb, nt: (last(p, nt), 0)),
                      const((D_MODEL, LANES)), const((1, LANES)),
                      wspec_a(up_shape), wspec_a(up_shape), wspec_a(dn_shape),
                      wspec_b(up_shape), wspec_b(up_shape), wspec_b(dn_shape),
                      const((1, D_MODEL)), const((1, D_MODEL))],
            out_specs=pl.BlockSpec((rows, LANES), lambda p, ea, eb, nt: (p, 0)),
            scratch_shapes=[pltpu.VMEM((2, tm, D_EXPERT), BF16)]),
        out_shape=jax.ShapeDtypeStruct(xs.shape, F32),
        compiler_params=_cparams(("arbitrary",)),
        name="experts",
    )(tile_ea, tile_eb, n_tiles, xs, w_router.astype(BF16), b_router,
      w_gate, w_up, w_down, w_gate, w_up, w_down, g.reshape(1, D_MODEL), b.reshape(1, D_MODEL))


def _gather_kernel(dest_ref, ys_ref, *refs, seg_blocks):
    out_refs = refs[:len(seg_blocks)]
    rec_scr, sems = refs[len(seg_blocks):]
    i = pl.program_id(0)
    n_steps = pl.num_programs(0)
    tm = out_refs[0].shape[0]
    rows = tm * SUBLANES
    slot = i % 2

    def fetch(s, buf):
        base = s * tm
        _start_rows(lambda r: _tile_copy(ys_ref, dest_ref[base + r], rec_scr.at[buf], r * SUBLANES, sems.at[buf]),
                    tm)

    @pl.when(i == 0)
    def _():
        fetch(0, 0)

    for buf in range(2):
        @pl.when(jnp.logical_and(i + 1 < n_steps, (i + 1) % 2 == buf))
        def _(buf=buf):
            fetch(i + 1, buf)

    pltpu.make_async_copy(ys_ref.at[pl.ds(0, rows), :], rec_scr.at[slot], sems.at[slot]).wait()

    start = 0
    for out_ref, nb in zip(out_refs, seg_blocks):
        @pl.when(jnp.logical_and(i >= start, i < start + nb))
        def _(out_ref=out_ref):
            for j in range(D_MODEL // LANES):
                out_ref[:, j * LANES:(j + 1) * LANES] = rec_scr[slot, pl.ds(j, tm, stride=SUBLANES), :]
        start += nb


def _gather_rows(dest, ys, seg_rows):
    tm = ROW_TM
    seg_blocks = tuple(r // tm for r in seg_rows)
    return pl.pallas_call(
        functools.partial(_gather_kernel, seg_blocks=seg_blocks),
        grid_spec=pltpu.PrefetchScalarGridSpec(
            num_scalar_prefetch=1,
            grid=(sum(seg_blocks),),
            in_specs=[pl.BlockSpec(memory_space=pl.ANY)],
            out_specs=_segment_specs(seg_blocks, tm, D_MODEL),
            scratch_shapes=[pltpu.VMEM((2, tm * SUBLANES, LANES), F32), pltpu.SemaphoreType.DMA((2,))]),
        out_shape=[jax.ShapeDtypeStruct((r, D_MODEL), F32) for r in seg_rows],
        compiler_params=_cparams(("arbitrary",)),
        name="gather_rows",
    )(dest, ys)


_PAIR_A = np.array([0, 0, 0, 1, 1, 2], np.int32)
_PAIR_B = np.array([1, 2, 3, 2, 3, 3], np.int32)


def _moe_layer(layer, x, info, counts, w_router, b_router, w_gate, w_up, w_down, g, b, out_rows):
    n = x.shape[0]
    tm = MOE_TM
    n_sorted = n + N_CLASSES * tm
    cls = info[:, 0, :].reshape(n).astype(jnp.int32)
    rank = info[:, 1, :].reshape(n).astype(jnp.int32)
    counts = counts[:N_CLASSES, 0].astype(jnp.int32)
    padded = (counts + tm - 1) // tm * tm
    classes = jnp.arange(N_CLASSES, dtype=jnp.int32)
    ends = jnp.sum(jnp.where(classes[None, :] <= classes[:, None], padded[None, :], 0), axis=1)
    offs = ends - padded
    total = ends[N_CLASSES - 1]
    dest = rank + jnp.sum(jnp.where(cls[:, None] == classes[None, :], offs[None, :], 0), axis=1)
    unused = total + classes * tm
    pad_start = jnp.concatenate([jnp.where(padded > 0, ends - tm, -1),
                                 jnp.where(unused < n_sorted, unused, -1)]).astype(jnp.int32)
    tile_start = jnp.arange(n_sorted // tm, dtype=jnp.int32) * tm
    tile_start = jnp.minimum(tile_start, total - tm)
    tile_cls = jnp.sum((ends[None, :] <= tile_start[:, None]).astype(jnp.int32), axis=1)
    pair = tile_cls % N_PAIRS
    pair_a = jnp.sum(jnp.where(pair[:, None] == np.arange(N_PAIRS)[None, :], _PAIR_A[None, :], 0), axis=1)
    pair_b = jnp.sum(jnp.where(pair[:, None] == np.arange(N_PAIRS)[None, :], _PAIR_B[None, :], 0), axis=1)
    grp = tile_cls // N_PAIRS
    tile_ea = (grp * EXPERTS_PER_GROUP + pair_a).astype(jnp.int32)
    tile_eb = (grp * EXPERTS_PER_GROUP + pair_b).astype(jnp.int32)
    n_tiles = (total // tm).astype(jnp.int32).reshape(1)
    dest_row = (dest * SUBLANES).astype(jnp.int32)
    xs = _dispatch(dest_row, pad_start, x, n_sorted)
    ys = _experts(layer, tile_ea, tile_eb, n_tiles, xs, w_router, b_router, w_gate, w_up, w_down, g, b)
    return _gather_rows(dest_row, ys, out_rows)


_A_ORDER = np.array([0, 4, 1, 5, 2, 6, 3, 7])


def _prep_ab(w_in, w_out):
    qa = w_in[:, :QA_W].reshape(D_MODEL, A_HEADS, HEAD_DIM)[:, _A_ORDER].reshape(D_MODEL, QA_W) * ATTN_SCALE
    kva = w_in[:, QA_W:A_IN]
    qb = w_in[:, A_IN:A_IN + B_W] * ATTN_SCALE
    kvb = w_in[:, A_IN + B_W:]
    w = jnp.concatenate([qa, kva, qb, kvb], axis=1).astype(BF16)
    wo_a = w_out[:QA_W].reshape(A_HEADS, HEAD_DIM, D_MODEL)[_A_ORDER].reshape(QA_W, D_MODEL).astype(BF16)
    wo_b = w_out[QA_W:].astype(BF16)
    return w, wo_a, wo_b


def _prep_c(w_in, w_out):
    w = jnp.concatenate([w_in[:, :C_W] * ATTN_SCALE, w_in[:, C_W:]], axis=1).astype(BF16)
    return w, w_out.astype(BF16)


def _trunk(xs, seg_starts, seg_ends, rel_bias, w_in_ab, a_sink, w_out_ab, w_in_c, c_rpb, w_out_c,
           ln1_g, ln1_b, ln2_g, ln2_b, router_g_w, router_g_b, router_e_w, router_e_b,
           w_gate, w_up, w_down):
    io_rows = tuple(a.shape[0] for a in xs)
    n = sum(io_rows)
    bias_a = _bias_a(rel_bias)
    bias_b = _bias_b(rel_bias)
    w_gate = w_gate.astype(BF16)
    w_up = w_up.astype(BF16)
    w_down = w_down.astype(BF16)
    for l in range(DEPTH):
        i = l // 2
        pad = LANES - N_GROUPS - N_EXPERTS
        w_router = jnp.pad(jnp.concatenate([router_g_w[l], router_e_w[l]], axis=1), ((0, 0), (0, pad)))
        b_router = jnp.pad(jnp.concatenate([router_g_b[l], router_e_b[l]]), (0, pad)).reshape(1, LANES)
        if l % 2 == 0:
            w, wo_a, wo_b = _prep_ab(w_in_ab[i], w_out_ab[i])
            a_qkv, b_qkv = _inproj(xs, w, ((0, A_IN), (A_IN, A_IN + B_IN)), (BF16, F32))
            o_a = _attn_a(a_qkv, bias_a, a_sink[i].astype(F32) * LOG2E, seg_starts, seg_ends)
            o_b = _attn_b(b_qkv, bias_b, seg_starts, seg_ends)
            parts, weights = [o_a, o_b], [wo_a, wo_b]
        else:
            w, wo = _prep_c(w_in_c[i], w_out_c[i])
            (c_qkv,) = _inproj(xs, w, ((0, 3 * C_W),), (BF16,))
            parts, weights = [_attn_c(c_qkv, _bias_c(c_rpb[i]), seg_starts, seg_ends)], [wo]
        x, info, counts = _outproj_ln(parts, weights, xs, ln1_g[l], ln1_b[l], w_router, b_router)
        xs = _moe_layer(l, x, info, counts, w_router, b_router, w_gate, w_up, w_down, ln2_g[l], ln2_b[l],
                        io_rows if l == DEPTH - 1 else (n,))
    return xs


def kernel(x_prompt, x_sample, rel_bias, w_in_ab, a_sink, w_out_ab, w_in_c, c_rpb, w_out_c,
           ln1_g, ln1_b, ln2_g, ln2_b, router_g_w, router_g_b, router_e_w, router_e_b,
           w_gate, w_up, w_down):
    lens = [x_prompt.shape[1]] * x_prompt.shape[0] + [x_sample.shape[1]] * x_sample.shape[0]
    seg_ends = tuple(int(v) for v in np.cumsum(lens))
    seg_starts = tuple(e - n for e, n in zip(seg_ends, lens))
    for n in lens:
        assert n % max(A_TB, B_TB, C_TB) == 0 and n // GRID_W >= NA_ROWS
    xs = [x_prompt.reshape(-1, D_MODEL), x_sample.reshape(-1, D_MODEL)]
    y_p, y_s = _trunk(xs, seg_starts, seg_ends, rel_bias, w_in_ab, a_sink, w_out_ab, w_in_c, c_rpb, w_out_c,
                      ln1_g, ln1_b, ln2_g, ln2_b, router_g_w, router_g_b, router_e_w, router_e_b,
                      w_gate, w_up, w_down)
    return (y_p.reshape(x_prompt.shape), y_s.reshape(x_sample.shape))
```

```python
import functools
import math

import numpy as np
import jax
import jax.numpy as jnp
from jax import lax
from jax.experimental import pallas as pl
from jax.experimental.pallas import tpu as pltpu

F32 = jnp.float32
BF16 = jnp.bfloat16

D_MODEL = 1024
DEPTH = 4
HEAD_DIM = 64
LANES = 128
A_HEADS = 8
A_KV_HEADS = 2
A_WINDOW = 128
B_HEADS = 8
B_BRANCHES = ((128, 1), (512, 4), (2048, 16))
B_HALF = 64
C_HEADS = 16
GRID_W = 64
NA_ROWS = 8
NA_COLS = 16
REL_BUCKETS = 32
REL_MAX_DIST = 1024
N_GROUPS = 4
EXPERTS_PER_GROUP = 4
N_EXPERTS = 16
D_EXPERT = 512
N_PAIRS = 6
N_CLASSES = N_GROUPS * N_PAIRS
DEEPNORM_ALPHA = (2.0 * DEPTH) ** 0.25
LN_EPS = 1e-5
LOG2E = math.log2(math.e)
ATTN_SCALE = HEAD_DIM ** -0.5 * LOG2E
NEG_INF = -1e30

QA_W = A_HEADS * HEAD_DIM
KVA_W = A_KV_HEADS * HEAD_DIM
A_IN = QA_W + 2 * KVA_W
B_W = B_HEADS * HEAD_DIM
B_IN = 3 * B_W
C_W = C_HEADS * HEAD_DIM

ATT_TB = 1024
MM_TM = 512
MOE_TM = 256
ROW_TM = 1024
VMEM_LIMIT = 56 * 1024 * 1024


def _cparams(sem):
    return pltpu.CompilerParams(dimension_semantics=sem, vmem_limit_bytes=VMEM_LIMIT)


def _segment_flags(tok0, size, seg_starts, seg_ends):
    is_first = functools.reduce(jnp.logical_or, [tok0 == s for s in seg_starts])
    is_last = functools.reduce(jnp.logical_or, [tok0 + size == e for e in seg_ends])
    return is_first, is_last


def _t5_bucket_np(rel):
    half_b = REL_BUCKETS // 2
    max_exact = half_b // 2
    n = np.abs(rel)
    large = max_exact + (np.log(np.maximum(n, max_exact).astype(np.float32) / max_exact)
                         / math.log(REL_MAX_DIST / max_exact) * (half_b - max_exact)).astype(np.int32)
    large = np.minimum(large, half_b - 1)
    return np.where(rel > 0, half_b, 0) + np.where(n < max_exact, n, large)


def _banded_bias(table, half, dil, q_len=None):
    q_len = half if q_len is None else q_len
    rel = np.arange(q_len + 2 * half)[None, :] - half - np.arange(q_len)[:, None]
    bucket = jnp.asarray(_t5_bucket_np(rel * dil).astype(np.int32))
    hit = bucket[None] == jnp.arange(REL_BUCKETS, dtype=jnp.int32)[:, None, None]
    bias = jnp.sum(jnp.where(hit[:, None], table.astype(F32)[:, :, None, None], 0.0), axis=0)
    return jnp.where(jnp.asarray(np.abs(rel) <= half)[None], bias * LOG2E, NEG_INF)


def _bias_a(rel_bias):
    w = A_WINDOW
    ch = A_CHUNK
    b = _banded_bias(rel_bias[:, :A_HEADS], w, 1).reshape(A_HEADS, w // ch, ch, 3 * w)
    b = b.transpose(1, 0, 2, 3).reshape(w // ch, A_HEADS * ch, 3 * w)
    col = np.arange(3 * w)
    first = jnp.where(jnp.asarray(col < w), NEG_INF, b)
    last = jnp.where(jnp.asarray(col >= 2 * w), NEG_INF, b)
    return jnp.stack([b, first, last])


def _bias_b(rel_bias):
    out = []
    for (_, d), q in zip(B_BRANCHES, B_QBLK):
        b = _banded_bias(rel_bias[:, A_HEADS:], B_HALF, d, q)
        out.append(b.reshape(B_HEADS // 2, 2 * q, q + 2 * B_HALF))
    return out


def _bias_c(rpb):
    gw = GRID_W
    n_dr = 2 * NA_ROWS - 1
    side = gw - NA_COLS
    p = jnp.concatenate([jnp.repeat(rpb[..., :1], side, axis=-1), rpb.astype(F32),
                         jnp.repeat(rpb[..., -1:], side + 1, axis=-1)], axis=-1)
    z = jnp.broadcast_to(p[:, :, None, :], (C_HEADS, n_dr, gw, 2 * gw)).reshape(C_HEADS, n_dr, 2 * gw * gw)
    t = z[:, :, gw - 1:gw - 1 + gw * (2 * gw - 1)].reshape(C_HEADS, n_dr, gw, 2 * gw - 1)[..., :gw]
    cq = np.arange(gw)[:, None]
    w = np.arange(gw)[None, :]
    c0 = np.clip(cq - NA_COLS // 2, 0, gw - NA_COLS)
    t = jnp.where(jnp.asarray((w >= c0) & (w < c0 + NA_COLS)), t * LOG2E, NEG_INF)
    t = t.reshape(C_HEADS // 2, 2, n_dr, gw, gw).transpose(0, 1, 3, 2, 4)
    bias = jnp.stack([t[:, :, :, NA_ROWS - 1 - s:2 * NA_ROWS - 1 - s, :].reshape(C_HEADS // 2, 2, gw, NA_ROWS * gw)
                      for s in range(NA_ROWS)], axis=1)
    return bias.reshape(C_HEADS // 2, NA_ROWS, 2 * gw, NA_ROWS * gw)


def _segment_blocks(segs, tm):
    return tuple(a.shape[0] // tm for a in segs)


def _segment_specs(seg_blocks, tm, width):
    specs, start = [], 0
    for nb in seg_blocks:
        specs.append(pl.BlockSpec((tm, width), lambda i, *_, s=start, nb=nb: (jnp.clip(i - s, 0, nb - 1), 0)))
        start += nb
    return specs


def _segment_rows(i, refs, seg_blocks):
    x, start = refs[0][...], seg_blocks[0]
    for ref, nb in zip(refs[1:], seg_blocks[1:]):
        x = jnp.where(i >= start, ref[...], x)
        start += nb
    return x


def _inproj_kernel(*refs, splits, seg_blocks):
    n_seg = len(seg_blocks)
    w_ref = refs[n_seg]
    o_refs = refs[n_seg + 1:]
    x = _segment_rows(pl.program_id(0), refs[:n_seg], seg_blocks).astype(BF16)
    for o_ref, (lo, hi) in zip(o_refs, splits):
        o_ref[...] = jnp.dot(x, w_ref[:, lo:hi], preferred_element_type=F32).astype(o_ref.dtype)


IN_TM = 2 * MM_TM


def _inproj(xs, w, splits, dtypes):
    tm = IN_TM
    seg_blocks = _segment_blocks(xs, tm)
    n = sum(seg_blocks) * tm
    w_spec = pl.BlockSpec(w.shape, lambda i: (0, 0), pipeline_mode=pl.Buffered(1))
    return pl.pallas_call(
        functools.partial(_inproj_kernel, splits=splits, seg_blocks=seg_blocks),
        grid=(n // tm,),
        in_specs=_segment_specs(seg_blocks, tm, D_MODEL) + [w_spec],
        out_specs=[pl.BlockSpec((tm, hi - lo), lambda i: (i, 0)) for lo, hi in splits],
        out_shape=[jax.ShapeDtypeStruct((n, hi - lo), dt) for (lo, hi), dt in zip(splits, dtypes)],
        compiler_params=_cparams(("parallel",)),
        name="inproj",
    )(*xs, w)


def _layer_norm(z, g, b):
    mu = jnp.mean(z, axis=-1, keepdims=True)
    zc = z - mu
    var = jnp.mean(zc * zc, axis=-1, keepdims=True)
    return zc * lax.rsqrt(var + LN_EPS) * g + b


def _outproj_ln_kernel(*refs, n_parts, seg_blocks):
    n_seg = len(seg_blocks)
    o_refs = refs[:n_parts]
    w_refs = refs[n_parts:2 * n_parts]
    x_refs = refs[2 * n_parts:2 * n_parts + n_seg]
    g_ref, b_ref, whl_ref, br_ref, out_ref, info_ref, cnt_ref, run_scr, tri_scr = refs[2 * n_parts + n_seg:]
    i = pl.program_id(0)
    h = DEEPNORM_ALPHA * _segment_rows(i, x_refs, seg_blocks)
    for o_ref, w_ref in zip(o_refs, w_refs):
        h = h + jnp.dot(o_ref[...], w_ref[...], preferred_element_type=F32)
    out = _layer_norm(h, g_ref[...], b_ref[...])
    out_ref[...] = out
    _route(i, out, whl_ref, br_ref, info_ref, cnt_ref, run_scr, tri_scr)


def _outproj_ln(parts, weights, xs, g, b, w_router, b_router):
    tm = MM_TM
    seg_blocks = _segment_blocks(xs, tm)
    n = sum(seg_blocks) * tm
    n_parts = len(parts)
    wh = w_router.astype(BF16)
    whl = jnp.concatenate([wh, (w_router - wh.astype(F32)).astype(BF16)], axis=1)
    const = lambda shape: pl.BlockSpec(shape, lambda i: (0, 0))
    return pl.pallas_call(
        functools.partial(_outproj_ln_kernel, n_parts=n_parts, seg_blocks=seg_blocks),
        grid=(n // tm,),
        in_specs=([pl.BlockSpec((tm, p.shape[1]), lambda i: (i, 0)) for p in parts]
                  + [const(w.shape) for w in weights]
                  + _segment_specs(seg_blocks, tm, D_MODEL)
                  + [const((1, D_MODEL)), const((1, D_MODEL)),
                     const((D_MODEL, 2 * LANES)), const((1, LANES))]),
        out_specs=[pl.BlockSpec((tm, D_MODEL), lambda i: (i, 0)),
                   pl.BlockSpec((None, SUBLANES, tm), lambda i: (i, 0, 0)),
                   const((CLASS_ROWS, LANES))],
        out_shape=[jax.ShapeDtypeStruct((n, D_MODEL), F32),
                   jax.ShapeDtypeStruct((n // tm, SUBLANES, tm), F32),
                   jax.ShapeDtypeStruct((CLASS_ROWS, LANES), F32)],
        scratch_shapes=[pltpu.VMEM((CLASS_ROWS, LANES), F32), pltpu.VMEM((tm, tm), BF16)],
        compiler_params=_cparams(("arbitrary",)),
        name="outproj_ln",
    )(*parts, *weights, *xs, g.reshape(1, D_MODEL), b.reshape(1, D_MODEL), whl, b_router)


ATT_DEPTH = 3
B_DEPTH = 10


def _staged(n, weights, values, depth=ATT_DEPTH):
    for i in range(min(depth, n)):
        weights(i)
    for i in range(n):
        if i + depth < n:
            weights(i + depth)
        values(i)


def _attn_a_kernel(q_ref, kvm_ref, kvp_ref, kvn_ref, bias_ref, sink_ref, o_ref, kv_scr, p_scr,
                   *, seg_starts, seg_ends):
    w = A_WINDOW
    tb = A_TB
    n_sub = tb // w
    tok0 = pl.program_id(0) * tb
    is_first, is_last = _segment_flags(tok0, tb, seg_starts, seg_ends)
    kv_scr[0:w, :] = kvp_ref[...]
    kv_scr[w:w + tb, :] = kvm_ref[...]
    kv_scr[w + tb:, :] = kvn_ref[...]
    low = lax.broadcasted_iota(jnp.int32, (1, LANES), 1) < HEAD_DIM
    ch = A_CHUNK
    per = w // ch

    sinks = [jnp.concatenate([jnp.full((ch, 1), sink_ref[c + 4 * g], F32) for c in range(4)], axis=0)
             for g in range(A_KV_HEADS)]

    sink = jnp.concatenate(sinks, axis=0)

    def scores(t):
        j = t // per
        q = q_ref[t * ch:(t + 1) * ch, :]
        qg = jnp.concatenate([jnp.where(low, q[:, c * LANES:(c + 1) * LANES], 0) for c in range(4)]
                             + [jnp.where(low, 0, q[:, c * LANES:(c + 1) * LANES]) for c in range(4)], axis=0)
        k2 = kv_scr[j * w:(j + 3) * w, :LANES]
        return lax.dot_general(qg, k2, (((1,), (1,)), ((), ())), preferred_element_type=F32)

    rows = A_HEADS * ch
    rdens = {}

    def weights(t):
        j = t // per
        if j == 0:
            variant = jnp.where(is_first, 1, 0)
        elif j == n_sub - 1:
            variant = jnp.where(is_last, 2, 0)
        else:
            variant = 0
        s = scores(t) + bias_ref[variant, t % per]
        m = jnp.maximum(jnp.max(s, axis=-1, keepdims=True), sink)
        e = jnp.exp2(s - m)
        rdens[t] = 1.0 / (jnp.sum(e, axis=-1, keepdims=True) + jnp.exp2(sink - m))
        p_scr[t * rows:(t + 1) * rows, :] = e.astype(BF16)

    def values(t):
        j = t // per
        v2 = kv_scr[j * w:(j + 3) * w, LANES:]
        pv = jnp.dot(p_scr[t * rows:(t + 1) * rows, :], v2, preferred_element_type=F32) * rdens[t]
        for c in range(4):
            oc = jnp.where(low, pv[c * ch:(c + 1) * ch], pv[(4 + c) * ch:(5 + c) * ch])
            o_ref[t * ch:(t + 1) * ch, c * LANES:(c + 1) * LANES] = oc.astype(o_ref.dtype)

    _staged(tb // ch, weights, values)


A_CHUNK = 128
A_TB = 2 * ATT_TB


def _attn_a(a_qkv, bias, sink, seg_starts, seg_ends):
    n = a_qkv.shape[0]
    w = A_WINDOW
    tb = A_TB
    sub = tb // w
    nhb = n // w
    kv_col = QA_W // (2 * LANES)
    return pl.pallas_call(
        functools.partial(_attn_a_kernel, seg_starts=seg_starts, seg_ends=seg_ends),
        grid=(n // tb,),
        in_specs=[pl.BlockSpec((tb, QA_W), lambda i: (i, 0)),
                  pl.BlockSpec((tb, 2 * LANES), lambda i: (i, kv_col)),
                  pl.BlockSpec((w, 2 * LANES), lambda i: (jnp.maximum(i * sub - 1, 0), kv_col)),
                  pl.BlockSpec((w, 2 * LANES), lambda i: (jnp.minimum((i + 1) * sub, nhb - 1), kv_col)),
                  pl.BlockSpec(bias.shape, lambda i: (0,) * bias.ndim),
                  pl.BlockSpec(memory_space=pltpu.SMEM)],
        out_specs=pl.BlockSpec((tb, QA_W), lambda i: (i, 0)),
        out_shape=jax.ShapeDtypeStruct((n, QA_W), BF16),
        scratch_shapes=[pltpu.VMEM((tb + 2 * w, 2 * LANES), BF16),
                        pltpu.VMEM((A_HEADS * tb, 3 * w), BF16)],
        compiler_params=_cparams(("parallel",)),
        name="attn_a",
    )(a_qkv, a_qkv, a_qkv, a_qkv, bias, sink)


def _attn_b_kernel(q_ref, kp_ref, km_ref, kn_ref, vp_ref, vm_ref, vn_ref, bias0_ref, bias1_ref, bias2_ref, o_ref,
                   k_scr, v_scr, o_scr, m_scr, l_scr, p0_scr, p1_scr, p2_scr, *, seg_starts, seg_ends):
    tb = B_TB
    h = B_HALF
    tok0 = pl.program_id(1) * tb
    is_first, is_last = _segment_flags(tok0, tb, seg_starts, seg_ends)
    k_scr[0:tb, :] = kp_ref[...]
    k_scr[tb:2 * tb, :] = km_ref[...]
    k_scr[2 * tb:, :] = kn_ref[...]
    v_scr[0:tb, :] = vp_ref[...]
    v_scr[tb:2 * tb, :] = vm_ref[...]
    v_scr[2 * tb:, :] = vn_ref[...]
    lane = lax.broadcasted_iota(jnp.int32, (1, LANES), 1)
    low = lane < HEAD_DIM
    bias_refs = (bias0_ref, bias1_ref, bias2_ref)
    p_scrs = (p0_scr, p1_scr, p2_scr)

    pens = []
    for q in B_QBLK:
        col = lax.broadcasted_iota(jnp.int32, (1, q + 2 * h), 1)
        pens.append((jnp.where(jnp.logical_and(col < h, is_first), NEG_INF, 0.0),
                     jnp.where(jnp.logical_and(col >= q + h, is_last), NEG_INF, 0.0)))

    def slices(br, d, r, b):
        q = B_QBLK[br]
        row0 = r + q * d * b
        if d == 1:
            return pl.ds(row0, q), pl.ds(tb + row0 - h, q + 2 * h)
        return pl.ds(row0, q, stride=d), pl.ds(tb + row0 - h * d, q + 2 * h, stride=d)

    def scores(br, d, r, b):
        qs, ks = slices(br, d, r, b)
        q = q_ref[qs, :].astype(BF16)
        k = k_scr[ks, :].astype(BF16)
        qq = jnp.concatenate([jnp.where(low, q, 0), jnp.where(low, 0, q)], axis=0)
        return lax.dot_general(qq, k, (((1,), (1,)), ((), ())), preferred_element_type=F32)

    tiles = [(br, d, r, b, r * (tb // (B_QBLK[br] * d)) + b) for br, (_, d) in reversed(list(enumerate(B_BRANCHES)))
             for r in range(d) for b in range(tb // (B_QBLK[br] * d))]

    def weights(i):
        br, d, r, b, t = tiles[i]
        q = B_QBLK[br]
        qs, _ = slices(br, d, r, b)
        s = scores(br, d, r, b) + bias_refs[br][...]
        if b == 0:
            s = s + pens[br][0]
        if b == tb // (q * d) - 1:
            s = s + pens[br][1]
        m = jnp.max(s, axis=-1, keepdims=True)
        e = jnp.exp2(s - m)
        l = jnp.sum(e, axis=-1, keepdims=True)
        p_scrs[br][t * 2 * q:(t + 1) * 2 * q, :] = e.astype(BF16)
        m_scr[br, qs, :] = jnp.where(low, m[:q], m[q:])
        l_scr[br, qs, :] = jnp.where(low, l[:q], l[q:])

    def values(i):
        br, d, r, b, t = tiles[i]
        q = B_QBLK[br]
        qs, ks = slices(br, d, r, b)
        v = v_scr[ks, :].astype(BF16)
        pv = jnp.dot(p_scrs[br][t * 2 * q:(t + 1) * 2 * q, :], v, preferred_element_type=F32)
        o_scr[br, qs, :] = jnp.where(low, pv[:q], pv[q:])

    _staged(len(tiles), weights, values, B_DEPTH)

    m_all = jnp.maximum(jnp.maximum(m_scr[0], m_scr[1]), m_scr[2])
    num = jnp.zeros((tb, LANES), F32)
    den = jnp.zeros((tb, LANES), F32)
    for br in range(len(B_BRANCHES)):
        a = jnp.exp2(m_scr[br] - m_all)
        num = num + a * o_scr[br]
        den = den + a * l_scr[br]
    o_ref[...] = (num / den).astype(o_ref.dtype)


B_TB = ATT_TB
B_QBLK = tuple(B_HALF for _ in B_BRANCHES)


def _attn_b(b_qkv, biases, seg_starts, seg_ends):
    n = b_qkv.shape[0]
    tb = B_TB
    nblk = n // tb
    npair = B_HEADS // 2
    prev = lambda i: jnp.maximum(i - 1, 0)
    nxt = lambda i: jnp.minimum(i + 1, nblk - 1)
    blk = lambda rowf, off: pl.BlockSpec((tb, LANES), lambda c, i: (rowf(i), off + c))
    same = lambda i: i
    stat = pltpu.VMEM((len(B_BRANCHES), tb, LANES), F32)
    return pl.pallas_call(
        functools.partial(_attn_b_kernel, seg_starts=seg_starts, seg_ends=seg_ends),
        grid=(npair, nblk),
        in_specs=[blk(same, 0),
                  blk(prev, npair), blk(same, npair), blk(nxt, npair),
                  blk(prev, 2 * npair), blk(same, 2 * npair), blk(nxt, 2 * npair)]
                 + [pl.BlockSpec((None,) + b.shape[1:], lambda c, i: (c, 0, 0)) for b in biases],
        out_specs=pl.BlockSpec((tb, LANES), lambda c, i: (i, c)),
        out_shape=jax.ShapeDtypeStruct((n, B_W), BF16),
        scratch_shapes=[pltpu.VMEM((3 * tb, LANES), F32), pltpu.VMEM((3 * tb, LANES), F32),
                        stat, stat, stat]
                       + [pltpu.VMEM((2 * tb, q + 2 * B_HALF), BF16) for q in B_QBLK],
        compiler_params=_cparams(("parallel", "parallel")),
        name="attn_b",
    )(b_qkv, b_qkv, b_qkv, b_qkv, b_qkv, b_qkv, b_qkv, *biases)


C_HALO = (NA_ROWS // 2) * GRID_W

def _attn_c_kernel(q_ref, kp_ref, km_ref, kn_ref, vp_ref, vm_ref, vn_ref, bias_ref, o_ref,
                   k_scr, v_scr, p_scr, *, seg_starts, seg_ends):
    tb = C_TB
    gw = GRID_W
    nkeys = NA_ROWS * gw
    tok0 = pl.program_id(1) * tb
    k_scr[0:C_HALO, :] = kp_ref[...]
    k_scr[C_HALO:C_HALO + tb, :] = km_ref[...]
    k_scr[C_HALO + tb:, :] = kn_ref[...]
    v_scr[0:C_HALO, :] = vp_ref[...]
    v_scr[C_HALO:C_HALO + tb, :] = vm_ref[...]
    v_scr[C_HALO + tb:, :] = vn_ref[...]
    seg_row0 = jnp.int32(0)
    seg_rows = jnp.int32(0)
    for s, e in zip(seg_starts, seg_ends):
        inside = jnp.logical_and(tok0 >= s, tok0 < e)
        seg_row0 = jnp.where(inside, s // gw, seg_row0)
        seg_rows = jnp.where(inside, (e - s) // gw, seg_rows)
    lane = lax.broadcasted_iota(jnp.int32, (1, LANES), 1)
    low = lane < HEAD_DIM

    def window(rr):
        rs = tok0 // gw + rr - seg_row0
        start = jnp.clip(rs - NA_ROWS // 2, 0, seg_rows - NA_ROWS)
        shift = rs - start
        return shift, pl.ds(pl.multiple_of((rr + NA_ROWS // 2 - shift) * gw, gw), nkeys)

    def scores(rr):
        _, ks = window(rr)
        q = q_ref[rr * gw:(rr + 1) * gw, :]
        qq = jnp.concatenate([jnp.where(low, q, 0), jnp.where(low, 0, q)], axis=0)
        return lax.dot_general(qq, k_scr[ks, :], (((1,), (1,)), ((), ())), preferred_element_type=F32)

    n_rows = tb // gw
    rdens = {}

    def weights(rr):
        shift, _ = window(rr)
        s = scores(rr) + bias_ref[shift]
        m = jnp.max(s, axis=-1, keepdims=True)
        e = jnp.exp2(s - m)
        rdens[rr] = 1.0 / jnp.sum(e, axis=-1, keepdims=True)
        p_scr[rr * 2 * gw:(rr + 1) * 2 * gw, :] = e.astype(BF16)

    def values(rr):
        _, ks = window(rr)
        pv = jnp.dot(p_scr[rr * 2 * gw:(rr + 1) * 2 * gw, :], v_scr[ks, :], preferred_element_type=F32) * rdens[rr]
        o_ref[rr * gw:(rr + 1) * gw, :] = jnp.where(low, pv[:gw], pv[gw:]).astype(o_ref.dtype)

    _staged(n_rows, weights, values)


C_TB = 2 * ATT_TB


def _attn_c(c_qkv, bias, seg_starts, seg_ends):
    n = c_qkv.shape[0]
    tb = C_TB
    npair = C_HEADS // 2
    sub = tb // C_HALO
    nhb = n // C_HALO
    main = lambda off: pl.BlockSpec((tb, LANES), lambda c, i: (i, off + c))
    prev = lambda off: pl.BlockSpec((C_HALO, LANES), lambda c, i: (jnp.maximum(i * sub - 1, 0), off + c))
    nxt = lambda off: pl.BlockSpec((C_HALO, LANES),
                                   lambda c, i: (jnp.minimum((i + 1) * sub, nhb - 1), off + c))
    return pl.pallas_call(
        functools.partial(_attn_c_kernel, seg_starts=seg_starts, seg_ends=seg_ends),
        grid=(npair, n // tb),
        in_specs=[main(0),
                  prev(npair), main(npair), nxt(npair),
                  prev(2 * npair), main(2 * npair), nxt(2 * npair),
                  pl.BlockSpec((None,) + bias.shape[1:], lambda c, i: (c, 0, 0, 0))],
        out_specs=pl.BlockSpec((tb, LANES), lambda c, i: (i, c)),
        out_shape=jax.ShapeDtypeStruct((n, C_W), BF16),
        scratch_shapes=[pltpu.VMEM((tb + 2 * C_HALO, LANES), BF16),
                        pltpu.VMEM((tb + 2 * C_HALO, LANES), BF16),
                        pltpu.VMEM((2 * tb, NA_ROWS * GRID_W), BF16)],
        compiler_params=_cparams(("parallel", "parallel")),
        name="attn_c",
    )(c_qkv, c_qkv, c_qkv, c_qkv, c_qkv, c_qkv, c_qkv, bias)


def _route(i, x, whl_ref, b_ref, info_ref, cnt_ref, run_scr, tri_scr):
    tm = x.shape[0]

    @pl.when(i == 0)
    def _():
        run_scr[...] = jnp.zeros_like(run_scr)

    xh = x.astype(BF16)
    xl = (x - xh.astype(F32)).astype(BF16)
    hh_hl = jnp.dot(xh, whl_ref[...], preferred_element_type=F32)
    logits = (hh_hl[:, :LANES] + jnp.dot(xl, whl_ref[:, :LANES], preferred_element_type=F32)
              + hh_hl[:, LANES:]) + b_ref[...]
    lt = logits.T
    first = lambda hit, n: jnp.min(jnp.where(hit, lax.broadcasted_iota(jnp.int32, (n, tm), 0).astype(F32),
                                             float(n)), axis=0, keepdims=True)
    lg = lt[0:N_GROUPS]
    g_sel = first(lg == jnp.max(lg, axis=0, keepdims=True), N_GROUPS)
    le = jnp.zeros((EXPERTS_PER_GROUP, tm), F32)
    for g in range(N_GROUPS):
        lo = N_GROUPS + g * EXPERTS_PER_GROUP
        le = jnp.where(g_sel == g, lt[lo:lo + EXPERTS_PER_GROUP], le)
    row = lax.broadcasted_iota(jnp.int32, (EXPERTS_PER_GROUP, tm), 0).astype(F32)
    i1 = first(le == jnp.max(le, axis=0, keepdims=True), EXPERTS_PER_GROUP)
    rest = jnp.where(row == i1, NEG_INF, le)
    i2 = first(jnp.logical_and(rest == jnp.max(rest, axis=0, keepdims=True), row != i1), EXPERTS_PER_GROUP)
    a = jnp.minimum(i1, i2)
    b = jnp.maximum(i1, i2)
    cls = g_sel * N_PAIRS + a * 3.0 - jnp.where(a == 2.0, 1.0, 0.0) + (b - a - 1.0)

    @pl.when(i == 0)
    def _():
        tri_scr[...] = (lax.broadcasted_iota(jnp.int32, (tm, tm), 0)
                        < lax.broadcasted_iota(jnp.int32, (tm, tm), 1)).astype(BF16)

    onehot = lax.broadcasted_iota(jnp.int32, (CLASS_ROWS, tm), 0).astype(F32) == cls
    before = jnp.dot(onehot.astype(BF16), tri_scr[...], preferred_element_type=F32) + run_scr[:, 0:1]
    rank = jnp.sum(jnp.where(onehot, before, 0.0), axis=0, keepdims=True)
    run_scr[...] = run_scr[...] + jnp.sum(onehot.astype(F32), axis=1, keepdims=True)
    srow = lax.broadcasted_iota(jnp.int32, (SUBLANES, tm), 0)
    info_ref[...] = jnp.where(srow == 0, cls, jnp.where(srow == 1, rank, 0.0))
    cnt_ref[...] = run_scr[...]


CLASS_ROWS = 32
SUBLANES = 8


def _tile_copy(src, src_row, dst, dst_row, sem):
    return pltpu.make_async_copy(src.at[pl.ds(pl.multiple_of(src_row, SUBLANES), SUBLANES), :],
                                 dst.at[pl.ds(pl.multiple_of(dst_row, SUBLANES), SUBLANES), :], sem)


ROW_UNROLL = 8


def _start_rows(copy, n):
    def body(g, carry):
        for u in range(ROW_UNROLL):
            copy(g * ROW_UNROLL + u).start(priority=u % 2)
        return carry

    lax.fori_loop(0, n // ROW_UNROLL, body, 0)


def _dispatch_kernel(dest_ref, pad_ref, x_ref, xs_ref, rec_scr, zero_scr, sems, zsem):
    i = pl.program_id(0)
    last = pl.num_programs(0) - 1
    tm = x_ref.shape[0]
    rows = tm * SUBLANES
    slot = i % 2
    tile_rows = MOE_TM * SUBLANES

    def zero_copy(c):
        start = pl.multiple_of(pad_ref[c] * SUBLANES, tile_rows)
        return pltpu.make_async_copy(zero_scr, xs_ref.at[pl.ds(start, tile_rows), :], zsem)

    @pl.when(i == 0)
    def _():
        zero_scr[...] = jnp.zeros_like(zero_scr)
        for c in range(pad_ref.shape[0]):
            @pl.when(pad_ref[c] >= 0)
            def _():
                zero_copy(c).start()
        for c in range(pad_ref.shape[0]):
            @pl.when(pad_ref[c] >= 0)
            def _():
                zero_copy(c).wait()

    base = i * tm
    for s in range(2):
        @pl.when(slot == s)
        def _(s=s):
            for j in range(D_MODEL // LANES):
                rec_scr[s, pl.ds(j, tm, stride=SUBLANES), :] = x_ref[:, j * LANES:(j + 1) * LANES]
            _start_rows(lambda r: _tile_copy(rec_scr.at[s], r * SUBLANES, xs_ref, dest_ref[base + r], sems.at[s]),
                        tm)

    def wait_step(s):
        pltpu.make_async_copy(rec_scr.at[s], xs_ref.at[pl.ds(0, rows), :], sems.at[s]).wait()

    @pl.when(i > 0)
    def _():
        wait_step(1 - slot)

    @pl.when(i == last)
    def _():
        wait_step(slot)


def _dispatch(dest, pad_start, x, n_sorted):
    n = x.shape[0]
    tm = ROW_TM
    return pl.pallas_call(
        _dispatch_kernel,
        grid_spec=pltpu.PrefetchScalarGridSpec(
            num_scalar_prefetch=2,
            grid=(n // tm,),
            in_specs=[pl.BlockSpec((tm, D_MODEL), lambda i, d, p: (i, 0))],
            out_specs=pl.BlockSpec(memory_space=pl.ANY),
            scratch_shapes=[pltpu.VMEM((2, tm * SUBLANES, LANES), F32),
                            pltpu.VMEM((MOE_TM * SUBLANES, LANES), F32),
                            pltpu.SemaphoreType.DMA((2,)), pltpu.SemaphoreType.DMA]),
        out_shape=jax.ShapeDtypeStruct((n_sorted * SUBLANES, LANES), F32),
        compiler_params=_cparams(("arbitrary",)),
        name="dispatch",
    )(dest, pad_start, x)


def _expert_kernel(ea_ref, eb_ref, nt_ref, xs_ref, wr_ref, br_ref,
                   wga_ref, wua_ref, wda_ref, wgb_ref, wub_ref, wdb_ref, g_ref, b_ref, ys_ref, hid_scr):
    p = pl.program_id(0)
    tm = MOE_TM

    @pl.when(p >= nt_ref[0])
    def _():
        ys_ref[...] = jnp.zeros_like(ys_ref)

    @pl.when(p < nt_ref[0])
    def _():
        x = jnp.concatenate([xs_ref[pl.ds(j, tm, stride=SUBLANES), :] for j in range(D_MODEL // LANES)],
                            axis=1)
        xb = x.astype(BF16)

        logits = jnp.dot(xb, wr_ref[...], preferred_element_type=F32) + br_ref[...]
        lane = lax.broadcasted_iota(jnp.int32, (tm, LANES), 1)
        lane_a = N_GROUPS + ea_ref[p]
        lane_b = N_GROUPS + eb_ref[p]
        grp = ea_ref[p] // EXPERTS_PER_GROUP
        is_g = lane < N_GROUPS
        mg = jnp.max(jnp.where(is_g, logits, NEG_INF), axis=-1, keepdims=True)
        eg = jnp.where(is_g, jnp.exp(logits - mg), 0.0)
        g_gate = (jnp.sum(jnp.where(lane == grp, eg, 0.0), axis=-1, keepdims=True)
                  / jnp.sum(eg, axis=-1, keepdims=True))
        l_a = jnp.sum(jnp.where(lane == lane_a, logits, 0.0), axis=-1, keepdims=True)
        l_b = jnp.sum(jnp.where(lane == lane_b, logits, 0.0), axis=-1, keepdims=True)
        mx = jnp.maximum(l_a, l_b)
        p_a = jnp.exp(l_a - mx)
        p_b = jnp.exp(l_b - mx)
        scale = g_gate / (p_a + p_b)
        w_a = p_a * scale
        w_b = p_b * scale

        for e, (wg_ref, wu_ref) in enumerate(((wga_ref, wua_ref), (wgb_ref, wub_ref))):
            gate = jnp.dot(xb, wg_ref[...], preferred_element_type=F32)
            up = jnp.dot(xb, wu_ref[...], preferred_element_type=F32)
            hid_scr[e] = (gate * (1.0 / (1.0 + jnp.exp(-gate))) * up).astype(BF16)
        y = (w_a * jnp.dot(hid_scr[0], wda_ref[...], preferred_element_type=F32)
             + w_b * jnp.dot(hid_scr[1], wdb_ref[...], preferred_element_type=F32))
        out = _layer_norm(DEEPNORM_ALPHA * x + y, g_ref[...], b_ref[...])
        for j in range(D_MODEL // LANES):
            ys_ref[pl.ds(j, tm, stride=SUBLANES), :] = out[:, j * LANES:(j + 1) * LANES]


def _experts(layer, tile_ea, tile_eb, n_tiles, xs, w_router, b_router, w_gate, w_up, w_down, g, b):
    tm = MOE_TM
    rows = tm * SUBLANES
    n_grid = xs.shape[0] // rows
    last = lambda p, nt: jnp.maximum(jnp.minimum(p, nt[0] - 1), 0)
    wspec_a = lambda shape: pl.BlockSpec((None, None) + shape, lambda p, ea, eb, nt: (layer, ea[p], 0, 0))
    wspec_b = lambda shape: pl.BlockSpec((None, None) + shape, lambda p, ea, eb, nt: (layer, eb[p], 0, 0))
    const = lambda shape: pl.BlockSpec(shape, lambda p, ea, eb, nt: (0, 0))
    up_shape = (D_MODEL, D_EXPERT)
    dn_shape = (D_EXPERT, D_MODEL)
    return pl.pallas_call(
        _expert_kernel,
        grid_spec=pltpu.PrefetchScalarGridSpec(
            num_scalar_prefetch=3,
            grid=(n_grid,),
            in_specs=[pl.BlockSpec((rows, LANES), lambda p, ea, eb, nt: (last(p, nt), 0)),
                      const((D_MODEL, LANES)), const((1, LANES)),
                      wspec_a(up_shape), wspec_a(up_shape), wspec_a(dn_shape),
                      wspec_b(up_shape), wspec_b(up_shape), wspec_b(dn_shape),
                      const((1, D_MODEL)), const((1, D_MODEL))],
            out_specs=pl.BlockSpec((rows, LANES), lambda p, ea, eb, nt: (p, 0)),
            scratch_shapes=[pltpu.VMEM((2, tm, D_EXPERT), BF16)]),
        out_shape=jax.ShapeDtypeStruct(xs.shape, F32),
        compiler_params=_cparams(("arbitrary",)),
        name="experts",
    )(tile_ea, tile_eb, n_tiles, xs, w_router.astype(BF16), b_router,
      w_gate, w_up, w_down, w_gate, w_up, w_down, g.reshape(1, D_MODEL), b.reshape(1, D_MODEL))


def _gather_kernel(dest_ref, ys_ref, *refs, seg_blocks):
    out_refs = refs[:len(seg_blocks)]
    rec_scr, sems = refs[len(seg_blocks):]
    i = pl.program_id(0)
    n_steps = pl.num_programs(0)
    tm = out_refs[0].shape[0]
    rows = tm * SUBLANES
    slot = i % 2

    def fetch(s, buf):
        base = s * tm
        _start_rows(lambda r: _tile_copy(ys_ref, dest_ref[base + r], rec_scr.at[buf], r * SUBLANES, sems.at[buf]),
                    tm)

    @pl.when(i == 0)
    def _():
        fetch(0, 0)

    for buf in range(2):
        @pl.when(jnp.logical_and(i + 1 < n_steps, (i + 1) % 2 == buf))
        def _(buf=buf):
            fetch(i + 1, buf)

    pltpu.make_async_copy(ys_ref.at[pl.ds(0, rows), :], rec_scr.at[slot], sems.at[slot]).wait()

    start = 0
    for out_ref, nb in zip(out_refs, seg_blocks):
        @pl.when(jnp.logical_and(i >= start, i < start + nb))
        def _(out_ref=out_ref):
            for j in range(D_MODEL // LANES):
                out_ref[:, j * LANES:(j + 1) * LANES] = rec_scr[slot, pl.ds(j, tm, stride=SUBLANES), :]
        start += nb


def _gather_rows(dest, ys, seg_rows):
    tm = ROW_TM
    seg_blocks = tuple(r // tm for r in seg_rows)
    return pl.pallas_call(
        functools.partial(_gather_kernel, seg_blocks=seg_blocks),
        grid_spec=pltpu.PrefetchScalarGridSpec(
            num_scalar_prefetch=1,
            grid=(sum(seg_blocks),),
            in_specs=[pl.BlockSpec(memory_space=pl.ANY)],
            out_specs=_segment_specs(seg_blocks, tm, D_MODEL),
            scratch_shapes=[pltpu.VMEM((2, tm * SUBLANES, LANES), F32), pltpu.SemaphoreType.DMA((2,))]),
        out_shape=[jax.ShapeDtypeStruct((r, D_MODEL), F32) for r in seg_rows],
        compiler_params=_cparams(("arbitrary",)),
        name="gather_rows",
    )(dest, ys)


_PAIR_A = np.array([0, 0, 0, 1, 1, 2], np.int32)
_PAIR_B = np.array([1, 2, 3, 2, 3, 3], np.int32)


def _moe_layer(layer, x, info, counts, w_router, b_router, w_gate, w_up, w_down, g, b, out_rows):
    n = x.shape[0]
    tm = MOE_TM
    n_sorted = n + N_CLASSES * tm
    cls = info[:, 0, :].reshape(n).astype(jnp.int32)
    rank = info[:, 1, :].reshape(n).astype(jnp.int32)
    counts = counts[:N_CLASSES, 0].astype(jnp.int32)
    padded = (counts + tm - 1) // tm * tm
    classes = jnp.arange(N_CLASSES, dtype=jnp.int32)
    ends = jnp.sum(jnp.where(classes[None, :] <= classes[:, None], padded[None, :], 0), axis=1)
    offs = ends - padded
    total = ends[N_CLASSES - 1]
    dest = rank + jnp.sum(jnp.where(cls[:, None] == classes[None, :], offs[None, :], 0), axis=1)
    unused = total + classes * tm
    pad_start = jnp.concatenate([jnp.where(padded > 0, ends - tm, -1),
                                 jnp.where(unused < n_sorted, unused, -1)]).astype(jnp.int32)
    tile_start = jnp.arange(n_sorted // tm, dtype=jnp.int32) * tm
    tile_start = jnp.minimum(tile_start, total - tm)
    tile_cls = jnp.sum((ends[None, :] <= tile_start[:, None]).astype(jnp.int32), axis=1)
    pair = tile_cls % N_PAIRS
    pair_a = jnp.sum(jnp.where(pair[:, None] == np.arange(N_PAIRS)[None, :], _PAIR_A[None, :], 0), axis=1)
    pair_b = jnp.sum(jnp.where(pair[:, None] == np.arange(N_PAIRS)[None, :], _PAIR_B[None, :], 0), axis=1)
    grp = tile_cls // N_PAIRS
    tile_ea = (grp * EXPERTS_PER_GROUP + pair_a).astype(jnp.int32)
    tile_eb = (grp * EXPERTS_PER_GROUP + pair_b).astype(jnp.int32)
    n_tiles = (total // tm).astype(jnp.int32).reshape(1)
    dest_row = (dest * SUBLANES).astype(jnp.int32)
    xs = _dispatch(dest_row, pad_start, x, n_sorted)
    ys = _experts(layer, tile_ea, tile_eb, n_tiles, xs, w_router, b_router, w_gate, w_up, w_down, g, b)
    return _gather_rows(dest_row, ys, out_rows)


_A_ORDER = np.array([0, 4, 1, 5, 2, 6, 3, 7])


def _prep_ab(w_in, w_out):
    qa = w_in[:, :QA_W].reshape(D_MODEL, A_HEADS, HEAD_DIM)[:, _A_ORDER].reshape(D_MODEL, QA_W) * ATTN_SCALE
    kva = w_in[:, QA_W:A_IN]
    qb = w_in[:, A_IN:A_IN + B_W] * ATTN_SCALE
    kvb = w_in[:, A_IN + B_W:]
    w = jnp.concatenate([qa, kva, qb, kvb], axis=1).astype(BF16)
    wo_a = w_out[:QA_W].reshape(A_HEADS, HEAD_DIM, D_MODEL)[_A_ORDER].reshape(QA_W, D_MODEL).astype(BF16)
    wo_b = w_out[QA_W:].astype(BF16)
    return w, wo_a, wo_b


def _prep_c(w_in, w_out):
    w = jnp.concatenate([w_in[:, :C_W] * ATTN_SCALE, w_in[:, C_W:]], axis=1).astype(BF16)
    return w, w_out.astype(BF16)


def _trunk(xs, seg_starts, seg_ends, rel_bias, w_in_ab, a_sink, w_out_ab, w_in_c, c_rpb, w_out_c,
           ln1_g, ln1_b, ln2_g, ln2_b, router_g_w, router_g_b, router_e_w, router_e_b,
           w_gate, w_up, w_down):
    io_rows = tuple(a.shape[0] for a in xs)
    n = sum(io_rows)
    bias_a = _bias_a(rel_bias)
    bias_b = _bias_b(rel_bias)
    w_gate = w_gate.astype(BF16)
    w_up = w_up.astype(BF16)
    w_down = w_down.astype(BF16)
    for l in range(DEPTH):
        i = l // 2
        pad = LANES - N_GROUPS - N_EXPERTS
        w_router = jnp.pad(jnp.concatenate([router_g_w[l], router_e_w[l]], axis=1), ((0, 0), (0, pad)))
        b_router = jnp.pad(jnp.concatenate([router_g_b[l], router_e_b[l]]), (0, pad)).reshape(1, LANES)
        if l % 2 == 0:
            w, wo_a, wo_b = _prep_ab(w_in_ab[i], w_out_ab[i])
            a_qkv, b_qkv = _inproj(xs, w, ((0, A_IN), (A_IN, A_IN + B_IN)), (BF16, F32))
            o_a = _attn_a(a_qkv, bias_a, a_sink[i].astype(F32) * LOG2E, seg_starts, seg_ends)
            o_b = _attn_b(b_qkv, bias_b, seg_starts, seg_ends)
            parts, weights = [o_a, o_b], [wo_a, wo_b]
        else:
            w, wo = _prep_c(w_in_c[i], w_out_c[i])
            (c_qkv,) = _inproj(xs, w, ((0, 3 * C_W),), (BF16,))
            parts, weights = [_attn_c(c_qkv, _bias_c(c_rpb[i]), seg_starts, seg_ends)], [wo]
        x, info, counts = _outproj_ln(parts, weights, xs, ln1_g[l], ln1_b[l], w_router, b_router)
        xs = _moe_layer(l, x, info, counts, w_router, b_router, w_gate, w_up, w_down, ln2_g[l], ln2_b[l],
                        io_rows if l == DEPTH - 1 else (n,))
    return xs


def kernel(x_prompt, x_sample, rel_bias, w_in_ab, a_sink, w_out_ab, w_in_c, c_rpb, w_out_c,
           ln1_g, ln1_b, ln2_g, ln2_b, router_g_w, router_g_b, router_e_w, router_e_b,
           w_gate, w_up, w_down):
    lens = [x_prompt.shape[1]] * x_prompt.shape[0] + [x_sample.shape[1]] * x_sample.shape[0]
    seg_ends = tuple(int(v) for v in np.cumsum(lens))
    seg_starts = tuple(e - n for e, n in zip(seg_ends, lens))
    for n in lens:
        assert n % max(A_TB, B_TB, C_TB) == 0 and n // GRID_W >= NA_ROWS
    xs = [x_prompt.reshape(-1, D_MODEL), x_sample.reshape(-1, D_MODEL)]
    y_p, y_s = _trunk(xs, seg_starts, seg_ends, rel_bias, w_in_ab, a_sink, w_out_ab, w_in_c, c_rpb, w_out_c,
                      ln1_g, ln1_b, ln2_g, ln2_b, router_g_w, router_g_b, router_e_w, router_e_b,
                      w_gate, w_up, w_down)
    return (y_p.reshape(x_prompt.shape), y_s.reshape(x_sample.shape))
```
